```python
import jax
import jax.numpy as jnp
from jax import lax
import numpy as np


D_MODEL = 1024
BATCH = 8
SEQ = 2048
DEPTH = 4

HEAD_DIM = 64
MOBA_HEADS = 8
NSA_HEADS = 8
NSA_KV_HEADS = 2
NSA_GROUP = NSA_HEADS // NSA_KV_HEADS
MOBA_WIDTH = MOBA_HEADS * HEAD_DIM
NSA_WIDTH = NSA_HEADS * HEAD_DIM
NSA_KV_WIDTH = NSA_KV_HEADS * HEAD_DIM
MIX_WIDTH = MOBA_WIDTH + NSA_WIDTH
NSA_BRANCHES = 3
MOBA_BLOCK = 256
MOBA_TOPK = 3
CMP_LEN = 32
CMP_STRIDE = 16
CMP_HIDDEN = 256
SLC_BLOCK = 64
SLC_TOPK = 16
WINDOW = 512
Q_BLOCK = 128
D_FF = 2816
SPLIT_WIDTHS = [MOBA_WIDTH, MOBA_WIDTH, MOBA_WIDTH, NSA_WIDTH] + [NSA_KV_WIDTH] * 6 + [NSA_BRANCHES * NSA_HEADS]
IN_WIDTH = sum(SPLIT_WIDTHS)
SPLIT_POINTS = [int(v) for v in np.cumsum(SPLIT_WIDTHS)[:-1]]
NEG = -1e30
SLC_FORCE = 1e4
EPS = 1e-6

kernel_name = 'hybrid_moba_nsa_macaron_alibi'


def rms_norm(x, g):
    xf = x.astype(jnp.float32)
    y = xf * lax.rsqrt(jnp.mean(xf * xf, axis=-1, keepdims=True) + EPS)
    return (y * g.astype(jnp.float32)).astype(x.dtype)


def swiglu(x, w_gate, w_up, w_down):
    return (jax.nn.silu(x @ w_gate) * (x @ w_up)) @ w_down


def alibi_slopes(n):
    return jnp.asarray(2.0 ** (-8.0 * np.arange(1, n + 1) / n), dtype=jnp.float32)


def masked_softmax(scores, mask):
    s = jnp.where(mask, scores.astype(jnp.float32), NEG)
    m = jnp.max(s, axis=-1, keepdims=True)
    p = jnp.where(mask, jnp.exp(s - m), 0.0)
    return p / jnp.maximum(jnp.sum(p, axis=-1, keepdims=True), 1e-30)


def take_blocks(blocks, idx):
    return jax.vmap(lambda b_, i_: b_[i_])(blocks, idx)


def batch_qblock_ids(B, S):
    nq = S // Q_BLOCK
    return jnp.repeat(jnp.arange(B), nq), jnp.tile(jnp.arange(nq), B)


def moba_attention(q, k, v, slopes):
    B, H, S, dh = q.shape
    scale = HEAD_DIM ** -0.5
    nb = -(-S // MOBA_BLOCK)
    pad = nb * MOBA_BLOCK - S
    kb_all = jnp.pad(k, ((0, 0), (0, 0), (0, pad), (0, 0))).reshape(B, H, nb, MOBA_BLOCK, dh)
    vb_all = jnp.pad(v, ((0, 0), (0, 0), (0, pad), (0, 0))).reshape(B, H, nb, MOBA_BLOCK, dh)
    k_mean = jnp.mean(kb_all.astype(jnp.float32), axis=3)
    own = jnp.arange(S) // MOBA_BLOCK
    gate = jnp.einsum('bhsd,bhnd->bhsn', q.astype(jnp.float32), k_mean)
    cand = jnp.arange(nb)[None, :] < own[:, None]
    kk = max(1, min(MOBA_TOPK, nb - 1))
    _, sel = lax.top_k(jnp.where(cand, gate, NEG), kk)
    sel_ok = sel < own[:, None]
    blk = jnp.arange(MOBA_BLOCK)
    n_sel = kk * MOBA_BLOCK

    def step(ids):
        b, qb = ids
        q0 = qb * Q_BLOCK
        t = q0 + jnp.arange(Q_BLOCK)
        qc = lax.dynamic_slice_in_dim(q[b], q0, Q_BLOCK, axis=1)
        ic = lax.dynamic_slice_in_dim(sel[b], q0, Q_BLOCK, axis=1)
        okc = lax.dynamic_slice_in_dim(sel_ok[b], q0, Q_BLOCK, axis=1)
        kb, vb = kb_all[b], vb_all[b]
        k_sel = take_blocks(kb, ic)
        v_sel = take_blocks(vb, ic)
        ob = q0 // MOBA_BLOCK
        k_own = lax.dynamic_index_in_dim(kb, ob, axis=1, keepdims=False)
        v_own = lax.dynamic_index_in_dim(vb, ob, axis=1, keepdims=False)
        s_sel = ic[..., None] * MOBA_BLOCK + blk
        s_own = ob * MOBA_BLOCK + blk
        l_sel = (jnp.einsum('hqd,hqkjd->hqkj', qc, k_sel).astype(jnp.float32) * scale
                 - slopes[:, None, None, None] * jnp.abs(t[None, :, None, None] - s_sel).astype(jnp.float32))
        l_own = (jnp.einsum('hqd,hjd->hqj', qc, k_own).astype(jnp.float32) * scale
                 - slopes[:, None, None] * jnp.abs(t[:, None] - s_own[None, :]).astype(jnp.float32)[None])
        logits = jnp.concatenate([l_sel.reshape(H, Q_BLOCK, n_sel), l_own], axis=-1)
        m_sel = jnp.broadcast_to(okc[..., None], (H, Q_BLOCK, kk, MOBA_BLOCK)).reshape(H, Q_BLOCK, n_sel)
        m_own = jnp.broadcast_to((s_own[None, :] <= t[:, None])[None], (H, Q_BLOCK, MOBA_BLOCK))
        p = masked_softmax(logits, jnp.concatenate([m_sel, m_own], axis=-1))
        p_sel = p[..., :n_sel].reshape(H, Q_BLOCK, kk, MOBA_BLOCK)
        p_own = p[..., n_sel:]
        o = jnp.einsum('hqkj,hqkjd->hqd', p_sel, v_sel) + jnp.einsum('hqj,hjd->hqd', p_own, v_own)
        return o.astype(q.dtype)

    out = lax.map(step, batch_qblock_ids(B, S))
    nq = S // Q_BLOCK
    return out.reshape(B, nq, H, Q_BLOCK, dh).transpose(0, 2, 1, 3, 4).reshape(B, H, S, dh)


def compress_blocks(kv, win_idx, pos_emb, w1, w2):
    B, G, S, dh = kv.shape
    n = win_idx.shape[0]
    blocks = kv[:, :, win_idx] + pos_emb
    return jax.nn.gelu(blocks.reshape(B, G, n, CMP_LEN * dh) @ w1) @ w2


def nsa_attention(q, k_c, v_c, k_s, v_s, k_w, v_w, gates,
                  pos_k, k_w1, k_w2, pos_v, v_w1, v_w2, slopes):
    B, G, R, S, dh = q.shape
    scale = HEAD_DIM ** -0.5
    t_all = jnp.arange(S)
    n_cmp = (S - CMP_LEN) // CMP_STRIDE + 1
    cmp_start = np.arange(n_cmp) * CMP_STRIDE
    win_idx = cmp_start[:, None] + np.arange(CMP_LEN)[None, :]
    kc = compress_blocks(k_c, win_idx, pos_k, k_w1, k_w2)
    vc = compress_blocks(v_c, win_idx, pos_v, v_w1, v_w2)
    cmp_end = jnp.asarray(cmp_start + CMP_LEN - 1)
    d_cmp = jnp.abs(t_all[:, None] - cmp_end[None, :]).astype(jnp.float32)
    l_cmp = (jnp.einsum('bgrsd,bgnd->bgrsn', q, kc).astype(jnp.float32) * scale
             - slopes[None, :, :, None, None] * d_cmp)
    p_cmp = masked_softmax(l_cmp, cmp_end[None, :] <= t_all[:, None])
    o_cmp = jnp.einsum('bgrsn,bgnd->bgrsd', p_cmp, vc)
    n_slc = S // SLC_BLOCK
    c0 = cmp_start[:, None]
    j0 = (np.arange(n_slc) * SLC_BLOCK)[None, :]
    overlap = jnp.asarray((c0 < j0 + SLC_BLOCK) & (c0 + CMP_LEN > j0), dtype=jnp.float32)
    imp = jnp.einsum('bgrsn,nj->bgsj', p_cmp, overlap)
    tb = t_all // SLC_BLOCK
    jj = jnp.arange(n_slc)[None, :]
    cand = jj <= tb[:, None]
    forced = (jj == 0) | (jj == tb[:, None]) | (jj == tb[:, None] - 1)
    score = jnp.where(cand, jnp.where(forced, SLC_FORCE, imp), NEG)
    k_top = min(SLC_TOPK, n_slc)
    _, sel = lax.top_k(score, k_top)
    sel_ok = sel <= tb[:, None]
    ksb = k_s.reshape(B, G, n_slc, SLC_BLOCK, dh)
    vsb = v_s.reshape(B, G, n_slc, SLC_BLOCK, dh)
    kw_pad = jnp.pad(k_w, ((0, 0), (0, 0), (WINDOW, 0), (0, 0)))
    vw_pad = jnp.pad(v_w, ((0, 0), (0, 0), (WINDOW, 0), (0, 0)))
    blk = jnp.arange(SLC_BLOCK)
    n_sel = k_top * SLC_BLOCK

    def step(ids):
        b, qb = ids
        q0 = qb * Q_BLOCK
        t = q0 + jnp.arange(Q_BLOCK)
        qc = lax.dynamic_slice_in_dim(q[b], q0, Q_BLOCK, axis=2)
        ic = lax.dynamic_slice_in_dim(sel[b], q0, Q_BLOCK, axis=1)
        okc = lax.dynamic_slice_in_dim(sel_ok[b], q0, Q_BLOCK, axis=1)
        k_sel = take_blocks(ksb[b], ic).reshape(G, Q_BLOCK, n_sel, dh)
        v_sel = take_blocks(vsb[b], ic).reshape(G, Q_BLOCK, n_sel, dh)
        dist = (t[None, :, None, None] - (ic[..., None] * SLC_BLOCK + blk)).reshape(G, Q_BLOCK, n_sel)
        l_s = (jnp.einsum('grqd,gqnd->grqn', qc, k_sel).astype(jnp.float32) * scale
               - slopes[:, :, None, None] * jnp.abs(dist).astype(jnp.float32)[:, None])
        m_s = jnp.broadcast_to(okc[..., None], (G, Q_BLOCK, k_top, SLC_BLOCK)).reshape(G, Q_BLOCK, n_sel) & (dist >= 0)
        p_s = masked_softmax(l_s, m_s[:, None])
        o_s = jnp.einsum('grqn,gqnd->grqd', p_s, v_sel)
        kwc = lax.dynamic_slice_in_dim(kw_pad[b], q0, Q_BLOCK + WINDOW, axis=1)
        vwc = lax.dynamic_slice_in_dim(vw_pad[b], q0, Q_BLOCK + WINDOW, axis=1)
        s_w = q0 - WINDOW + jnp.arange(Q_BLOCK + WINDOW)
        dw = t[:, None] - s_w[None, :]
        m_w = (dw >= 0) & (dw < WINDOW) & (s_w[None, :] >= 0)
        l_w = (jnp.einsum('grqd,gjd->grqj', qc, kwc).astype(jnp.float32) * scale
               - slopes[:, :, None, None] * jnp.abs(dw).astype(jnp.float32)[None, None])
        p_w = masked_softmax(l_w, m_w)
        o_w = jnp.einsum('grqj,gjd->grqd', p_w, vwc)
        return o_s.astype(q.dtype), o_w.astype(q.dtype)

    o_s, o_w = lax.map(step, batch_qblock_ids(B, S))
    nq = S // Q_BLOCK
    o_s = o_s.reshape(B, nq, G, R, Q_BLOCK, dh).transpose(0, 2, 3, 1, 4, 5).reshape(B, G, R, S, dh)
    o_w = o_w.reshape(B, nq, G, R, Q_BLOCK, dh).transpose(0, 2, 3, 1, 4, 5).reshape(B, G, R, S, dh)
    o = gates[..., 0:1] * o_cmp + gates[..., 1:2] * o_s + gates[..., 2:3] * o_w
    return o.astype(q.dtype)


def hybrid_mixer(h, w_in, cmp_pos_k, cmp_k_w1, cmp_k_w2, cmp_pos_v, cmp_v_w1, cmp_v_w2,
                 moba_out_norm, nsa_out_norm, w_out):
    B, S, _ = h.shape
    proj = h @ w_in
    q_m, k_m, v_m, q_n, k_c, v_c, k_s, v_s, k_w, v_w, g = jnp.split(proj, SPLIT_POINTS, axis=-1)

    def heads(z, n):
        return z.reshape(B, S, n, HEAD_DIM).transpose(0, 2, 1, 3)

    o_m = moba_attention(heads(q_m, MOBA_HEADS), heads(k_m, MOBA_HEADS), heads(v_m, MOBA_HEADS),
                         alibi_slopes(MOBA_HEADS))
    qn = heads(q_n, NSA_HEADS).reshape(B, NSA_KV_HEADS, NSA_GROUP, S, HEAD_DIM)
    gates = jax.nn.sigmoid(g.astype(jnp.float32)).reshape(B, S, NSA_KV_HEADS, NSA_GROUP, NSA_BRANCHES).transpose(0, 2, 3, 1, 4)
    o_n = nsa_attention(qn, heads(k_c, NSA_KV_HEADS), heads(v_c, NSA_KV_HEADS),
                        heads(k_s, NSA_KV_HEADS), heads(v_s, NSA_KV_HEADS),
                        heads(k_w, NSA_KV_HEADS), heads(v_w, NSA_KV_HEADS), gates,
                        cmp_pos_k, cmp_k_w1, cmp_k_w2, cmp_pos_v, cmp_v_w1, cmp_v_w2,
                        alibi_slopes(NSA_HEADS).reshape(NSA_KV_HEADS, NSA_GROUP))
    o_m = o_m.transpose(0, 2, 1, 3).reshape(B, S, MOBA_WIDTH)
    o_n = o_n.transpose(0, 3, 1, 2, 4).reshape(B, S, NSA_WIDTH)
    y = jnp.concatenate([rms_norm(o_m, moba_out_norm), rms_norm(o_n, nsa_out_norm)], axis=-1)
    return y @ w_out


def setup_inputs(seed: int = 0) -> dict:
    key = jax.random.key(seed)
    ks = jax.random.split(key, 22)
    f32 = jnp.float32
    L = DEPTH

    def w(k, shape, fan_in):
        return jax.random.normal(k, shape, f32) * (fan_in ** -0.5)

    def gain(k, shape):
        return 1.0 + 0.02 * jax.random.normal(k, shape, f32)

    return {
        'x': jax.random.normal(ks[0], (BATCH, SEQ, D_MODEL), f32),
        'ffa_norm': gain(ks[1], (L, D_MODEL)),
        'ffa_w_gate': w(ks[2], (L, D_MODEL, D_FF), D_MODEL),
        'ffa_w_up': w(ks[3], (L, D_MODEL, D_FF), D_MODEL),
        'ffa_w_down': w(ks[4], (L, D_FF, D_MODEL), D_FF),
        'mix_norm': gain(ks[5], (L, D_MODEL)),
        'w_in': w(ks[6], (L, D_MODEL, IN_WIDTH), D_MODEL),
        'cmp_pos_k': 0.1 * jax.random.normal(ks[7], (L, CMP_LEN, HEAD_DIM), f32),
        'cmp_k_w1': w(ks[8], (L, CMP_LEN * HEAD_DIM, CMP_HIDDEN), CMP_LEN * HEAD_DIM),
        'cmp_k_w2': w(ks[9], (L, CMP_HIDDEN, HEAD_DIM), CMP_HIDDEN),
        'cmp_pos_v': 0.1 * jax.random.normal(ks[10], (L, CMP_LEN, HEAD_DIM), f32),
        'cmp_v_w1': w(ks[11], (L, CMP_LEN * HEAD_DIM, CMP_HIDDEN), CMP_LEN * HEAD_DIM),
        'cmp_v_w2': w(ks[12], (L, CMP_HIDDEN, HEAD_DIM), CMP_HIDDEN),
        'moba_out_norm': gain(ks[13], (L, MOBA_WIDTH)),
        'nsa_out_norm': gain(ks[14], (L, NSA_WIDTH)),
        'w_out': w(ks[15], (L, MIX_WIDTH, D_MODEL), MIX_WIDTH),
        'ffb_norm': gain(ks[16], (L, D_MODEL)),
        'ffb_w_gate': w(ks[17], (L, D_MODEL, D_FF), D_MODEL),
        'ffb_w_up': w(ks[18], (L, D_MODEL, D_FF), D_MODEL),
        'ffb_w_down': w(ks[19], (L, D_FF, D_MODEL), D_FF),
        'final_norm': gain(ks[20], (D_MODEL,)),
    }


def reference(x, ffa_norm, ffa_w_gate, ffa_w_up, ffa_w_down, mix_norm, w_in,
              cmp_pos_k, cmp_k_w1, cmp_k_w2, cmp_pos_v, cmp_v_w1, cmp_v_w2,
              moba_out_norm, nsa_out_norm, w_out,
              ffb_norm, ffb_w_gate, ffb_w_up, ffb_w_down, final_norm):
    for l in range(DEPTH):
        x = x + 0.5 * swiglu(rms_norm(x, ffa_norm[l]), ffa_w_gate[l], ffa_w_up[l], ffa_w_down[l])
        x = x + hybrid_mixer(rms_norm(x, mix_norm[l]), w_in[l],
                             cmp_pos_k[l], cmp_k_w1[l], cmp_k_w2[l],
                             cmp_pos_v[l], cmp_v_w1[l], cmp_v_w2[l],
                             moba_out_norm[l], nsa_out_norm[l], w_out[l])
        x = x + 0.5 * swiglu(rms_norm(x, ffb_norm[l]), ffb_w_gate[l], ffb_w_up[l], ffb_w_down[l])
    return rms_norm(x, final_norm)
```

```python
import functools

import numpy as np
import jax
import jax.numpy as jnp
from jax import lax
from jax.experimental import pallas as pl
from jax.experimental.pallas import tpu as pltpu

HEAD_DIM = 64
MOBA_HEADS = 8
NSA_HEADS = 8
NSA_KV_HEADS = 2
NSA_GROUP = NSA_HEADS // NSA_KV_HEADS
NSA_BRANCHES = 3
MOBA_BLOCK = 256
MOBA_TOPK = 3
CMP_LEN = 32
CMP_STRIDE = 16
SLC_BLOCK = 64
SLC_TOPK = 16
WINDOW = 512
NEG = -1e30
SLC_FORCE = 1e4
EPS = 1e-6
SCALE = HEAD_DIM ** -0.5

LANES = 128
HEADS_PER_TILE = LANES // HEAD_DIM
COL_QM, COL_KM, COL_VM, COL_QN = 0, 4, 8, 12
COL_KC, COL_VC, COL_KS, COL_VS, COL_KW, COL_VW, COL_G = 16, 17, 18, 19, 20, 21, 22
IN_TILES = 24
IN_PAD = IN_TILES * LANES
IN_WIDTH = 2840

MXU_DTYPE = jnp.bfloat16
VMEM_LIMIT = 48 * 1024 * 1024
F32 = jnp.float32
HI = lax.Precision.HIGHEST


def _dot(a, b):
    return jnp.dot(a.astype(MXU_DTYPE), b.astype(MXU_DTYPE), preferred_element_type=F32)


def _dot_nt(a, b, precision=None):
    return lax.dot_general(a, b, (((1,), (1,)), ((), ())), precision=precision,
                           preferred_element_type=F32)


def _dot_hi(a, b):
    return jnp.dot(a, b, precision=HI, preferred_element_type=F32)


def _iota(shape, dim):
    return lax.broadcasted_iota(jnp.int32, shape, dim)


def _div_pow2(x, n):
    assert n & (n - 1) == 0
    return x >> (n.bit_length() - 1)


def _params(*sem):
    return pltpu.CompilerParams(dimension_semantics=sem, vmem_limit_bytes=VMEM_LIMIT)


def _rms(x, g):
    return x * lax.rsqrt(jnp.mean(x * x, axis=-1, keepdims=True) + EPS) * g


def _ffn_kernel(x_ref, g_ref, wg_ref, wu_ref, wd_ref, o_ref, h_ref, acc_ref):
    j = pl.program_id(1)

    @pl.when(j == 0)
    def _():
        h_ref[...] = _rms(x_ref[...], g_ref[...]).astype(h_ref.dtype)
        acc_ref[...] = jnp.zeros_like(acc_ref)

    h = h_ref[...]
    a = jnp.dot(h, wg_ref[...], preferred_element_type=F32)
    b = jnp.dot(h, wu_ref[...], preferred_element_type=F32)
    u = a / (1.0 + jnp.exp(-a)) * b
    acc_ref[...] += _dot(u, wd_ref[...])

    @pl.when(j == pl.num_programs(1) - 1)
    def _():
        o_ref[...] = x_ref[...] + 0.5 * acc_ref[...]


def _ffn(x, g, wg, wu, wd, tm, tf):
    T, D = x.shape
    F = wg.shape[1]
    return pl.pallas_call(
        _ffn_kernel,
        grid=(T // tm, F // tf),
        in_specs=[
            pl.BlockSpec((tm, D), lambda i, j: (i, 0)),
            pl.BlockSpec((1, D), lambda i, j: (0, 0)),
            pl.BlockSpec((D, tf), lambda i, j: (0, j)),
            pl.BlockSpec((D, tf), lambda i, j: (0, j)),
            pl.BlockSpec((tf, D), lambda i, j: (j, 0)),
        ],
        out_specs=pl.BlockSpec((tm, D), lambda i, j: (i, 0)),
        out_shape=jax.ShapeDtypeStruct((T, D), F32),
        scratch_shapes=[pltpu.VMEM((tm, D), MXU_DTYPE), pltpu.VMEM((tm, D), F32)],
        compiler_params=_params("parallel", "arbitrary"),
        name="ffn",
    )(x, g, wg, wu, wd)


def _inproj_kernel(x_ref, g_ref, w_ref, o_ref, h_ref):
    @pl.when(pl.program_id(1) == 0)
    def _():
        h_ref[...] = _rms(x_ref[...], g_ref[...]).astype(h_ref.dtype)

    o_ref[...] = jnp.dot(h_ref[...], w_ref[...], preferred_element_type=F32)


def _inproj(x, g, w, tm, tn):
    T, D = x.shape
    N = w.shape[1]
    return pl.pallas_call(
        _inproj_kernel,
        grid=(T // tm, N // tn),
        in_specs=[
            pl.BlockSpec((tm, D), lambda i, j: (i, 0)),
            pl.BlockSpec((1, D), lambda i, j: (0, 0)),
            pl.BlockSpec((D, tn), lambda i, j: (0, j)),
        ],
        out_specs=pl.BlockSpec((tm, tn), lambda i, j: (i, j)),
        out_shape=jax.ShapeDtypeStruct((T, N), F32),
        scratch_shapes=[pltpu.VMEM((tm, D), MXU_DTYPE)],
        compiler_params=_params("parallel", "arbitrary"),
        name="inproj",
    )(x, g, w)


def _softmax_first(s, v, m_ref, l_ref, acc_ref):
    m = jnp.max(s, axis=1, keepdims=True)
    p = jnp.exp(s - m)
    m_ref[...] = m
    l_ref[...] = jnp.sum(p, axis=1, keepdims=True)
    acc_ref[...] = _dot(p, v)


def _softmax_next(s, v, m_ref, l_ref, acc_ref):
    m_old = m_ref[...]
    m_new = jnp.maximum(m_old, jnp.max(s, axis=1, keepdims=True))
    alpha = jnp.exp(m_old - m_new)
    p = jnp.exp(s - m_new)
    m_ref[...] = m_new
    l_ref[...] = alpha * l_ref[...] + jnp.sum(p, axis=1, keepdims=True)
    acc_ref[...] = alpha * acc_ref[...] + _dot(p, v)


def _rank_before(score, n, lane):
    rank = jnp.zeros(score.shape, jnp.int32)
    for m in range(n):
        col = score[:, m:m + 1]
        better = (col > score) | ((col == score) & (m < lane))
        rank = rank + better.astype(jnp.int32)
    return rank


def _moba_kernel(slopes_ref, q_ref, k_ref, v_ref, o_ref,
                 kmean_ref, kb_ref, vb_ref, m_ref, l_ref, acc_ref, *, nb, tq, kk):
    hp = pl.program_id(1)
    qi = pl.program_id(2)
    blk = MOBA_BLOCK

    @pl.when(qi == 0)
    def _():
        kmean_ref[...] = jnp.zeros_like(kmean_ref)
        for n in range(nb):
            kmean_ref[n:n + 1, :] = jnp.mean(k_ref[0, n * blk:(n + 1) * blk, :], axis=0, keepdims=True)
        kb_ref[...] = k_ref[0].astype(kb_ref.dtype)
        vb_ref[...] = v_ref[0].astype(vb_ref.dtype)

    q0 = qi * tq
    own = _div_pow2(q0, blk)
    rows = HEADS_PER_TILE * tq

    lo = _iota((tq, LANES), 1) < HEAD_DIM
    q = q_ref[0]
    qs = jnp.concatenate([jnp.where(lo, q, 0.0), jnp.where(lo, 0.0, q)], axis=0)

    gate = _dot_nt(qs, kmean_ref[...], precision=HI)
    lane = _iota((rows, LANES), 1)
    cand = lane < own
    gm = jnp.where(cand, gate, NEG)
    sel = (cand & (_rank_before(gm, nb - 1, lane) < kk)).astype(MXU_DTYPE)

    qsb = (qs * SCALE).astype(MXU_DTYPE)
    row1 = _iota((rows, 1), 0)
    slope = jnp.where(row1 < tq, slopes_ref[HEADS_PER_TILE * hp], slopes_ref[HEADS_PER_TILE * hp + 1])
    rel = _iota((rows, blk), 1) - (_iota((rows, blk), 0) & (tq - 1))
    bias = slope * rel.astype(F32)

    def offset(n):
        return jnp.full((rows, 1), n * blk - q0, jnp.int32)

    off = offset(own)
    k = kb_ref[pl.ds(pl.multiple_of(own * blk, blk), blk), :]
    v = vb_ref[pl.ds(pl.multiple_of(own * blk, blk), blk), :]
    s = _dot_nt(qsb, k) + bias + slope * off.astype(F32)
    s = jnp.where(rel + off <= 0, s, NEG)
    _softmax_first(s, v, m_ref, l_ref, acc_ref)

    def past(n, carry):
        k = kb_ref[pl.ds(pl.multiple_of(n * blk, blk), blk), :]
        v = vb_ref[pl.ds(pl.multiple_of(n * blk, blk), blk), :]
        s = _dot_nt(qsb, k) + bias + slope * offset(n).astype(F32)
        pick = (_iota((LANES, blk), 0) == n).astype(MXU_DTYPE)
        chosen = jnp.dot(sel, pick, preferred_element_type=F32)
        s = jnp.where(chosen > 0.5, s, NEG)
        _softmax_next(s, v, m_ref, l_ref, acc_ref)
        return carry

    lax.fori_loop(0, own, past, 0)

    o = acc_ref[...] / l_ref[...]
    o_ref[0] = jnp.where(lo, o[0:tq], o[tq:rows])


def _moba(slopes, proj, tq):
    B, S, _ = proj.shape
    nb = S // MOBA_BLOCK
    kk = max(1, min(MOBA_TOPK, nb - 1))
    rows = HEADS_PER_TILE * tq
    n_tiles = MOBA_HEADS // HEADS_PER_TILE
    kern = functools.partial(_moba_kernel, nb=nb, tq=tq, kk=kk)
    return pl.pallas_call(
        kern,
        grid=(B, n_tiles, S // tq),
        in_specs=[
            pl.BlockSpec(memory_space=pltpu.SMEM),
            pl.BlockSpec((1, tq, LANES), lambda b, h, i: (b, i, COL_QM + h)),
            pl.BlockSpec((1, S, LANES), lambda b, h, i: (b, 0, COL_KM + h)),
            pl.BlockSpec((1, S, LANES), lambda b, h, i: (b, 0, COL_VM + h)),
        ],
        out_specs=pl.BlockSpec((1, tq, LANES), lambda b, h, i: (b, i, h)),
        out_shape=jax.ShapeDtypeStruct((B, S, MOBA_HEADS * HEAD_DIM), F32),
        scratch_shapes=[
            pltpu.VMEM((LANES, LANES), F32),
            pltpu.VMEM((S, LANES), MXU_DTYPE),
            pltpu.VMEM((S, LANES), MXU_DTYPE),
            pltpu.VMEM((rows, 1), F32),
            pltpu.VMEM((rows, 1), F32),
            pltpu.VMEM((rows, LANES), F32),
        ],
        compiler_params=_params("parallel", "parallel", "arbitrary"),
        name="moba",
    )(slopes, proj, proj, proj)


def _gelu_tanh(x):
    return x * (0.5 * (1.0 + jnp.tanh(0.7978845608028654 * (x + 0.044715 * (x * x * x)))))


def _compress_kernel(xk_ref, xv_ref, pk_ref, pv_ref, w1k_ref, w1v_ref, w2k_ref, w2v_ref,
                     kc_ref, vc_ref, *, nr):
    half = (CMP_LEN // 2) * HEAD_DIM
    for x_ref, p_ref, w1_ref, w2_ref, o_ref in ((xk_ref, pk_ref, w1k_ref, w2k_ref, kc_ref),
                                               (xv_ref, pv_ref, w1v_ref, w2v_ref, vc_ref)):
        out = jnp.zeros((nr, LANES), F32)
        for g in range(NSA_KV_HEADS):
            x = x_ref[0, g]
            top = _dot_hi(x + p_ref[:, 0:half], w1_ref[0:half, :])
            bot = _dot_hi(x + p_ref[:, half:2 * half], w1_ref[half:2 * half, :])
            hid = top + pltpu.roll(bot, nr - 1, 0)
            out = out + _dot_hi(_gelu_tanh(hid), w2_ref[g])
        o_ref[0] = out


def _compress(xk, xv, pk, pv, w1k, w1v, w2k, w2v):
    B, G, nr, W = xk.shape
    H = w1k.shape[1]
    kern = functools.partial(_compress_kernel, nr=nr)
    full = lambda *shape: pl.BlockSpec(shape, lambda b: (0,) * len(shape))
    xspec = pl.BlockSpec((1, G, nr, W), lambda b: (b, 0, 0, 0))
    ospec = pl.BlockSpec((1, nr, LANES), lambda b: (b, 0, 0))
    oshape = jax.ShapeDtypeStruct((B, nr, LANES), F32)
    return pl.pallas_call(
        kern,
        grid=(B,),
        in_specs=[xspec, xspec, full(1, 2 * W), full(1, 2 * W), full(2 * W, H), full(2 * W, H),
                  full(G, H, LANES), full(G, H, LANES)],
        out_specs=(ospec, ospec),
        out_shape=(oshape, oshape),
        compiler_params=_params("parallel"),
        name="nsa_compress",
    )(xk, xv, pk, pv, w1k, w1v, w2k, w2v)


def _nsa_kernel(slopes_ref, qa_ref, qb_ref, g_ref, kc_ref, vc_ref, ks_ref, vs_ref, kw_ref, vw_ref,
                o_ref, kcb, vcb, ksb, vsb, kwb, vwb, selm_ref, m_ref, l_ref, acc_ref,
                ocmp_ref, oslc_ref, *, tq, n_cmp, n_slc, k_top):
    g = pl.program_id(1)
    qi = pl.program_id(2)
    q0 = qi * tq
    rows = NSA_GROUP * tq
    sblk = 4 * SLC_BLOCK
    per_chunk = sblk // SLC_BLOCK

    @pl.when(qi == 0)
    def _():
        def both_halves(ref):
            x = ref[0]
            mine = (_iota(x.shape, 1) >= HEAD_DIM) == (g == 1)
            return jnp.where(mine, x, pltpu.roll(x, HEAD_DIM, 1)).astype(MXU_DTYPE)

        kcb[...] = both_halves(kc_ref)
        vcb[...] = both_halves(vc_ref)
        ksb[...] = both_halves(ks_ref)
        vsb[...] = both_halves(vs_ref)
        kwb[...] = both_halves(kw_ref)
        vwb[...] = both_halves(vw_ref)

    lo = _iota((tq, LANES), 1) < HEAD_DIM
    qa = qa_ref[0] * SCALE
    qb = qb_ref[0] * SCALE
    qs = jnp.concatenate([jnp.where(lo, qa, 0.0), jnp.where(lo, 0.0, qa),
                          jnp.where(lo, qb, 0.0), jnp.where(lo, 0.0, qb)], axis=0).astype(MXU_DTYPE)
    row1 = _iota((rows, 1), 0)
    h0 = NSA_GROUP * g
    slope = jnp.where(row1 < tq, slopes_ref[h0],
                      jnp.where(row1 < 2 * tq, slopes_ref[h0 + 1],
                                jnp.where(row1 < 3 * tq, slopes_ref[h0 + 2], slopes_ref[h0 + 3])))
    i1 = row1 & (tq - 1)
    t1 = q0 + i1

    ncol = _iota((rows, LANES), 1)
    cend = ncol * CMP_STRIDE + (CMP_LEN - 1)
    maskc = (cend <= t1) & (ncol < n_cmp)
    s = _dot_nt(qs, kcb[...]) + slope * (cend - t1).astype(F32)
    s = jnp.where(maskc, s, NEG)
    p = jnp.where(maskc, jnp.exp(s - jnp.max(s, axis=1, keepdims=True)), 0.0)
    p = p / jnp.maximum(jnp.sum(p, axis=1, keepdims=True), 1e-30)
    ocmp_ref[...] = _dot(p, vcb[...])

    psum = p[0:tq] + p[tq:2 * tq] + p[2 * tq:3 * tq] + p[3 * tq:4 * tq]
    nn = _iota((LANES, LANES), 0) * CMP_STRIDE
    jj = _iota((LANES, LANES), 1) * SLC_BLOCK
    overlap = ((nn < jj + SLC_BLOCK) & (nn + CMP_LEN > jj)
               & (nn < n_cmp * CMP_STRIDE) & (jj < n_slc * SLC_BLOCK)).astype(F32)
    imp = _dot_hi(psum, overlap)
    jl = _iota((tq, LANES), 1)
    tb = _div_pow2(q0 + _iota((tq, LANES), 0), SLC_BLOCK)
    cand = jl <= tb
    forced = (jl == 0) | (jl == tb) | (jl == tb - 1)
    score = jnp.where(cand, jnp.where(forced, SLC_FORCE, imp), NEG)
    selm_ref[...] = cand.astype(F32)

    @pl.when(q0 + tq > k_top * SLC_BLOCK)
    def _():
        selm_ref[...] = (cand & (_rank_before(score, n_slc, jl) < k_top)).astype(F32)

    selb = selm_ref[...].astype(MXU_DTYPE)

    rel = _iota((rows, sblk), 1) - i1
    bias = slope * rel.astype(F32)

    def slc_scores(c):
        k = ksb[pl.ds(pl.multiple_of(c * sblk, sblk), sblk), :]
        v = vsb[pl.ds(pl.multiple_of(c * sblk, sblk), sblk), :]
        off = jnp.full((rows, 1), c * sblk - q0, jnp.int32)
        s = _dot_nt(qs, k) + bias + slope * off.astype(F32)
        expand = ((_div_pow2(_iota((LANES, sblk), 1), SLC_BLOCK) + per_chunk * c)
                  == _iota((LANES, sblk), 0)).astype(MXU_DTYPE)
        chosen = jnp.dot(selb, expand, preferred_element_type=F32) > 0.5
        chosen = jnp.concatenate([chosen] * NSA_GROUP, axis=0)
        return s, v, chosen, off

    dc = _div_pow2(q0, sblk)
    s, v, chosen, off = slc_scores(dc)
    s = jnp.where(chosen & (rel + off <= 0), s, NEG)
    _softmax_first(s, v, m_ref, l_ref, acc_ref)

    def slc_past(c, carry):
        s, v, chosen, _ = slc_scores(c)
        _softmax_next(jnp.where(chosen, s, NEG), v, m_ref, l_ref, acc_ref)
        return carry

    lax.fori_loop(0, dc, slc_past, 0)
    oslc_ref[...] = acc_ref[...] / l_ref[...]

    relw = _iota((rows, tq), 1) - i1
    biasw = slope * relw.astype(F32)
    n_back = WINDOW // tq

    def win_scores(d):
        start = pl.multiple_of(q0 - d * tq, tq)
        k = kwb[pl.ds(start, tq), :]
        v = vwb[pl.ds(start, tq), :]
        return _dot_nt(qs, k) + biasw - slope * float(d * tq), v

    s, v = win_scores(0)
    _softmax_first(jnp.where(relw <= 0, s, NEG), v, m_ref, l_ref, acc_ref)
    for d in range(1, n_back + 1):
        @pl.when(qi >= d)
        def _(d=d):
            s, v = win_scores(d)
            if d == n_back:
                s = jnp.where(relw > 0, s, NEG)
            _softmax_next(s, v, m_ref, l_ref, acc_ref)

    owin = acc_ref[...] / l_ref[...]

    z = g_ref[0]
    gates = 1.0 / (1.0 + jnp.exp(-z))
    per_group = NSA_GROUP * NSA_BRANCHES
    gates = jnp.where(g == 0, gates, pltpu.roll(gates, LANES - per_group, 1))
    ocmp = ocmp_ref[...]
    oslc = oslc_ref[...]
    heads = []
    for r in range(NSA_GROUP):
        sl = slice(r * tq, (r + 1) * tq)
        c = NSA_BRANCHES * r
        heads.append(gates[:, c:c + 1] * ocmp[sl] + gates[:, c + 1:c + 2] * oslc[sl]
                     + gates[:, c + 2:c + 3] * owin[sl])
    o_ref[0, :, 0:LANES] = jnp.where(lo, heads[0], heads[1])
    o_ref[0, :, LANES:2 * LANES] = jnp.where(lo, heads[2], heads[3])


def _nsa(slopes, proj, kc, vc, tq):
    B, S, _ = proj.shape
    n_cmp = (S - CMP_LEN) // CMP_STRIDE + 1
    n_slc = S // SLC_BLOCK
    k_top = min(SLC_TOPK, n_slc)
    nr = kc.shape[1]
    rows = NSA_GROUP * tq
    kern = functools.partial(_nsa_kernel, tq=tq, n_cmp=n_cmp, n_slc=n_slc, k_top=k_top)
    qtiles = NSA_GROUP // HEADS_PER_TILE

    def seq(col):
        return pl.BlockSpec((1, S, LANES), lambda b, g, i: (b, 0, col))

    small = pl.BlockSpec((1, nr, LANES), lambda b, g, i: (b, 0, 0))
    return pl.pallas_call(
        kern,
        grid=(B, NSA_KV_HEADS, S // tq),
        in_specs=[
            pl.BlockSpec(memory_space=pltpu.SMEM),
            pl.BlockSpec((1, tq, LANES), lambda b, g, i: (b, i, COL_QN + qtiles * g)),
            pl.BlockSpec((1, tq, LANES), lambda b, g, i: (b, i, COL_QN + qtiles * g + 1)),
            pl.BlockSpec((1, tq, LANES), lambda b, g, i: (b, i, COL_G)),
            small, small, seq(COL_KS), seq(COL_VS), seq(COL_KW), seq(COL_VW),
        ],
        out_specs=pl.BlockSpec((1, tq, qtiles * LANES), lambda b, g, i: (b, i, g)),
        out_shape=jax.ShapeDtypeStruct((B, S, NSA_HEADS * HEAD_DIM), F32),
        scratch_shapes=[
            pltpu.VMEM((nr, LANES), MXU_DTYPE), pltpu.VMEM((nr, LANES), MXU_DTYPE),
            pltpu.VMEM((S, LANES), MXU_DTYPE), pltpu.VMEM((S, LANES), MXU_DTYPE),
            pltpu.VMEM((S, LANES), MXU_DTYPE), pltpu.VMEM((S, LANES), MXU_DTYPE),
            pltpu.VMEM((tq, LANES), F32),
            pltpu.VMEM((rows, 1), F32), pltpu.VMEM((rows, 1), F32), pltpu.VMEM((rows, LANES), F32),
            pltpu.VMEM((rows, LANES), F32), pltpu.VMEM((rows, LANES), F32),
        ],
        compiler_params=_params("parallel", "parallel", "arbitrary"),
        name="nsa",
    )(slopes, proj, proj, proj, kc, vc, proj, proj, proj, proj)


def _outproj_kernel(om_ref, on_ref, x_ref, gm_ref, gn_ref, wm_ref, wn_ref, o_ref):
    y = _dot(_rms(om_ref[...], gm_ref[...]), wm_ref[...]) + _dot(_rms(on_ref[...], gn_ref[...]), wn_ref[...])
    o_ref[...] = x_ref[...] + y


def _outproj(om, on, x, gm, gn, wm, wn, tm):
    T, D = x.shape
    Wm, Wn = om.shape[1], on.shape[1]
    return pl.pallas_call(
        _outproj_kernel,
        grid=(T // tm,),
        in_specs=[
            pl.BlockSpec((tm, Wm), lambda i: (i, 0)),
            pl.BlockSpec((tm, Wn), lambda i: (i, 0)),
            pl.BlockSpec((tm, D), lambda i: (i, 0)),
            pl.BlockSpec((1, Wm), lambda i: (0, 0)),
            pl.BlockSpec((1, Wn), lambda i: (0, 0)),
            pl.BlockSpec((Wm, D), lambda i: (0, 0)),
            pl.BlockSpec((Wn, D), lambda i: (0, 0)),
        ],
        out_specs=pl.BlockSpec((tm, D), lambda i: (i, 0)),
        out_shape=jax.ShapeDtypeStruct((T, D), F32),
        compiler_params=_params("parallel"),
        name="outproj",
    )(om, on, x, gm, gn, wm, wn)


def _norm_kernel(x_ref, g_ref, o_ref):
    o_ref[...] = _rms(x_ref[...], g_ref[...])


def _final_norm(x, g, tm):
    T, D = x.shape
    return pl.pallas_call(
        _norm_kernel,
        grid=(T // tm,),
        in_specs=[pl.BlockSpec((tm, D), lambda i: (i, 0)), pl.BlockSpec((1, D), lambda i: (0, 0))],
        out_specs=pl.BlockSpec((tm, D), lambda i: (i, 0)),
        out_shape=jax.ShapeDtypeStruct((T, D), F32),
        compiler_params=_params("parallel"),
        name="final_norm",
    )(x, g)


def _alibi_slopes(n):
    return jnp.asarray(2.0 ** (-8.0 * np.arange(1, n + 1) / n), dtype=F32)


def _token_tile(T, want):
    return want if T % want == 0 else T


def _mixer(x, B, S, mix_norm, w_in, pos_k, k_w1, k_w2, pos_v, v_w1, v_w2, moba_norm, nsa_norm, w_out):
    T, D = x.shape
    w_in_p = jnp.pad(w_in, ((0, 0), (0, IN_PAD - w_in.shape[1]))).astype(MXU_DTYPE)
    proj = _inproj(x, mix_norm[None], w_in_p, _token_tile(T, 1024), IN_PAD // 3).reshape(B, S, IN_PAD)

    o_m = _moba(_alibi_slopes(MOBA_HEADS), proj, MOBA_BLOCK)

    nr = S // CMP_STRIDE

    def windows(col):
        z = proj[:, :, col * LANES:(col + 1) * LANES].reshape(B, S, NSA_KV_HEADS, HEAD_DIM)
        return z.transpose(0, 2, 1, 3).reshape(B, NSA_KV_HEADS, nr, CMP_STRIDE * HEAD_DIM)

    def placed(w2):
        return jnp.stack([jnp.pad(w2, ((0, 0), (g * HEAD_DIM, LANES - (g + 1) * HEAD_DIM)))
                          for g in range(NSA_KV_HEADS)])

    kc, vc = _compress(windows(COL_KC), windows(COL_VC), pos_k.reshape(1, -1), pos_v.reshape(1, -1),
                       k_w1, v_w1, placed(k_w2), placed(v_w2))
    o_n = _nsa(_alibi_slopes(NSA_HEADS), proj, kc, vc, 128)

    wm = w_out[:o_m.shape[-1]].astype(MXU_DTYPE)
    wn = w_out[o_m.shape[-1]:].astype(MXU_DTYPE)
    return _outproj(o_m.reshape(T, -1), o_n.reshape(T, -1), x, moba_norm[None], nsa_norm[None], wm, wn,
                    _token_tile(T, 512))


def _swiglu_step(x, norm, w_gate, w_up, w_down):
    T = x.shape[0]
    return _ffn(x, norm[None], w_gate.astype(MXU_DTYPE), w_up.astype(MXU_DTYPE), w_down.astype(MXU_DTYPE),
                _token_tile(T, 1024), 256)


@jax.jit
def kernel(x, ffa_norm, ffa_w_gate, ffa_w_up, ffa_w_down, mix_norm, w_in, cmp_pos_k, cmp_k_w1, cmp_k_w2,
           cmp_pos_v, cmp_v_w1, cmp_v_w2, moba_out_norm, nsa_out_norm, w_out, ffb_norm, ffb_w_gate,
           ffb_w_up, ffb_w_down, final_norm):
    B, S, D = x.shape
    assert S % MOBA_BLOCK == 0 and S >= WINDOW and w_in.shape[-1] == IN_WIDTH
    h = x.reshape(B * S, D)
    for l in range(ffa_norm.shape[0]):
        h = _swiglu_step(h, ffa_norm[l], ffa_w_gate[l], ffa_w_up[l], ffa_w_down[l])
        h = _mixer(h, B, S, mix_norm[l], w_in[l], cmp_pos_k[l], cmp_k_w1[l], cmp_k_w2[l],
                   cmp_pos_v[l], cmp_v_w1[l], cmp_v_w2[l], moba_out_norm[l], nsa_out_norm[l], w_out[l])
        h = _swiglu_step(h, ffb_norm[l], ffb_w_gate[l], ffb_w_up[l], ffb_w_down[l])
    return _final_norm(h, final_norm[None], _token_tile(B * S, 1024)).reshape(B, S, D)
```

```python
import functools

import numpy as np
import jax
import jax.numpy as jnp
from jax import lax
from jax.experimental import pallas as pl
from jax.experimental.pallas import tpu as pltpu

HEAD_DIM = 64
MOBA_HEADS = 8
NSA_HEADS = 8
NSA_KV_HEADS = 2
NSA_GROUP = NSA_HEADS // NSA_KV_HEADS
NSA_BRANCHES = 3
MOBA_BLOCK = 256
MOBA_TOPK = 3
CMP_LEN = 32
CMP_STRIDE = 16
SLC_BLOCK = 64
SLC_TOPK = 16
WINDOW = 512
NEG = -1e30
SLC_FORCE = 1e4
EPS = 1e-6
SCALE = HEAD_DIM ** -0.5

LANES = 128
SUBLANES = 8
HEADS_PER_TILE = LANES // HEAD_DIM
COL_QM, COL_KM, COL_VM, COL_QN = 0, 4, 8, 12
COL_KC, COL_VC, COL_KS, COL_VS, COL_KW, COL_VW, COL_G = 16, 17, 18, 19, 20, 21, 22
IN_TILES = 24
IN_PAD = IN_TILES * LANES
IN_WIDTH = 2840
NSA_QTILE = 128
SLC_CHUNK = 4 * SLC_BLOCK

MXU_DTYPE = jnp.bfloat16
VMEM_LIMIT = 48 * 1024 * 1024
F32 = jnp.float32
HI = lax.Precision.HIGHEST


def _dot(a, b):
    return jnp.dot(a.astype(MXU_DTYPE), b.astype(MXU_DTYPE), preferred_element_type=F32)


def _dot_nt(a, b, precision=None):
    return lax.dot_general(a, b, (((1,), (1,)), ((), ())), precision=precision,
                           preferred_element_type=F32)


def _dot_hi(a, b):
    return jnp.dot(a, b, precision=HI, preferred_element_type=F32)


def _iota(shape, dim):
    return lax.broadcasted_iota(jnp.int32, shape, dim)


def _div_pow2(x, n):
    assert n & (n - 1) == 0
    return x >> (n.bit_length() - 1)


def _round_up(n, m):
    return -(-n // m) * m


def _params(*sem):
    return pltpu.CompilerParams(dimension_semantics=sem, vmem_limit_bytes=VMEM_LIMIT)


def _rms(x, g):
    return x * lax.rsqrt(jnp.mean(x * x, axis=-1, keepdims=True) + EPS) * g


def _ffn_kernel(x_ref, g_ref, wg_ref, wu_ref, wd_ref, o_ref, h_ref, acc_ref):
    j = pl.program_id(1)

    @pl.when(j == 0)
    def _():
        h_ref[...] = _rms(x_ref[...], g_ref[...]).astype(h_ref.dtype)
        acc_ref[...] = jnp.zeros_like(acc_ref)

    h = h_ref[...]
    a = jnp.dot(h, wg_ref[...], preferred_element_type=F32)
    b = jnp.dot(h, wu_ref[...], preferred_element_type=F32)
    u = a / (1.0 + jnp.exp(-a)) * b
    acc_ref[...] += _dot(u, wd_ref[...])

    @pl.when(j == pl.num_programs(1) - 1)
    def _():
        o_ref[...] = x_ref[...] + 0.5 * acc_ref[...]


def _ffn(x, g, wg, wu, wd, tm, tf):
    T, D = x.shape
    F = wg.shape[1]
    return pl.pallas_call(
        _ffn_kernel,
        grid=(T // tm, F // tf),
        in_specs=[
            pl.BlockSpec((tm, D), lambda i, j: (i, 0)),
            pl.BlockSpec((1, D), lambda i, j: (0, 0)),
            pl.BlockSpec((D, tf), lambda i, j: (0, j)),
            pl.BlockSpec((D, tf), lambda i, j: (0, j)),
            pl.BlockSpec((tf, D), lambda i, j: (j, 0)),
        ],
        out_specs=pl.BlockSpec((tm, D), lambda i, j: (i, 0)),
        out_shape=jax.ShapeDtypeStruct((T, D), F32),
        scratch_shapes=[pltpu.VMEM((tm, D), MXU_DTYPE), pltpu.VMEM((tm, D), F32)],
        compiler_params=_params("parallel", "arbitrary"),
        name="ffn",
    )(x, g, wg, wu, wd)


def _inproj_kernel(x_ref, g_ref, w_ref, o_ref, h_ref):
    @pl.when(pl.program_id(1) == 0)
    def _():
        h_ref[...] = _rms(x_ref[...], g_ref[...]).astype(h_ref.dtype)

    o_ref[...] = jnp.dot(h_ref[...], w_ref[...], preferred_element_type=F32)


def _inproj(x, g, w, tm, tn):
    T, D = x.shape
    N = w.shape[1]
    return pl.pallas_call(
        _inproj_kernel,
        grid=(T // tm, N // tn),
        in_specs=[
            pl.BlockSpec((tm, D), lambda i, j: (i, 0)),
            pl.BlockSpec((1, D), lambda i, j: (0, 0)),
            pl.BlockSpec((D, tn), lambda i, j: (0, j)),
        ],
        out_specs=pl.BlockSpec((tm, tn), lambda i, j: (i, j)),
        out_shape=jax.ShapeDtypeStruct((T, N), F32),
        scratch_shapes=[pltpu.VMEM((tm, D), MXU_DTYPE)],
        compiler_params=_params("parallel", "arbitrary"),
        name="inproj",
    )(x, g, w)


def _softmax_first(s, vt, m_ref, l_ref, acc_ref):
    m = jnp.max(s, axis=0, keepdims=True)
    p = jnp.exp(s - m)
    m_ref[...] = m
    l_ref[...] = jnp.sum(p, axis=0, keepdims=True)
    acc_ref[...] = _dot(vt, p)


def _softmax_next(s, vt, m_ref, l_ref, acc_ref):
    m_old = m_ref[...]
    m_new = jnp.maximum(m_old, jnp.max(s, axis=0, keepdims=True))
    alpha = jnp.exp(m_old - m_new)
    p = jnp.exp(s - m_new)
    m_ref[...] = m_new
    l_ref[...] = alpha * l_ref[...] + jnp.sum(p, axis=0, keepdims=True)
    acc_ref[...] = alpha * acc_ref[...] + _dot(vt, p)


def _rank_before(score, n, idx):
    rank = jnp.zeros(score.shape, jnp.int32)
    for m in range(n):
        row = score[m:m + 1, :]
        better = (row > score) | ((row == score) & (m < idx))
        rank = rank + better.astype(jnp.int32)
    return rank


def _moba_kernel(slopes_ref, q_ref, k_ref, v_ref, o_ref,
                 kmean_ref, kb_ref, vt_ref, row_ref, *, nb, kk):
    hp = pl.program_id(1)
    own = pl.program_id(2)
    blk = tq = MOBA_BLOCK

    @pl.when(own == 0)
    def _():
        kmean_ref[...] = jnp.zeros_like(kmean_ref)
        for n in range(nb):
            rows = slice(n * blk, (n + 1) * blk)
            kmean_ref[n:n + 1, :] = jnp.mean(k_ref[0, rows, :], axis=0, keepdims=True)
            vt_ref[:, rows] = v_ref[0, rows, :].T.astype(vt_ref.dtype)
        kb_ref[...] = k_ref[0].astype(kb_ref.dtype)

    q0 = own * tq
    cols = HEADS_PER_TILE * tq

    lo = _iota((tq, LANES), 1) < HEAD_DIM
    q = q_ref[0]
    qs = jnp.concatenate([jnp.where(lo, q, 0.0), jnp.where(lo, 0.0, q)], axis=0)
    col1 = _iota((1, cols), 1)
    slope = jnp.where(col1 < tq, slopes_ref[HEADS_PER_TILE * hp], slopes_ref[HEADS_PER_TILE * hp + 1])

    gate = _dot_nt(kmean_ref[...], qs, precision=HI)
    n_idx = _iota(gate.shape, 0)
    cand = n_idx < own
    gm = jnp.where(cand, gate, NEG)
    sel = cand & (_rank_before(gm, nb - 1, n_idx) < kk)
    row_ref[...] = slope * (n_idx * blk - q0).astype(F32) + jnp.where(sel, 0.0, NEG)

    qsb = (qs * SCALE).astype(MXU_DTYPE)
    rel = _iota((blk, cols), 0) - (_iota((blk, cols), 1) & (tq - 1))
    bias = slope * rel.astype(F32)

    def attend(c):
        nkeys = (c + 1) * blk
        s = _dot_nt(kb_ref[0:nkeys, :], qsb)
        pieces = [s[n * blk:(n + 1) * blk] + bias + row_ref[n:n + 1, :] for n in range(c)]
        pieces.append(jnp.where(rel <= 0, s[c * blk:nkeys] + bias, NEG))
        s = jnp.concatenate(pieces, axis=0)
        p = jnp.exp(s - jnp.max(s, axis=0, keepdims=True))
        inv = 1.0 / jnp.sum(p, axis=0, keepdims=True)
        pb = p.astype(MXU_DTYPE)
        halves = []
        for h in range(HEADS_PER_TILE):
            dims = slice(h * HEAD_DIM, (h + 1) * HEAD_DIM)
            qcols = slice(h * tq, (h + 1) * tq)
            halves.append(_dot(vt_ref[dims, 0:nkeys], pb[:, qcols]) * inv[:, qcols])
        o_ref[0] = jnp.concatenate(halves, axis=0).T

    for c in range(nb):
        @pl.when(own == c)
        def _(c=c):
            attend(c)


def _moba(slopes, proj):
    B, S, _ = proj.shape
    tq = MOBA_BLOCK
    nb = S // MOBA_BLOCK
    kk = max(1, min(MOBA_TOPK, nb - 1))
    cols = HEADS_PER_TILE * tq
    n_tiles = MOBA_HEADS // HEADS_PER_TILE
    nb_pad = _round_up(nb, SUBLANES)
    kern = functools.partial(_moba_kernel, nb=nb, kk=kk)
    return pl.pallas_call(
        kern,
        grid=(B, n_tiles, S // tq),
        in_specs=[
            pl.BlockSpec(memory_space=pltpu.SMEM),
            pl.BlockSpec((1, tq, LANES), lambda b, h, i: (b, i, COL_QM + h)),
            pl.BlockSpec((1, S, LANES), lambda b, h, i: (b, 0, COL_KM + h)),
            pl.BlockSpec((1, S, LANES), lambda b, h, i: (b, 0, COL_VM + h)),
        ],
        out_specs=pl.BlockSpec((1, tq, LANES), lambda b, h, i: (b, i, h)),
        out_shape=jax.ShapeDtypeStruct((B, S, MOBA_HEADS * HEAD_DIM), F32),
        scratch_shapes=[
            pltpu.VMEM((nb_pad, LANES), F32),
            pltpu.VMEM((S, LANES), MXU_DTYPE),
            pltpu.VMEM((LANES, S), MXU_DTYPE),
            pltpu.VMEM((nb_pad, cols), F32),
        ],
        compiler_params=_params("parallel", "parallel", "arbitrary"),
        name="moba",
    )(slopes, proj, proj, proj)


def _gelu_tanh(x):
    return x * (0.5 * (1.0 + jnp.tanh(0.7978845608028654 * (x + 0.044715 * (x * x * x)))))


def _compress_kernel(xk_ref, xv_ref, pk_ref, pv_ref, w1k_ref, w1v_ref, w2k_ref, w2v_ref,
                     kc_ref, vc_ref, *, nr):
    half = (CMP_LEN // 2) * HEAD_DIM
    for x_ref, p_ref, w1_ref, w2_ref, o_ref in ((xk_ref, pk_ref, w1k_ref, w2k_ref, kc_ref),
                                               (xv_ref, pv_ref, w1v_ref, w2v_ref, vc_ref)):
        out = jnp.zeros((nr, LANES), F32)
        for g in range(NSA_KV_HEADS):
            x = x_ref[0, g]
            top = _dot_hi(x + p_ref[:, 0:half], w1_ref[0:half, :])
            bot = _dot_hi(x + p_ref[:, half:2 * half], w1_ref[half:2 * half, :])
            hid = top + pltpu.roll(bot, nr - 1, 0)
            out = out + _dot_hi(_gelu_tanh(hid), w2_ref[g])
        o_ref[0] = out


def _compress(xk, xv, pk, pv, w1k, w1v, w2k, w2v):
    B, G, nr, W = xk.shape
    H = w1k.shape[1]
    kern = functools.partial(_compress_kernel, nr=nr)
    full = lambda *shape: pl.BlockSpec(shape, lambda b: (0,) * len(shape))
    xspec = pl.BlockSpec((1, G, nr, W), lambda b: (b, 0, 0, 0))
    ospec = pl.BlockSpec((1, nr, LANES), lambda b: (b, 0, 0))
    oshape = jax.ShapeDtypeStruct((B, nr, LANES), F32)
    return pl.pallas_call(
        kern,
        grid=(B,),
        in_specs=[xspec, xspec, full(1, 2 * W), full(1, 2 * W), full(2 * W, H), full(2 * W, H),
                  full(G, H, LANES), full(G, H, LANES)],
        out_specs=(ospec, ospec),
        out_shape=(oshape, oshape),
        compiler_params=_params("parallel"),
        name="nsa_compress",
    )(xk, xv, pk, pv, w1k, w1v, w2k, w2v)


def _nsa_kernel(slopes_ref, qa_ref, qb_ref, g_ref, kc_ref, vc_ref, ks_ref, vs_ref, kw_ref, vw_ref,
                o_ref, kcb, vct, ksb, vst, kwb, vwt, gt_ref, mb_ref, m_ref, l_ref, acc_ref,
                ocmp_ref, oslc_ref, owin_ref, *, S, tq, n_cmp, n_slc, k_top):
    g = pl.program_id(1)
    qi = pl.program_id(2)
    q0 = qi * tq
    cols = NSA_GROUP * tq
    sblk = SLC_CHUNK
    per_chunk = sblk // SLC_BLOCK
    nr = kcb.shape[0]

    @pl.when(qi == 0)
    def _():
        def both_halves(x):
            mine = (_iota(x.shape, 1) >= HEAD_DIM) == (g == 1)
            return jnp.where(mine, x, pltpu.roll(x, HEAD_DIM, 1))

        def transposed(x):
            return both_halves(x).T[0:HEAD_DIM].astype(MXU_DTYPE)

        kcb[...] = both_halves(kc_ref[0]).astype(MXU_DTYPE)
        vct[...] = transposed(vc_ref[0])
        for c in range(S // sblk):
            rows = slice(c * sblk, (c + 1) * sblk)
            ksb[rows, :] = both_halves(ks_ref[0, rows, :]).astype(MXU_DTYPE)
            kwb[rows, :] = both_halves(kw_ref[0, rows, :]).astype(MXU_DTYPE)
            vst[:, rows] = transposed(vs_ref[0, rows, :])
        for c in range(S // tq):
            vwt[c] = transposed(vw_ref[0, c * tq:(c + 1) * tq, :])

    lo = _iota((tq, LANES), 1) < HEAD_DIM
    qa = qa_ref[0] * SCALE
    qb = qb_ref[0] * SCALE
    qs = jnp.concatenate([jnp.where(lo, qa, 0.0), jnp.where(lo, 0.0, qa),
                          jnp.where(lo, qb, 0.0), jnp.where(lo, 0.0, qb)], axis=0).astype(MXU_DTYPE)
    col1 = _iota((1, cols), 1)
    h0 = NSA_GROUP * g
    slope = jnp.where(col1 < tq, slopes_ref[h0],
                      jnp.where(col1 < 2 * tq, slopes_ref[h0 + 1],
                                jnp.where(col1 < 3 * tq, slopes_ref[h0 + 2], slopes_ref[h0 + 3])))
    i1 = col1 & (tq - 1)
    t1 = q0 + i1

    nidx = _iota((nr, cols), 0)
    cend = nidx * CMP_STRIDE + (CMP_LEN - 1)
    maskc = (cend <= t1) & (nidx < n_cmp)
    s = _dot_nt(kcb[...], qs) + slope * (cend - t1).astype(F32)
    s = jnp.where(maskc, s, NEG)
    p = jnp.where(maskc, jnp.exp(s - jnp.max(s, axis=0, keepdims=True)), 0.0)
    p = p / jnp.maximum(jnp.sum(p, axis=0, keepdims=True), 1e-30)
    ocmp_ref[...] = _dot(vct[...], p)

    psum = p[:, 0:tq] + p[:, tq:2 * tq] + p[:, 2 * tq:3 * tq] + p[:, 3 * tq:4 * tq]
    nj = mb_ref.shape[0]
    jj = _iota((nj, nr), 0) * SLC_BLOCK
    nn = _iota((nj, nr), 1) * CMP_STRIDE
    overlap = ((nn < jj + SLC_BLOCK) & (nn + CMP_LEN > jj)
               & (nn < n_cmp * CMP_STRIDE) & (jj < n_slc * SLC_BLOCK)).astype(F32)
    imp = _dot_hi(overlap, psum)
    jl = _iota((nj, tq), 0)
    tb = _div_pow2(q0 + _iota((nj, tq), 1), SLC_BLOCK)
    cand = jl <= tb
    forced = (jl == 0) | (jl == tb) | (jl == tb - 1)
    score = jnp.where(cand, jnp.where(forced, SLC_FORCE, imp), NEG)

    def mask_rows(keep):
        return jnp.concatenate([jnp.where(keep, 0.0, NEG)] * NSA_GROUP, axis=1)

    mb_ref[...] = mask_rows(cand)

    @pl.when(q0 + tq > k_top * SLC_BLOCK)
    def _():
        mb_ref[...] = mask_rows(cand & (_rank_before(score, n_slc, jl) < k_top))

    rel = _iota((sblk, cols), 0) - i1
    bias = slope * rel.astype(F32)

    def offset_row(pos):
        return slope * jnp.full((1, cols), pos - q0, jnp.int32).astype(F32)

    def slc_attend(c):
        nkeys = (c + 1) * sblk
        s = _dot_nt(ksb[0:nkeys, :], qs)
        pieces = []
        for n in range(c + 1):
            orow = offset_row(n * sblk)
            for b in range(per_chunk):
                sub = slice(b * SLC_BLOCK, (b + 1) * SLC_BLOCK)
                j = n * per_chunk + b
                piece = s[j * SLC_BLOCK:(j + 1) * SLC_BLOCK] + bias[sub] + (mb_ref[j:j + 1, :] + orow)
                if n == c:
                    piece = jnp.where(rel[sub] + (c * sblk - q0) <= 0, piece, NEG)
                pieces.append(piece)
        s = jnp.concatenate(pieces, axis=0)
        p = jnp.exp(s - jnp.max(s, axis=0, keepdims=True))
        inv = 1.0 / jnp.sum(p, axis=0, keepdims=True)
        oslc_ref[...] = _dot(vst[:, 0:nkeys], p) * inv

    dc = _div_pow2(q0, sblk)
    for c in range(S // sblk):
        @pl.when(dc == c)
        def _(c=c):
            slc_attend(c)

    relw = _iota((tq, cols), 0) - i1
    biasw = slope * relw.astype(F32)
    n_back = WINDOW // tq

    @pl.when(qi >= n_back)
    def _():
        start = pl.multiple_of(q0 - n_back * tq, tq)
        s = _dot_nt(kwb[pl.ds(start, (n_back + 1) * tq), :], qs)
        pieces = []
        for e in range(n_back + 1):
            piece = s[e * tq:(e + 1) * tq] + biasw - slope * float((n_back - e) * tq)
            if e == 0:
                piece = jnp.where(relw > 0, piece, NEG)
            if e == n_back:
                piece = jnp.where(relw <= 0, piece, NEG)
            pieces.append(piece)
        s = jnp.concatenate(pieces, axis=0)
        p = jnp.exp(s - jnp.max(s, axis=0, keepdims=True))
        inv = 1.0 / jnp.sum(p, axis=0, keepdims=True)
        pb = p.astype(MXU_DTYPE)
        acc = _dot(vwt[qi - n_back], pb[0:tq])
        for e in range(1, n_back + 1):
            acc = acc + _dot(vwt[qi - n_back + e], pb[e * tq:(e + 1) * tq])
        owin_ref[...] = acc * inv

    @pl.when(qi < n_back)
    def _():
        def win_scores(d):
            k = kwb[pl.ds(pl.multiple_of(q0 - d * tq, tq), tq), :]
            return _dot_nt(k, qs) + biasw - slope * float(d * tq)

        _softmax_first(jnp.where(relw <= 0, win_scores(0), NEG), vwt[qi], m_ref, l_ref, acc_ref)
        for d in range(1, n_back):
            @pl.when(qi >= d)
            def _(d=d):
                _softmax_next(win_scores(d), vwt[qi - d], m_ref, l_ref, acc_ref)

        owin_ref[...] = acc_ref[...] * (1.0 / l_ref[...])

    owin = owin_ref[...]

    z = g_ref[0].T
    gt_ref[...] = 1.0 / (1.0 + jnp.exp(-z))
    ocmp = ocmp_ref[...]
    oslc = oslc_ref[...]
    heads = []
    for r in range(NSA_GROUP):
        sl = slice(r * tq, (r + 1) * tq)
        c = NSA_BRANCHES * (NSA_GROUP * g + r)
        heads.append(gt_ref[pl.ds(c, 1), :] * ocmp[:, sl] + gt_ref[pl.ds(c + 1, 1), :] * oslc[:, sl]
                     + gt_ref[pl.ds(c + 2, 1), :] * owin[:, sl])
    o_ref[0, :, 0:LANES] = jnp.concatenate(heads[0:2], axis=0).T
    o_ref[0, :, LANES:2 * LANES] = jnp.concatenate(heads[2:4], axis=0).T


def _nsa(slopes, proj, kc, vc):
    B, S, _ = proj.shape
    tq = NSA_QTILE
    n_cmp = (S - CMP_LEN) // CMP_STRIDE + 1
    n_slc = S // SLC_BLOCK
    k_top = min(SLC_TOPK, n_slc)
    nr = kc.shape[1]
    cols = NSA_GROUP * tq
    kern = functools.partial(_nsa_kernel, S=S, tq=tq, n_cmp=n_cmp, n_slc=n_slc, k_top=k_top)
    qtiles = NSA_GROUP // HEADS_PER_TILE

    def seq(col):
        return pl.BlockSpec((1, S, LANES), lambda b, g, i: (b, 0, col))

    small = pl.BlockSpec((1, nr, LANES), lambda b, g, i: (b, 0, 0))
    return pl.pallas_call(
        kern,
        grid=(B, NSA_KV_HEADS, S // tq),
        in_specs=[
            pl.BlockSpec(memory_space=pltpu.SMEM),
            pl.BlockSpec((1, tq, LANES), lambda b, g, i: (b, i, COL_QN + qtiles * g)),
            pl.BlockSpec((1, tq, LANES), lambda b, g, i: (b, i, COL_QN + qtiles * g + 1)),
            pl.BlockSpec((1, tq, LANES), lambda b, g, i: (b, i, COL_G)),
            small, small, seq(COL_KS), seq(COL_VS), seq(COL_KW), seq(COL_VW),
        ],
        out_specs=pl.BlockSpec((1, tq, qtiles * LANES), lambda b, g, i: (b, i, g)),
        out_shape=jax.ShapeDtypeStruct((B, S, NSA_HEADS * HEAD_DIM), F32),
        scratch_shapes=[
            pltpu.VMEM((nr, LANES), MXU_DTYPE), pltpu.VMEM((HEAD_DIM, nr), MXU_DTYPE),
            pltpu.VMEM((S, LANES), MXU_DTYPE), pltpu.VMEM((HEAD_DIM, S), MXU_DTYPE),
            pltpu.VMEM((S, LANES), MXU_DTYPE), pltpu.VMEM((S // tq, HEAD_DIM, tq), MXU_DTYPE),
            pltpu.VMEM((LANES, tq), F32),
            pltpu.VMEM((_round_up(n_slc, SUBLANES), cols), F32),
            pltpu.VMEM((1, cols), F32), pltpu.VMEM((1, cols), F32), pltpu.VMEM((HEAD_DIM, cols), F32),
            pltpu.VMEM((HEAD_DIM, cols), F32), pltpu.VMEM((HEAD_DIM, cols), F32),
            pltpu.VMEM((HEAD_DIM, cols), F32),
        ],
        compiler_params=_params("parallel", "parallel", "arbitrary"),
        name="nsa",
    )(slopes, proj, proj, proj, kc, vc, proj, proj, proj, proj)


def _outproj_kernel(om_ref, on_ref, x_ref, gm_ref, gn_ref, wm_ref, wn_ref, o_ref):
    y = _dot(_rms(om_ref[...], gm_ref[...]), wm_ref[...]) + _dot(_rms(on_ref[...], gn_ref[...]), wn_ref[...])
    o_ref[...] = x_ref[...] + y


def _outproj(om, on, x, gm, gn, wm, wn, tm):
    T, D = x.shape
    Wm, Wn = om.shape[1], on.shape[1]
    return pl.pallas_call(
        _outproj_kernel,
        grid=(T // tm,),
        in_specs=[
            pl.BlockSpec((tm, Wm), lambda i: (i, 0)),
            pl.BlockSpec((tm, Wn), lambda i: (i, 0)),
            pl.BlockSpec((tm, D), lambda i: (i, 0)),
            pl.BlockSpec((1, Wm), lambda i: (0, 0)),
            pl.BlockSpec((1, Wn), lambda i: (0, 0)),
            pl.BlockSpec((Wm, D), lambda i: (0, 0)),
            pl.BlockSpec((Wn, D), lambda i: (0, 0)),
        ],
        out_specs=pl.BlockSpec((tm, D), lambda i: (i, 0)),
        out_shape=jax.ShapeDtypeStruct((T, D), F32),
        compiler_params=_params("parallel"),
        name="outproj",
    )(om, on, x, gm, gn, wm, wn)


def _norm_kernel(x_ref, g_ref, o_ref):
    o_ref[...] = _rms(x_ref[...], g_ref[...])


def _final_norm(x, g, tm):
    T, D = x.shape
    return pl.pallas_call(
        _norm_kernel,
        grid=(T // tm,),
        in_specs=[pl.BlockSpec((tm, D), lambda i: (i, 0)), pl.BlockSpec((1, D), lambda i: (0, 0))],
        out_specs=pl.BlockSpec((tm, D), lambda i: (i, 0)),
        out_shape=jax.ShapeDtypeStruct((T, D), F32),
        compiler_params=_params("parallel"),
        name="final_norm",
    )(x, g)


def _alibi_slopes(n):
    return jnp.asarray(2.0 ** (-8.0 * np.arange(1, n + 1) / n), dtype=F32)


def _token_tile(T, want):
    return want if T % want == 0 else T


def _mixer(x, B, S, mix_norm, w_in, pos_k, k_w1, k_w2, pos_v, v_w1, v_w2, moba_norm, nsa_norm, w_out):
    T, D = x.shape
    w_in_p = jnp.pad(w_in, ((0, 0), (0, IN_PAD - w_in.shape[1]))).astype(MXU_DTYPE)
    proj = _inproj(x, mix_norm[None], w_in_p, _token_tile(T, 1024), IN_PAD // 3).reshape(B, S, IN_PAD)

    o_m = _moba(_alibi_slopes(MOBA_HEADS), proj)

    nr = S // CMP_STRIDE

    def windows(col):
        z = proj[:, :, col * LANES:(col + 1) * LANES].reshape(B, S, NSA_KV_HEADS, HEAD_DIM)
        return z.transpose(0, 2, 1, 3).reshape(B, NSA_KV_HEADS, nr, CMP_STRIDE * HEAD_DIM)

    def placed(w2):
        return jnp.stack([jnp.pad(w2, ((0, 0), (g * HEAD_DIM, LANES - (g + 1) * HEAD_DIM)))
                          for g in range(NSA_KV_HEADS)])

    kc, vc = _compress(windows(COL_KC), windows(COL_VC), pos_k.reshape(1, -1), pos_v.reshape(1, -1),
                       k_w1, v_w1, placed(k_w2), placed(v_w2))
    o_n = _nsa(_alibi_slopes(NSA_HEADS), proj, kc, vc)

    wm = w_out[:o_m.shape[-1]].astype(MXU_DTYPE)
    wn = w_out[o_m.shape[-1]:].astype(MXU_DTYPE)
    return _outproj(o_m.reshape(T, -1), o_n.reshape(T, -1), x, moba_norm[None], nsa_norm[None], wm, wn,
                    _token_tile(T, 512))


def _swiglu_step(x, norm, w_gate, w_up, w_down):
    T = x.shape[0]
    return _ffn(x, norm[None], w_gate.astype(MXU_DTYPE), w_up.astype(MXU_DTYPE), w_down.astype(MXU_DTYPE),
                _token_tile(T, 1024), 256)


@jax.jit
def kernel(x, ffa_norm, ffa_w_gate, ffa_w_up, ffa_w_down, mix_norm, w_in, cmp_pos_k, cmp_k_w1, cmp_k_w2,
           cmp_pos_v, cmp_v_w1, cmp_v_w2, moba_out_norm, nsa_out_norm, w_out, ffb_norm, ffb_w_gate,
           ffb_w_up, ffb_w_down, final_norm):
    B, S, D = x.shape
    assert S % MOBA_BLOCK == 0 and S >= WINDOW and w_in.shape[-1] == IN_WIDTH
    h = x.reshape(B * S, D)
    for l in range(ffa_norm.shape[0]):
        h = _swiglu_step(h, ffa_norm[l], ffa_w_gate[l], ffa_w_up[l], ffa_w_down[l])
        h = _mixer(h, B, S, mix_norm[l], w_in[l], cmp_pos_k[l], cmp_k_w1[l], cmp_k_w2[l],
                   cmp_pos_v[l], cmp_v_w1[l], cmp_v_w2[l], moba_out_norm[l], nsa_out_norm[l], w_out[l])
        h = _swiglu_step(h, ffb_norm[l], ffb_w_gate[l], ffb_w_up[l], ffb_w_down[l])
    return _final_norm(h, final_norm[None], _token_tile(B * S, 1024)).reshape(B, S, D)
```

```python
import functools

import numpy as np
import jax
import jax.numpy as jnp
from jax import lax
from jax.experimental import pallas as pl
from jax.experimental.pallas import tpu as pltpu

HEAD_DIM = 64
MOBA_HEADS = 8
NSA_HEADS = 8
NSA_KV_HEADS = 2
NSA_GROUP = NSA_HEADS // NSA_KV_HEADS
NSA_BRANCHES = 3
MOBA_BLOCK = 256
MOBA_TOPK = 3
CMP_LEN = 32
CMP_STRIDE = 16
SLC_BLOCK = 64
SLC_TOPK = 16
WINDOW = 512
NEG = -1e30
SLC_FORCE = 1e4
EPS = 1e-6
SCALE = HEAD_DIM ** -0.5

LANES = 128
SUBLANES = 8
HEADS_PER_TILE = LANES // HEAD_DIM
COL_QM, COL_KM, COL_VM, COL_QN = 0, 4, 8, 12
COL_KC, COL_VC, COL_KS, COL_VS, COL_KW, COL_VW, COL_G = 16, 17, 18, 19, 20, 21, 22
IN_TILES = 24
IN_PAD = IN_TILES * LANES
IN_WIDTH = 2840
NSA_QTILE = 128
SLC_CHUNK = 4 * SLC_BLOCK

MXU_DTYPE = jnp.bfloat16
VMEM_LIMIT = 48 * 1024 * 1024
F32 = jnp.float32
HI = lax.Precision.HIGHEST


def _dot(a, b):
    return jnp.dot(a.astype(MXU_DTYPE), b.astype(MXU_DTYPE), preferred_element_type=F32)


def _dot_nt(a, b, precision=None):
    return lax.dot_general(a, b, (((1,), (1,)), ((), ())), precision=precision,
                           preferred_element_type=F32)


def _dot_hi(a, b):
    return jnp.dot(a, b, precision=HI, preferred_element_type=F32)


def _iota(shape, dim):
    return lax.broadcasted_iota(jnp.int32, shape, dim)


def _div_pow2(x, n):
    assert n & (n - 1) == 0
    return x >> (n.bit_length() - 1)


def _round_up(n, m):
    return -(-n // m) * m


def _params(*sem):
    return pltpu.CompilerParams(dimension_semantics=sem, vmem_limit_bytes=VMEM_LIMIT)


def _rms(x, g):
    return x * lax.rsqrt(jnp.mean(x * x, axis=-1, keepdims=True) + EPS) * g


def _ffn_kernel(x_ref, g_ref, wg_ref, wu_ref, wd_ref, o_ref, h_ref, acc_ref):
    j = pl.program_id(1)

    @pl.when(j == 0)
    def _():
        h_ref[...] = _rms(x_ref[...], g_ref[...]).astype(h_ref.dtype)
        acc_ref[...] = jnp.zeros_like(acc_ref)

    h = h_ref[...]
    a = _dot(h, wg_ref[...])
    b = _dot(h, wu_ref[...])
    u = a / (1.0 + jnp.exp(-a)) * b
    acc_ref[...] += _dot(u, wd_ref[...])

    @pl.when(j == pl.num_programs(1) - 1)
    def _():
        o_ref[...] = x_ref[...] + 0.5 * acc_ref[...]


def _ffn(x, g, wg, wu, wd, tm, tf):
    T, D = x.shape
    F = wg.shape[1]
    return pl.pallas_call(
        _ffn_kernel,
        grid=(T // tm, F // tf),
        in_specs=[
            pl.BlockSpec((tm, D), lambda i, j: (i, 0)),
            pl.BlockSpec((1, D), lambda i, j: (0, 0)),
            pl.BlockSpec((D, tf), lambda i, j: (0, j)),
            pl.BlockSpec((D, tf), lambda i, j: (0, j)),
            pl.BlockSpec((tf, D), lambda i, j: (j, 0)),
        ],
        out_specs=pl.BlockSpec((tm, D), lambda i, j: (i, 0)),
        out_shape=jax.ShapeDtypeStruct((T, D), F32),
        scratch_shapes=[pltpu.VMEM((tm, D), MXU_DTYPE), pltpu.VMEM((tm, D), F32)],
        compiler_params=_params("parallel", "arbitrary"),
        name="ffn",
    )(x, g, wg, wu, wd)


def _inproj_kernel(x_ref, g_ref, w_ref, o_ref, h_ref):
    @pl.when(pl.program_id(1) == 0)
    def _():
        h_ref[...] = _rms(x_ref[...], g_ref[...]).astype(h_ref.dtype)

    o_ref[...] = _dot(h_ref[...], w_ref[...])


def _inproj(x, g, w, tm, tn):
    T, D = x.shape
    N = w.shape[1]
    return pl.pallas_call(
        _inproj_kernel,
        grid=(T // tm, N // tn),
        in_specs=[
            pl.BlockSpec((tm, D), lambda i, j: (i, 0)),
            pl.BlockSpec((1, D), lambda i, j: (0, 0)),
            pl.BlockSpec((D, tn), lambda i, j: (0, j)),
        ],
        out_specs=pl.BlockSpec((tm, tn), lambda i, j: (i, j)),
        out_shape=jax.ShapeDtypeStruct((T, N), F32),
        scratch_shapes=[pltpu.VMEM((tm, D), MXU_DTYPE)],
        compiler_params=_params("parallel", "arbitrary"),
        name="inproj",
    )(x, g, w)


def _rank_before(score, n, idx):
    rank = jnp.zeros(score.shape, jnp.int32)
    for m in range(n):
        row = score[m:m + 1, :]
        better = (row > score) | ((row == score) & (m < idx))
        rank = rank + better.astype(jnp.int32)
    return rank


def _moba_kernel(slopes_ref, q_ref, k_ref, v_ref, o_ref,
                 kmean_ref, kb_ref, vt_ref, row_ref, *, nb, kk):
    hp = pl.program_id(1)
    own = pl.program_id(2)
    blk = tq = MOBA_BLOCK
    cols = HEADS_PER_TILE * tq

    @pl.when(own == 0)
    def _():
        kmean_ref[...] = jnp.zeros_like(kmean_ref)
        for n in range(nb):
            rows = slice(n * blk, (n + 1) * blk)
            kmean_ref[n:n + 1, :] = jnp.mean(k_ref[0, rows, :], axis=0, keepdims=True)
            vt_ref[:, rows] = v_ref[0, rows, :].T.astype(vt_ref.dtype)
        kb_ref[...] = k_ref[0].astype(kb_ref.dtype)

    q0 = own * tq
    lo = _iota((tq, LANES), 1) < HEAD_DIM
    q = q_ref[0]
    qs = jnp.concatenate([jnp.where(lo, q, 0.0), jnp.where(lo, 0.0, q)], axis=0)
    col1 = _iota((1, cols), 1)
    slope = jnp.where(col1 < tq, slopes_ref[HEADS_PER_TILE * hp], slopes_ref[HEADS_PER_TILE * hp + 1])

    gate = _dot_nt(kmean_ref[...], qs, precision=HI)
    n_idx = _iota(gate.shape, 0)
    cand = n_idx < own
    gm = jnp.where(cand, gate, NEG)
    sel = cand & (_rank_before(gm, nb - 1, n_idx) < kk)
    row_ref[...] = slope * (n_idx * blk - q0).astype(F32) + jnp.where(sel, 0.0, NEG)

    qsb = (qs * SCALE).astype(MXU_DTYPE)
    rel = _iota((blk, cols), 0) - (_iota((blk, cols), 1) & (tq - 1))
    bias = slope * rel.astype(F32)

    def attend(c):
        nkeys = (c + 1) * blk
        s = _dot_nt(kb_ref[0:nkeys, :], qsb)
        pieces = [s[n * blk:(n + 1) * blk] + bias + row_ref[n:n + 1, :] for n in range(c)]
        pieces.append(jnp.where(rel <= 0, s[c * blk:nkeys] + bias, NEG))
        s = jnp.concatenate(pieces, axis=0)
        p = jnp.exp(s - jnp.max(s, axis=0, keepdims=True))
        inv = 1.0 / jnp.sum(p, axis=0, keepdims=True)
        pb = p.astype(MXU_DTYPE)
        halves = []
        for h in range(HEADS_PER_TILE):
            dims = slice(h * HEAD_DIM, (h + 1) * HEAD_DIM)
            qcols = slice(h * tq, (h + 1) * tq)
            halves.append(_dot(vt_ref[dims, 0:nkeys], pb[:, qcols]) * inv[:, qcols])
        o_ref[0] = jnp.concatenate(halves, axis=0).T

    for c in range(nb):
        @pl.when(own == c)
        def _(c=c):
            attend(c)


def _moba(slopes, proj):
    B, S, _ = proj.shape
    tq = MOBA_BLOCK
    nb = S // MOBA_BLOCK
    kk = max(1, min(MOBA_TOPK, nb - 1))
    n_tiles = MOBA_HEADS // HEADS_PER_TILE
    kern = functools.partial(_moba_kernel, nb=nb, kk=kk)
    return pl.pallas_call(
        kern,
        grid=(B, n_tiles, S // tq),
        in_specs=[
            pl.BlockSpec(memory_space=pltpu.SMEM),
            pl.BlockSpec((1, tq, LANES), lambda b, h, i: (b, i, COL_QM + h)),
            pl.BlockSpec((1, S, LANES), lambda b, h, i: (b, 0, COL_KM + h)),
            pl.BlockSpec((1, S, LANES), lambda b, h, i: (b, 0, COL_VM + h)),
        ],
        out_specs=pl.BlockSpec((1, tq, LANES), lambda b, h, i: (b, i, h)),
        out_shape=jax.ShapeDtypeStruct((B, S, MOBA_HEADS * HEAD_DIM), F32),
        scratch_shapes=[
            pltpu.VMEM((_round_up(nb, SUBLANES), LANES), F32),
            pltpu.VMEM((S, LANES), MXU_DTYPE),
            pltpu.VMEM((LANES, S), MXU_DTYPE),
            pltpu.VMEM((_round_up(nb, SUBLANES), HEADS_PER_TILE * tq), F32),
        ],
        compiler_params=_params("parallel", "parallel", "arbitrary"),
        name="moba",
    )(slopes, proj, proj, proj)


def _gelu_tanh(x):
    return x * (0.5 * (1.0 + jnp.tanh(0.7978845608028654 * (x + 0.044715 * (x * x * x)))))


def _compress_kernel(xk_ref, xv_ref, pk_ref, pv_ref, w1k_ref, w1v_ref, w2k_ref, w2v_ref,
                     kc_ref, vc_ref, *, nr):
    half = (CMP_LEN // 2) * HEAD_DIM
    for x_ref, p_ref, w1_ref, w2_ref, o_ref in ((xk_ref, pk_ref, w1k_ref, w2k_ref, kc_ref),
                                               (xv_ref, pv_ref, w1v_ref, w2v_ref, vc_ref)):
        out = jnp.zeros((nr, LANES), F32)
        for g in range(NSA_KV_HEADS):
            x = x_ref[0, g]
            top = _dot_hi(x + p_ref[:, 0:half], w1_ref[0:half, :])
            bot = _dot_hi(x + p_ref[:, half:2 * half], w1_ref[half:2 * half, :])
            hid = top + pltpu.roll(bot, nr - 1, 0)
            out = out + _dot_hi(_gelu_tanh(hid), w2_ref[g])
        o_ref[0] = out


def _compress(xk, xv, pk, pv, w1k, w1v, w2k, w2v):
    B, G, nr, W = xk.shape
    H = w1k.shape[1]
    kern = functools.partial(_compress_kernel, nr=nr)
    full = lambda *shape: pl.BlockSpec(shape, lambda b: (0,) * len(shape))
    xspec = pl.BlockSpec((1, G, nr, W), lambda b: (b, 0, 0, 0))
    ospec = pl.BlockSpec((1, nr, LANES), lambda b: (b, 0, 0))
    oshape = jax.ShapeDtypeStruct((B, nr, LANES), F32)
    return pl.pallas_call(
        kern,
        grid=(B,),
        in_specs=[xspec, xspec, full(1, 2 * W), full(1, 2 * W), full(2 * W, H), full(2 * W, H),
                  full(G, H, LANES), full(G, H, LANES)],
        out_specs=(ospec, ospec),
        out_shape=(oshape, oshape),
        compiler_params=_params("parallel"),
        name="nsa_compress",
    )(xk, xv, pk, pv, w1k, w1v, w2k, w2v)


def _nsa_kernel(slopes_ref, qa_ref, qb_ref, g_ref, kc_ref, vc_ref, ks_ref, vs_ref, kw_ref, vw_ref,
                o_ref, kcb, vct, ksb, vst, kwb, vwt, gt_ref, mb_ref, ocmp_ref, owin_ref,
                *, S, tq, n_cmp, n_slc, k_top):
    g = pl.program_id(1)
    qi = pl.program_id(2)
    q0 = qi * tq
    cols = NSA_GROUP * tq
    sblk = SLC_CHUNK
    per_chunk = sblk // SLC_BLOCK
    nr = kcb.shape[0]

    @pl.when(qi == 0)
    def _():
        def both_halves(x):
            mine = (_iota(x.shape, 1) >= HEAD_DIM) == (g == 1)
            return jnp.where(mine, x, pltpu.roll(x, HEAD_DIM, 1))

        def transposed(x):
            return both_halves(x).T[0:HEAD_DIM].astype(MXU_DTYPE)

        kcb[...] = both_halves(kc_ref[0]).astype(MXU_DTYPE)
        vct[...] = transposed(vc_ref[0])
        for c in range(S // sblk):
            rows = slice(c * sblk, (c + 1) * sblk)
            ksb[rows, :] = both_halves(ks_ref[0, rows, :]).astype(MXU_DTYPE)
            kwb[rows, :] = both_halves(kw_ref[0, rows, :]).astype(MXU_DTYPE)
            vst[:, rows] = transposed(vs_ref[0, rows, :])
        for c in range(S // tq):
            vwt[c] = transposed(vw_ref[0, c * tq:(c + 1) * tq, :])

    lo = _iota((tq, LANES), 1) < HEAD_DIM
    qa = qa_ref[0] * SCALE
    qb = qb_ref[0] * SCALE
    qs = jnp.concatenate([jnp.where(lo, qa, 0.0), jnp.where(lo, 0.0, qa),
                          jnp.where(lo, qb, 0.0), jnp.where(lo, 0.0, qb)], axis=0).astype(MXU_DTYPE)
    col1 = _iota((1, cols), 1)
    h0 = NSA_GROUP * g
    slope = jnp.where(col1 < tq, slopes_ref[h0],
                      jnp.where(col1 < 2 * tq, slopes_ref[h0 + 1],
                                jnp.where(col1 < 3 * tq, slopes_ref[h0 + 2], slopes_ref[h0 + 3])))
    i1 = col1 & (tq - 1)
    t1 = q0 + i1

    relw = _iota((tq, cols), 0) - i1
    biasw = slope * relw.astype(F32)
    n_back = WINDOW // tq
    pieces = []
    for e in range(n_back + 1):
        d = n_back - e
        start = pl.multiple_of(jnp.maximum(q0 - d * tq, 0), tq)
        row = jnp.where(qi >= d, -slope * float(d * tq), NEG)
        piece = _dot_nt(kwb[pl.ds(start, tq), :], qs) + biasw + row
        if e == 0:
            piece = jnp.where(relw > 0, piece, NEG)
        if e == n_back:
            piece = jnp.where(relw <= 0, piece, NEG)
        pieces.append(piece)
    s = jnp.concatenate(pieces, axis=0)
    p = jnp.exp(s - jnp.max(s, axis=0, keepdims=True))
    inv = 1.0 / jnp.sum(p, axis=0, keepdims=True)
    pb = p.astype(MXU_DTYPE)
    acc = _dot(vwt[jnp.maximum(qi - n_back, 0)], pb[0:tq])
    for e in range(1, n_back + 1):
        acc = acc + _dot(vwt[jnp.maximum(qi - n_back + e, 0)], pb[e * tq:(e + 1) * tq])
    owin_ref[...] = acc * inv

    nidx = _iota((nr, cols), 0)
    cend = nidx * CMP_STRIDE + (CMP_LEN - 1)
    maskc = (cend <= t1) & (nidx < n_cmp)
    s = _dot_nt(kcb[...], qs) + slope * (cend - t1).astype(F32)
    s = jnp.where(maskc, s, NEG)
    p = jnp.where(maskc, jnp.exp(s - jnp.max(s, axis=0, keepdims=True)), 0.0)
    p = p / jnp.maximum(jnp.sum(p, axis=0, keepdims=True), 1e-30)
    ocmp_ref[...] = _dot(vct[...], p)

    psum = p[:, 0:tq] + p[:, tq:2 * tq] + p[:, 2 * tq:3 * tq] + p[:, 3 * tq:4 * tq]
    nj = mb_ref.shape[0]
    jj = _iota((nj, nr), 0) * SLC_BLOCK
    nn = _iota((nj, nr), 1) * CMP_STRIDE
    overlap = ((nn < jj + SLC_BLOCK) & (nn + CMP_LEN > jj)
               & (nn < n_cmp * CMP_STRIDE) & (jj < n_slc * SLC_BLOCK)).astype(F32)
    imp = _dot_hi(overlap, psum)
    jl = _iota((nj, tq), 0)
    tb = _div_pow2(q0 + _iota((nj, tq), 1), SLC_BLOCK)
    cand = jl <= tb
    forced = (jl == 0) | (jl == tb) | (jl == tb - 1)
    score = jnp.where(cand, jnp.where(forced, SLC_FORCE, imp), NEG)
    keep = cand & (_rank_before(score, n_slc, jl) < k_top)
    mb_ref[...] = jnp.concatenate([jnp.where(keep, 0.0, NEG)] * NSA_GROUP, axis=1)

    z = g_ref[0].T
    gt_ref[...] = 1.0 / (1.0 + jnp.exp(-z))

    def combine(oslc):
        ocmp = ocmp_ref[...]
        owin = owin_ref[...]
        heads = []
        for r in range(NSA_GROUP):
            sl = slice(r * tq, (r + 1) * tq)
            c = NSA_BRANCHES * (NSA_GROUP * g + r)
            heads.append(gt_ref[pl.ds(c, 1), :] * ocmp[:, sl] + gt_ref[pl.ds(c + 1, 1), :] * oslc[:, sl]
                         + gt_ref[pl.ds(c + 2, 1), :] * owin[:, sl])
        o_ref[0, :, 0:LANES] = jnp.concatenate(heads[0:2], axis=0).T
        o_ref[0, :, LANES:2 * LANES] = jnp.concatenate(heads[2:4], axis=0).T

    rel = _iota((sblk, cols), 0) - i1
    bias = slope * rel.astype(F32)

    def offset_row(pos):
        return slope * jnp.full((1, cols), pos - q0, jnp.int32).astype(F32)

    def slc_attend(c):
        nkeys = (c + 1) * sblk
        s = _dot_nt(ksb[0:nkeys, :], qs)
        pieces = []
        for n in range(c + 1):
            orow = offset_row(n * sblk)
            for b in range(per_chunk):
                sub = slice(b * SLC_BLOCK, (b + 1) * SLC_BLOCK)
                j = n * per_chunk + b
                piece = s[j * SLC_BLOCK:(j + 1) * SLC_BLOCK] + bias[sub] + (mb_ref[j:j + 1, :] + orow)
                if n == c:
                    piece = jnp.where(rel[sub] + (c * sblk - q0) <= 0, piece, NEG)
                pieces.append(piece)
        s = jnp.concatenate(pieces, axis=0)
        p = jnp.exp(s - jnp.max(s, axis=0, keepdims=True))
        inv = 1.0 / jnp.sum(p, axis=0, keepdims=True)
        combine(_dot(vst[:, 0:nkeys], p) * inv)

    dc = _div_pow2(q0, sblk)
    for c in range(S // sblk):
        @pl.when(dc == c)
        def _(c=c):
            slc_attend(c)


def _nsa(slopes, proj, kc, vc):
    B, S, _ = proj.shape
    tq = NSA_QTILE
    n_cmp = (S - CMP_LEN) // CMP_STRIDE + 1
    n_slc = S // SLC_BLOCK
    k_top = min(SLC_TOPK, n_slc)
    nr = kc.shape[1]
    cols = NSA_GROUP * tq
    kern = functools.partial(_nsa_kernel, S=S, tq=tq, n_cmp=n_cmp, n_slc=n_slc, k_top=k_top)
    qtiles = NSA_GROUP // HEADS_PER_TILE

    def seq(col):
        return pl.BlockSpec((1, S, LANES), lambda b, g, i: (b, 0, col))

    small = pl.BlockSpec((1, nr, LANES), lambda b, g, i: (b, 0, 0))
    return pl.pallas_call(
        kern,
        grid=(B, NSA_KV_HEADS, S // tq),
        in_specs=[
            pl.BlockSpec(memory_space=pltpu.SMEM),
            pl.BlockSpec((1, tq, LANES), lambda b, g, i: (b, i, COL_QN + qtiles * g)),
            pl.BlockSpec((1, tq, LANES), lambda b, g, i: (b, i, COL_QN + qtiles * g + 1)),
            pl.BlockSpec((1, tq, LANES), lambda b, g, i: (b, i, COL_G)),
            small, small, seq(COL_KS), seq(COL_VS), seq(COL_KW), seq(COL_VW),
        ],
        out_specs=pl.BlockSpec((1, tq, qtiles * LANES), lambda b, g, i: (b, i, g)),
        out_shape=jax.ShapeDtypeStruct((B, S, NSA_HEADS * HEAD_DIM), F32),
        scratch_shapes=[
            pltpu.VMEM((nr, LANES), MXU_DTYPE), pltpu.VMEM((HEAD_DIM, nr), MXU_DTYPE),
            pltpu.VMEM((S, LANES), MXU_DTYPE), pltpu.VMEM((HEAD_DIM, S), MXU_DTYPE),
            pltpu.VMEM((S, LANES), MXU_DTYPE), pltpu.VMEM((S // tq, HEAD_DIM, tq), MXU_DTYPE),
            pltpu.VMEM((LANES, tq), F32),
            pltpu.VMEM((_round_up(n_slc, SUBLANES), cols), F32),
            pltpu.VMEM((HEAD_DIM, cols), F32), pltpu.VMEM((HEAD_DIM, cols), F32),
        ],
        compiler_params=_params("parallel", "parallel", "arbitrary"),
        name="nsa",
    )(slopes, proj, proj, proj, kc, vc, proj, proj, proj, proj)


def _outproj_kernel(om_ref, on_ref, x_ref, gm_ref, gn_ref, wm_ref, wn_ref, o_ref):
    y = _dot(_rms(om_ref[...], gm_ref[...]), wm_ref[...]) + _dot(_rms(on_ref[...], gn_ref[...]), wn_ref[...])
    o_ref[...] = x_ref[...] + y


def _outproj(om, on, x, gm, gn, wm, wn, tm):
    T, D = x.shape
    Wm, Wn = om.shape[1], on.shape[1]
    return pl.pallas_call(
        _outproj_kernel,
        grid=(T // tm,),
        in_specs=[
            pl.BlockSpec((tm, Wm), lambda i: (i, 0)),
            pl.BlockSpec((tm, Wn), lambda i: (i, 0)),
            pl.BlockSpec((tm, D), lambda i: (i, 0)),
            pl.BlockSpec((1, Wm), lambda i: (0, 0)),
            pl.BlockSpec((1, Wn), lambda i: (0, 0)),
            pl.BlockSpec((Wm, D), lambda i: (0, 0)),
            pl.BlockSpec((Wn, D), lambda i: (0, 0)),
        ],
        out_specs=pl.BlockSpec((tm, D), lambda i: (i, 0)),
        out_shape=jax.ShapeDtypeStruct((T, D), F32),
        compiler_params=_params("parallel"),
        name="outproj",
    )(om, on, x, gm, gn, wm, wn)


def _norm_kernel(x_ref, g_ref, o_ref):
    o_ref[...] = _rms(x_ref[...], g_ref[...])


def _final_norm(x, g, tm):
    T, D = x.shape
    return pl.pallas_call(
        _norm_kernel,
        grid=(T // tm,),
        in_specs=[pl.BlockSpec((tm, D), lambda i: (i, 0)), pl.BlockSpec((1, D), lambda i: (0, 0))],
        out_specs=pl.BlockSpec((tm, D), lambda i: (i, 0)),
        out_shape=jax.ShapeDtypeStruct((T, D), F32),
        compiler_params=_params("parallel"),
        name="final_norm",
    )(x, g)


def _alibi_slopes(n):
    return jnp.asarray(2.0 ** (-8.0 * np.arange(1, n + 1) / n), dtype=F32)


def _token_tile(T, want):
    return want if T % want == 0 else T


def _mixer(x, B, S, mix_norm, w_in, pos_k, k_w1, k_w2, pos_v, v_w1, v_w2, moba_norm, nsa_norm, w_out):
    T, D = x.shape
    w_in_p = jnp.pad(w_in, ((0, 0), (0, IN_PAD - w_in.shape[1])))
    proj = _inproj(x, mix_norm[None], w_in_p, _token_tile(T, 1024), IN_PAD // 3).reshape(B, S, IN_PAD)

    o_m = _moba(_alibi_slopes(MOBA_HEADS), proj)

    nr = S // CMP_STRIDE

    def windows(col):
        z = proj[:, :, col * LANES:(col + 1) * LANES].reshape(B, S, NSA_KV_HEADS, HEAD_DIM)
        return z.transpose(0, 2, 1, 3).reshape(B, NSA_KV_HEADS, nr, CMP_STRIDE * HEAD_DIM)

    def placed(w2):
        return jnp.stack([jnp.pad(w2, ((0, 0), (g * HEAD_DIM, LANES - (g + 1) * HEAD_DIM)))
                          for g in range(NSA_KV_HEADS)])

    kc, vc = _compress(windows(COL_KC), windows(COL_VC), pos_k.reshape(1, -1), pos_v.reshape(1, -1),
                       k_w1, v_w1, placed(k_w2), placed(v_w2))
    o_n = _nsa(_alibi_slopes(NSA_HEADS), proj, kc, vc)

    wm = w_out[:o_m.shape[-1]]
    wn = w_out[o_m.shape[-1]:]
    return _outproj(o_m.reshape(T, -1), o_n.reshape(T, -1), x, moba_norm[None], nsa_norm[None], wm, wn,
                    _token_tile(T, 512))


def _swiglu_step(x, norm, w_gate, w_up, w_down):
    T = x.shape[0]
    return _ffn(x, norm[None], w_gate, w_up, w_down, _token_tile(T, 1024), 256)


@jax.jit
def kernel(x, ffa_norm, ffa_w_gate, ffa_w_up, ffa_w_down, mix_norm, w_in, cmp_pos_k, cmp_k_w1, cmp_k_w2,
           cmp_pos_v, cmp_v_w1, cmp_v_w2, moba_out_norm, nsa_out_norm, w_out, ffb_norm, ffb_w_gate,
           ffb_w_up, ffb_w_down, final_norm):
    B, S, D = x.shape
    assert S % MOBA_BLOCK == 0 and S >= WINDOW and w_in.shape[-1] == IN_WIDTH
    h = x.reshape(B * S, D)
    for l in range(ffa_norm.shape[0]):
        h = _swiglu_step(h, ffa_norm[l], ffa_w_gate[l], ffa_w_up[l], ffa_w_down[l])
        h = _mixer(h, B, S, mix_norm[l], w_in[l], cmp_pos_k[l], cmp_k_w1[l], cmp_k_w2[l],
                   cmp_pos_v[l], cmp_v_w1[l], cmp_v_w2[l], moba_out_norm[l], nsa_out_norm[l], w_out[l])
        h = _swiglu_step(h, ffb_norm[l], ffb_w_gate[l], ffb_w_up[l], ffb_w_down[l])
    return _final_norm(h, final_norm[None], _token_tile(B * S, 1024)).reshape(B, S, D)
```

```python
import functools

import numpy as np
import jax
import jax.numpy as jnp
from jax import lax
from jax.experimental import pallas as pl
from jax.experimental.pallas import tpu as pltpu

HEAD_DIM = 64
MOBA_HEADS = 8
NSA_HEADS = 8
NSA_KV_HEADS = 2
NSA_GROUP = NSA_HEADS // NSA_KV_HEADS
NSA_BRANCHES = 3
MOBA_BLOCK = 256
MOBA_TOPK = 3
CMP_LEN = 32
CMP_STRIDE = 16
SLC_BLOCK = 64
SLC_TOPK = 16
WINDOW = 512
NEG = -1e30
SLC_FORCE = 1e4
EPS = 1e-6
SCALE = HEAD_DIM ** -0.5

LANES = 128
SUBLANES = 8
HEADS_PER_TILE = LANES // HEAD_DIM
COL_QM, COL_KM, COL_VM, COL_QN = 0, 4, 8, 12
COL_KC, COL_VC, COL_KS, COL_VS, COL_KW, COL_VW, COL_G = 16, 17, 18, 19, 20, 21, 22
IN_TILES = 24
IN_PAD = IN_TILES * LANES
IN_WIDTH = 2840
NSA_QTILE = 128
SLC_CHUNK = 4 * SLC_BLOCK

MXU_DTYPE = jnp.bfloat16
VMEM_LIMIT = 48 * 1024 * 1024
F32 = jnp.float32
HI = lax.Precision.HIGHEST


def _dot(a, b):
    return jnp.dot(a.astype(MXU_DTYPE), b.astype(MXU_DTYPE), preferred_element_type=F32)


def _dot_nt(a, b, precision=None):
    return lax.dot_general(a, b, (((1,), (1,)), ((), ())), precision=precision,
                           preferred_element_type=F32)


def _dot_hi(a, b):
    return jnp.dot(a, b, precision=HI, preferred_element_type=F32)


def _iota(shape, dim):
    return lax.broadcasted_iota(jnp.int32, shape, dim)


def _div_pow2(x, n):
    assert n & (n - 1) == 0
    return x >> (n.bit_length() - 1)


def _round_up(n, m):
    return -(-n // m) * m


def _params(*sem):
    return pltpu.CompilerParams(dimension_semantics=sem, vmem_limit_bytes=VMEM_LIMIT)


def _rms(x, g):
    return x * lax.rsqrt(jnp.mean(x * x, axis=-1, keepdims=True) + EPS) * g


def _ffn_kernel(x_ref, g_ref, wg_ref, wu_ref, wd_ref, o_ref, h_ref, acc_ref):
    j = pl.program_id(1)

    @pl.when(j == 0)
    def _():
        h_ref[...] = _rms(x_ref[...], g_ref[...]).astype(h_ref.dtype)
        acc_ref[...] = jnp.zeros_like(acc_ref)

    h = h_ref[...]
    a = _dot(h, wg_ref[...])
    b = _dot(h, wu_ref[...])
    u = a / (1.0 + jnp.exp(-a)) * b
    acc_ref[...] += _dot(u, wd_ref[...])

    @pl.when(j == pl.num_programs(1) - 1)
    def _():
        o_ref[...] = x_ref[...] + 0.5 * acc_ref[...]


def _ffn(x, g, wg, wu, wd, l, tm, tf):
    T, D = x.shape
    F = wg.shape[2]
    return pl.pallas_call(
        _ffn_kernel,
        grid=(T // tm, F // tf),
        in_specs=[
            pl.BlockSpec((tm, D), lambda i, j: (i, 0)),
            pl.BlockSpec((None, 1, D), lambda i, j: (l, 0, 0)),
            pl.BlockSpec((None, D, tf), lambda i, j: (l, 0, j)),
            pl.BlockSpec((None, D, tf), lambda i, j: (l, 0, j)),
            pl.BlockSpec((None, tf, D), lambda i, j: (l, j, 0)),
        ],
        out_specs=pl.BlockSpec((tm, D), lambda i, j: (i, 0)),
        out_shape=jax.ShapeDtypeStruct((T, D), F32),
        scratch_shapes=[pltpu.VMEM((tm, D), MXU_DTYPE), pltpu.VMEM((tm, D), F32)],
        compiler_params=_params("parallel", "arbitrary"),
        name="ffn",
    )(x, g, wg, wu, wd)


def _inproj_kernel(x_ref, g_ref, w_ref, o_ref, h_ref):
    @pl.when(pl.program_id(1) == 0)
    def _():
        h_ref[...] = _rms(x_ref[...], g_ref[...]).astype(h_ref.dtype)

    o_ref[...] = _dot(h_ref[...], w_ref[...])


def _inproj(x, g, w, l, tm, tn):
    T, D = x.shape
    N = w.shape[2]
    return pl.pallas_call(
        _inproj_kernel,
        grid=(T // tm, N // tn),
        in_specs=[
            pl.BlockSpec((tm, D), lambda i, j: (i, 0)),
            pl.BlockSpec((None, 1, D), lambda i, j: (l, 0, 0)),
            pl.BlockSpec((None, D, tn), lambda i, j: (l, 0, j)),
        ],
        out_specs=pl.BlockSpec((tm, tn), lambda i, j: (i, j)),
        out_shape=jax.ShapeDtypeStruct((T, N), F32),
        scratch_shapes=[pltpu.VMEM((tm, D), MXU_DTYPE)],
        compiler_params=_params("parallel", "arbitrary"),
        name="inproj",
    )(x, g, w)


def _rank_before(score, n, idx):
    rank = jnp.zeros(score.shape, jnp.int32)
    for m in range(n):
        row = score[m:m + 1, :]
        better = (row > score) | ((row == score) & (m < idx))
        rank = rank + better.astype(jnp.int32)
    return rank


def _moba_kernel(slopes_ref, q_ref, k_ref, v_ref, o_ref,
                 kmean_ref, kb_ref, vt_ref, row_ref, *, nb, kk):
    hp = pl.program_id(1)
    own = pl.program_id(2)
    blk = tq = MOBA_BLOCK
    cols = HEADS_PER_TILE * tq

    @pl.when(own == 0)
    def _():
        kmean_ref[...] = jnp.zeros_like(kmean_ref)
        for n in range(nb):
            rows = slice(n * blk, (n + 1) * blk)
            kmean_ref[n:n + 1, :] = jnp.mean(k_ref[0, rows, :], axis=0, keepdims=True)
            vt_ref[:, rows] = v_ref[0, rows, :].T.astype(vt_ref.dtype)
        kb_ref[...] = k_ref[0].astype(kb_ref.dtype)

    q0 = own * tq
    lo = _iota((tq, LANES), 1) < HEAD_DIM
    q = q_ref[0]
    qs = jnp.concatenate([jnp.where(lo, q, 0.0), jnp.where(lo, 0.0, q)], axis=0)
    col1 = _iota((1, cols), 1)
    slope = jnp.where(col1 < tq, slopes_ref[HEADS_PER_TILE * hp], slopes_ref[HEADS_PER_TILE * hp + 1])

    gate = _dot_nt(kmean_ref[...], qs, precision=HI)
    n_idx = _iota(gate.shape, 0)
    cand = n_idx < own
    gm = jnp.where(cand, gate, NEG)
    sel = cand & (_rank_before(gm, nb - 1, n_idx) < kk)
    row_ref[...] = slope * (n_idx * blk - q0).astype(F32) + jnp.where(sel, 0.0, NEG)

    qsb = (qs * SCALE).astype(MXU_DTYPE)
    rel = _iota((blk, cols), 0) - (_iota((blk, cols), 1) & (tq - 1))
    bias = slope * rel.astype(F32)

    def attend(c):
        nkeys = (c + 1) * blk
        s = _dot_nt(kb_ref[0:nkeys, :], qsb)
        pieces = [s[n * blk:(n + 1) * blk] + bias + row_ref[n:n + 1, :] for n in range(c)]
        pieces.append(jnp.where(rel <= 0, s[c * blk:nkeys] + bias, NEG))
        s = jnp.concatenate(pieces, axis=0)
        p = jnp.exp(s - jnp.max(s, axis=0, keepdims=True))
        inv = 1.0 / jnp.sum(p, axis=0, keepdims=True)
        pb = p.astype(MXU_DTYPE)
        halves = []
        for h in range(HEADS_PER_TILE):
            dims = slice(h * HEAD_DIM, (h + 1) * HEAD_DIM)
            qcols = slice(h * tq, (h + 1) * tq)
            halves.append(_dot(vt_ref[dims, 0:nkeys], pb[:, qcols]) * inv[:, qcols])
        o_ref[0] = jnp.concatenate(halves, axis=0).T

    for c in range(nb):
        @pl.when(own == c)
        def _(c=c):
            attend(c)


def _moba(slopes, proj):
    B, S, _ = proj.shape
    tq = MOBA_BLOCK
    nb = S // MOBA_BLOCK
    kk = max(1, min(MOBA_TOPK, nb - 1))
    n_tiles = MOBA_HEADS // HEADS_PER_TILE
    kern = functools.partial(_moba_kernel, nb=nb, kk=kk)
    return pl.pallas_call(
        kern,
        grid=(B, n_tiles, S // tq),
        in_specs=[
            pl.BlockSpec(memory_space=pltpu.SMEM),
            pl.BlockSpec((1, tq, LANES), lambda b, h, i: (b, i, COL_QM + h)),
            pl.BlockSpec((1, S, LANES), lambda b, h, i: (b, 0, COL_KM + h)),
            pl.BlockSpec((1, S, LANES), lambda b, h, i: (b, 0, COL_VM + h)),
        ],
        out_specs=pl.BlockSpec((1, tq, LANES), lambda b, h, i: (b, i, h)),
        out_shape=jax.ShapeDtypeStruct((B, S, MOBA_HEADS * HEAD_DIM), F32),
        scratch_shapes=[
            pltpu.VMEM((_round_up(nb, SUBLANES), LANES), F32),
            pltpu.VMEM((S, LANES), MXU_DTYPE),
            pltpu.VMEM((LANES, S), MXU_DTYPE),
            pltpu.VMEM((_round_up(nb, SUBLANES), HEADS_PER_TILE * tq), F32),
        ],
        compiler_params=_params("parallel", "parallel", "arbitrary"),
        name="moba",
    )(slopes, proj, proj, proj)


def _gelu_tanh(x):
    return x * (0.5 * (1.0 + jnp.tanh(0.7978845608028654 * (x + 0.044715 * (x * x * x)))))


def _compress_kernel(xk_ref, xv_ref, pk_ref, pv_ref, w1k_ref, w1v_ref, w2k_ref, w2v_ref,
                     kc_ref, vc_ref, *, nr):
    half = (CMP_LEN // 2) * HEAD_DIM
    for x_ref, p_ref, w1_ref, w2_ref, o_ref in ((xk_ref, pk_ref, w1k_ref, w2k_ref, kc_ref),
                                               (xv_ref, pv_ref, w1v_ref, w2v_ref, vc_ref)):
        out = jnp.zeros((nr, LANES), F32)
        for g in range(NSA_KV_HEADS):
            x = x_ref[0, g]
            top = _dot_hi(x + p_ref[:, 0:half], w1_ref[0:half, :])
            bot = _dot_hi(x + p_ref[:, half:2 * half], w1_ref[half:2 * half, :])
            hid = top + pltpu.roll(bot, nr - 1, 0)
            out = out + _dot_hi(_gelu_tanh(hid), w2_ref[g])
        o_ref[0] = out


def _compress(xk, xv, pk, pv, w1k, w1v, w2k, w2v, l):
    B, G, nr, W = xk.shape
    H = w1k.shape[2]
    kern = functools.partial(_compress_kernel, nr=nr)
    layer = lambda *shape: pl.BlockSpec((None,) + shape, lambda b: (l,) + (0,) * len(shape))
    full = lambda *shape: pl.BlockSpec(shape, lambda b: (0,) * len(shape))
    xspec = pl.BlockSpec((1, G, nr, W), lambda b: (b, 0, 0, 0))
    ospec = pl.BlockSpec((1, nr, LANES), lambda b: (b, 0, 0))
    oshape = jax.ShapeDtypeStruct((B, nr, LANES), F32)
    return pl.pallas_call(
        kern,
        grid=(B,),
        in_specs=[xspec, xspec, layer(1, 2 * W), layer(1, 2 * W), layer(2 * W, H), layer(2 * W, H),
                  full(G, H, LANES), full(G, H, LANES)],
        out_specs=(ospec, ospec),
        out_shape=(oshape, oshape),
        compiler_params=_params("parallel"),
        name="nsa_compress",
    )(xk, xv, pk, pv, w1k, w1v, w2k, w2v)


def _nsa_kernel(slopes_ref, qa_ref, qb_ref, g_ref, kc_ref, vc_ref, ks_ref, vs_ref, kw_ref, vw_ref,
                o_ref, kcb, vct, ksb, vst, kwb, vwt, gt_ref, mb_ref, ocmp_ref, owin_ref,
                *, S, tq, n_cmp, n_slc, k_top):
    g = pl.program_id(1)
    qi = pl.program_id(2)
    q0 = qi * tq
    cols = NSA_GROUP * tq
    sblk = SLC_CHUNK
    per_chunk = sblk // SLC_BLOCK
    nr = kcb.shape[0]

    @pl.when(qi == 0)
    def _():
        def both_halves(x):
            mine = (_iota(x.shape, 1) >= HEAD_DIM) == (g == 1)
            return jnp.where(mine, x, pltpu.roll(x, HEAD_DIM, 1))

        def transposed(x):
            return both_halves(x).T[0:HEAD_DIM].astype(MXU_DTYPE)

        kcb[...] = both_halves(kc_ref[0]).astype(MXU_DTYPE)
        vct[...] = transposed(vc_ref[0])
        for c in range(S // sblk):
            rows = slice(c * sblk, (c + 1) * sblk)
            ksb[rows, :] = both_halves(ks_ref[0, rows, :]).astype(MXU_DTYPE)
            kwb[rows, :] = both_halves(kw_ref[0, rows, :]).astype(MXU_DTYPE)
            vst[:, rows] = transposed(vs_ref[0, rows, :])
        for c in range(S // tq):
            vwt[c] = transposed(vw_ref[0, c * tq:(c + 1) * tq, :])

    lo = _iota((tq, LANES), 1) < HEAD_DIM
    qa = qa_ref[0] * SCALE
    qb = qb_ref[0] * SCALE
    qs = jnp.concatenate([jnp.where(lo, qa, 0.0), jnp.where(lo, 0.0, qa),
                          jnp.where(lo, qb, 0.0), jnp.where(lo, 0.0, qb)], axis=0).astype(MXU_DTYPE)
    col1 = _iota((1, cols), 1)
    h0 = NSA_GROUP * g
    slope = jnp.where(col1 < tq, slopes_ref[h0],
                      jnp.where(col1 < 2 * tq, slopes_ref[h0 + 1],
                                jnp.where(col1 < 3 * tq, slopes_ref[h0 + 2], slopes_ref[h0 + 3])))
    i1 = col1 & (tq - 1)
    t1 = q0 + i1

    relw = _iota((tq, cols), 0) - i1
    biasw = slope * relw.astype(F32)
    n_back = WINDOW // tq
    pieces = []
    for e in range(n_back + 1):
        d = n_back - e
        start = pl.multiple_of(jnp.maximum(q0 - d * tq, 0), tq)
        row = jnp.where(qi >= d, -slope * float(d * tq), NEG)
        piece = _dot_nt(kwb[pl.ds(start, tq), :], qs) + biasw + row
        if e == 0:
            piece = jnp.where(relw > 0, piece, NEG)
        if e == n_back:
            piece = jnp.where(relw <= 0, piece, NEG)
        pieces.append(piece)
    s = jnp.concatenate(pieces, axis=0)
    p = jnp.exp(s - jnp.max(s, axis=0, keepdims=True))
    inv = 1.0 / jnp.sum(p, axis=0, keepdims=True)
    pb = p.astype(MXU_DTYPE)
    acc = _dot(vwt[jnp.maximum(qi - n_back, 0)], pb[0:tq])
    for e in range(1, n_back + 1):
        acc = acc + _dot(vwt[jnp.maximum(qi - n_back + e, 0)], pb[e * tq:(e + 1) * tq])
    owin_ref[...] = acc * inv

    nidx = _iota((nr, cols), 0)
    cend = nidx * CMP_STRIDE + (CMP_LEN - 1)
    maskc = (cend <= t1) & (nidx < n_cmp)
    s = _dot_nt(kcb[...], qs) + slope * (cend - t1).astype(F32)
    s = jnp.where(maskc, s, NEG)
    p = jnp.where(maskc, jnp.exp(s - jnp.max(s, axis=0, keepdims=True)), 0.0)
    p = p / jnp.maximum(jnp.sum(p, axis=0, keepdims=True), 1e-30)
    ocmp_ref[...] = _dot(vct[...], p)

    psum = p[:, 0:tq] + p[:, tq:2 * tq] + p[:, 2 * tq:3 * tq] + p[:, 3 * tq:4 * tq]
    nj = mb_ref.shape[0]
    jj = _iota((nj, nr), 0) * SLC_BLOCK
    nn = _iota((nj, nr), 1) * CMP_STRIDE
    overlap = ((nn < jj + SLC_BLOCK) & (nn + CMP_LEN > jj)
               & (nn < n_cmp * CMP_STRIDE) & (jj < n_slc * SLC_BLOCK)).astype(F32)
    imp = _dot_hi(overlap, psum)
    jl = _iota((nj, tq), 0)
    tb = _div_pow2(q0 + _iota((nj, tq), 1), SLC_BLOCK)
    cand = jl <= tb
    forced = (jl == 0) | (jl == tb) | (jl == tb - 1)
    score = jnp.where(cand, jnp.where(forced, SLC_FORCE, imp), NEG)
    keep = cand & (_rank_before(score, n_slc, jl) < k_top)
    mb_ref[...] = jnp.concatenate([jnp.where(keep, 0.0, NEG)] * NSA_GROUP, axis=1)

    z = g_ref[0].T
    gt_ref[...] = 1.0 / (1.0 + jnp.exp(-z))

    def combine(oslc):
        ocmp = ocmp_ref[...]
        owin = owin_ref[...]
        heads = []
        for r in range(NSA_GROUP):
            sl = slice(r * tq, (r + 1) * tq)
            c = NSA_BRANCHES * (NSA_GROUP * g + r)
            heads.append(gt_ref[pl.ds(c, 1), :] * ocmp[:, sl] + gt_ref[pl.ds(c + 1, 1), :] * oslc[:, sl]
                         + gt_ref[pl.ds(c + 2, 1), :] * owin[:, sl])
        o_ref[0, :, 0:LANES] = jnp.concatenate(heads[0:2], axis=0).T
        o_ref[0, :, LANES:2 * LANES] = jnp.concatenate(heads[2:4], axis=0).T

    rel = _iota((sblk, cols), 0) - i1
    bias = slope * rel.astype(F32)

    def offset_row(pos):
        return slope * jnp.full((1, cols), pos - q0, jnp.int32).astype(F32)

    def slc_attend(c):
        nkeys = (c + 1) * sblk
        s = _dot_nt(ksb[0:nkeys, :], qs)
        pieces = []
        for n in range(c + 1):
            orow = offset_row(n * sblk)
            for b in range(per_chunk):
                sub = slice(b * SLC_BLOCK, (b + 1) * SLC_BLOCK)
                j = n * per_chunk + b
                piece = s[j * SLC_BLOCK:(j + 1) * SLC_BLOCK] + bias[sub] + (mb_ref[j:j + 1, :] + orow)
                if n == c:
                    piece = jnp.where(rel[sub] + (c * sblk - q0) <= 0, piece, NEG)
                pieces.append(piece)
        s = jnp.concatenate(pieces, axis=0)
        p = jnp.exp(s - jnp.max(s, axis=0, keepdims=True))
        inv = 1.0 / jnp.sum(p, axis=0, keepdims=True)
        combine(_dot(vst[:, 0:nkeys], p) * inv)

    dc = _div_pow2(q0, sblk)
    for c in range(S // sblk):
        @pl.when(dc == c)
        def _(c=c):
            slc_attend(c)


def _nsa(slopes, proj, kc, vc):
    B, S, _ = proj.shape
    tq = NSA_QTILE
    n_cmp = (S - CMP_LEN) // CMP_STRIDE + 1
    n_slc = S // SLC_BLOCK
    k_top = min(SLC_TOPK, n_slc)
    nr = kc.shape[1]
    cols = NSA_GROUP * tq
    kern = functools.partial(_nsa_kernel, S=S, tq=tq, n_cmp=n_cmp, n_slc=n_slc, k_top=k_top)
    qtiles = NSA_GROUP // HEADS_PER_TILE

    def seq(col):
        return pl.BlockSpec((1, S, LANES), lambda b, g, i: (b, 0, col))

    small = pl.BlockSpec((1, nr, LANES), lambda b, g, i: (b, 0, 0))
    return pl.pallas_call(
        kern,
        grid=(B, NSA_KV_HEADS, S // tq),
        in_specs=[
            pl.BlockSpec(memory_space=pltpu.SMEM),
            pl.BlockSpec((1, tq, LANES), lambda b, g, i: (b, i, COL_QN + qtiles * g)),
            pl.BlockSpec((1, tq, LANES), lambda b, g, i: (b, i, COL_QN + qtiles * g + 1)),
            pl.BlockSpec((1, tq, LANES), lambda b, g, i: (b, i, COL_G)),
            small, small, seq(COL_KS), seq(COL_VS), seq(COL_KW), seq(COL_VW),
        ],
        out_specs=pl.BlockSpec((1, tq, qtiles * LANES), lambda b, g, i: (b, i, g)),
        out_shape=jax.ShapeDtypeStruct((B, S, NSA_HEADS * HEAD_DIM), F32),
        scratch_shapes=[
            pltpu.VMEM((nr, LANES), MXU_DTYPE), pltpu.VMEM((HEAD_DIM, nr), MXU_DTYPE),
            pltpu.VMEM((S, LANES), MXU_DTYPE), pltpu.VMEM((HEAD_DIM, S), MXU_DTYPE),
            pltpu.VMEM((S, LANES), MXU_DTYPE), pltpu.VMEM((S // tq, HEAD_DIM, tq), MXU_DTYPE),
            pltpu.VMEM((LANES, tq), F32),
            pltpu.VMEM((_round_up(n_slc, SUBLANES), cols), F32),
            pltpu.VMEM((HEAD_DIM, cols), F32), pltpu.VMEM((HEAD_DIM, cols), F32),
        ],
        compiler_params=_params("parallel", "parallel", "arbitrary"),
        name="nsa",
    )(slopes, proj, proj, proj, kc, vc, proj, proj, proj, proj)


def _outproj_kernel(om_ref, on_ref, x_ref, gm_ref, gn_ref, wm_ref, wn_ref, o_ref):
    y = _dot(_rms(om_ref[...], gm_ref[...]), wm_ref[...]) + _dot(_rms(on_ref[...], gn_ref[...]), wn_ref[...])
    o_ref[...] = x_ref[...] + y


def _outproj(om, on, x, gm, gn, w, l, tm):
    T, D = x.shape
    Wm, Wn = om.shape[1], on.shape[1]
    assert Wm == Wn and w.shape[1] == Wm + Wn
    return pl.pallas_call(
        _outproj_kernel,
        grid=(T // tm,),
        in_specs=[
            pl.BlockSpec((tm, Wm), lambda i: (i, 0)),
            pl.BlockSpec((tm, Wn), lambda i: (i, 0)),
            pl.BlockSpec((tm, D), lambda i: (i, 0)),
            pl.BlockSpec((None, 1, Wm), lambda i: (l, 0, 0)),
            pl.BlockSpec((None, 1, Wn), lambda i: (l, 0, 0)),
            pl.BlockSpec((None, Wm, D), lambda i: (l, 0, 0)),
            pl.BlockSpec((None, Wn, D), lambda i: (l, 1, 0)),
        ],
        out_specs=pl.BlockSpec((tm, D), lambda i: (i, 0)),
        out_shape=jax.ShapeDtypeStruct((T, D), F32),
        compiler_params=_params("parallel"),
        name="outproj",
    )(om, on, x, gm, gn, w, w)


def _norm_kernel(x_ref, g_ref, o_ref):
    o_ref[...] = _rms(x_ref[...], g_ref[...])


def _final_norm(x, g, tm):
    T, D = x.shape
    return pl.pallas_call(
        _norm_kernel,
        grid=(T // tm,),
        in_specs=[pl.BlockSpec((tm, D), lambda i: (i, 0)), pl.BlockSpec((1, D), lambda i: (0, 0))],
        out_specs=pl.BlockSpec((tm, D), lambda i: (i, 0)),
        out_shape=jax.ShapeDtypeStruct((T, D), F32),
        compiler_params=_params("parallel"),
        name="final_norm",
    )(x, g)


def _alibi_slopes(n):
    return jnp.asarray(2.0 ** (-8.0 * np.arange(1, n + 1) / n), dtype=F32)


def _token_tile(T, want):
    return want if T % want == 0 else T


def _mixer(x, B, S, l, mix_norm, w_in_p, pos_k, k_w1, k_w2, pos_v, v_w1, v_w2, moba_norm, nsa_norm, w_out):
    T, D = x.shape
    proj = _inproj(x, mix_norm, w_in_p, l, _token_tile(T, 1024), IN_PAD // 3).reshape(B, S, IN_PAD)

    o_m = _moba(_alibi_slopes(MOBA_HEADS), proj)

    nr = S // CMP_STRIDE

    def windows(col):
        z = proj[:, :, col * LANES:(col + 1) * LANES].reshape(B, S, NSA_KV_HEADS, HEAD_DIM)
        return z.transpose(0, 2, 1, 3).reshape(B, NSA_KV_HEADS, nr, CMP_STRIDE * HEAD_DIM)

    def placed(w2):
        return jnp.stack([jnp.pad(w2, ((0, 0), (g * HEAD_DIM, LANES - (g + 1) * HEAD_DIM)))
                          for g in range(NSA_KV_HEADS)])

    kc, vc = _compress(windows(COL_KC), windows(COL_VC), pos_k, pos_v, k_w1, v_w1, placed(k_w2), placed(v_w2), l)
    o_n = _nsa(_alibi_slopes(NSA_HEADS), proj, kc, vc)

    return _outproj(o_m.reshape(T, -1), o_n.reshape(T, -1), x, moba_norm, nsa_norm, w_out, l,
                    _token_tile(T, 512))


def _swiglu_step(x, l, norm, w_gate, w_up, w_down):
    T = x.shape[0]
    return _ffn(x, norm, w_gate, w_up, w_down, l, _token_tile(T, 1024), 256)


def _rows(p):
    return p.reshape(p.shape[0], 1, -1)


@jax.jit
def kernel(x, ffa_norm, ffa_w_gate, ffa_w_up, ffa_w_down, mix_norm, w_in, cmp_pos_k, cmp_k_w1, cmp_k_w2,
           cmp_pos_v, cmp_v_w1, cmp_v_w2, moba_out_norm, nsa_out_norm, w_out, ffb_norm, ffb_w_gate,
           ffb_w_up, ffb_w_down, final_norm):
    B, S, D = x.shape
    assert S % MOBA_BLOCK == 0 and S >= WINDOW and w_in.shape[-1] == IN_WIDTH
    h = x.reshape(B * S, D)
    w_in_p = jnp.pad(w_in, ((0, 0), (0, 0), (0, IN_PAD - IN_WIDTH)))
    ffa_norm, ffb_norm, mix_norm = _rows(ffa_norm), _rows(ffb_norm), _rows(mix_norm)
    moba_out_norm, nsa_out_norm = _rows(moba_out_norm), _rows(nsa_out_norm)
    cmp_pos_k, cmp_pos_v = _rows(cmp_pos_k), _rows(cmp_pos_v)
    for l in range(ffa_norm.shape[0]):
        h = _swiglu_step(h, l, ffa_norm, ffa_w_gate, ffa_w_up, ffa_w_down)
        h = _mixer(h, B, S, l, mix_norm, w_in_p, cmp_pos_k, cmp_k_w1, cmp_k_w2[l],
                   cmp_pos_v, cmp_v_w1, cmp_v_w2[l], moba_out_norm, nsa_out_norm, w_out)
        h = _swiglu_step(h, l, ffb_norm, ffb_w_gate, ffb_w_up, ffb_w_down)
    return _final_norm(h, final_norm[None], _token_tile(B * S, 1024)).reshape(B, S, D)
```

```python
import functools

import numpy as np
import jax
import jax.numpy as jnp
from jax import lax
from jax.experimental import pallas as pl
from jax.experimental.pallas import tpu as pltpu

HEAD_DIM = 64
MOBA_HEADS = 8
NSA_HEADS = 8
NSA_KV_HEADS = 2
NSA_GROUP = NSA_HEADS // NSA_KV_HEADS
NSA_BRANCHES = 3
MOBA_BLOCK = 256
MOBA_TOPK = 3
CMP_LEN = 32
CMP_STRIDE = 16
SLC_BLOCK = 64
SLC_TOPK = 16
WINDOW = 512
NEG = -1e30
SLC_FORCE = 1e4
EPS = 1e-6
SCALE = HEAD_DIM ** -0.5

LANES = 128
SUBLANES = 8
HEADS_PER_TILE = LANES // HEAD_DIM
COL_QM, COL_KM, COL_VM, COL_QN = 0, 4, 8, 12
COL_KC, COL_VC, COL_KS, COL_VS, COL_KW, COL_VW, COL_G = 16, 17, 18, 19, 20, 21, 22
IN_TILES = 24
IN_PAD = IN_TILES * LANES
IN_WIDTH = 2840
NSA_QTILE = 128
SLC_CHUNK = 4 * SLC_BLOCK

MXU_DTYPE = jnp.bfloat16
VMEM_LIMIT = 48 * 1024 * 1024
F32 = jnp.float32
HI = lax.Precision.HIGHEST


def _dot(a, b):
    return jnp.dot(a.astype(MXU_DTYPE), b.astype(MXU_DTYPE), preferred_element_type=F32)


def _dot_nt(a, b, precision=None):
    return lax.dot_general(a, b, (((1,), (1,)), ((), ())), precision=precision,
                           preferred_element_type=F32)


def _dot_hi(a, b):
    return jnp.dot(a, b, precision=HI, preferred_element_type=F32)


def _iota(shape, dim):
    return lax.broadcasted_iota(jnp.int32, shape, dim)


def _div_pow2(x, n):
    assert n & (n - 1) == 0
    return x >> (n.bit_length() - 1)


def _round_up(n, m):
    return -(-n // m) * m


def _params(*sem):
    return pltpu.CompilerParams(dimension_semantics=sem, vmem_limit_bytes=VMEM_LIMIT)


def _rms(x, g):
    return x * lax.rsqrt(jnp.mean(x * x, axis=-1, keepdims=True) + EPS) * g


def _ffn_kernel(x_ref, g_ref, wg_ref, wu_ref, wd_ref, o_ref, h_ref, acc_ref):
    j = pl.program_id(1)

    @pl.when(j == 0)
    def _():
        h_ref[...] = _rms(x_ref[...], g_ref[...]).astype(h_ref.dtype)
        acc_ref[...] = jnp.zeros_like(acc_ref)

    h = h_ref[...]
    a = _dot(h, wg_ref[...])
    b = _dot(h, wu_ref[...])
    u = a / (1.0 + jnp.exp(-a)) * b
    acc_ref[...] += _dot(u, wd_ref[...])

    @pl.when(j == pl.num_programs(1) - 1)
    def _():
        o_ref[...] = x_ref[...] + 0.5 * acc_ref[...]


def _ffn(x, g, wg, wu, wd, l, tm, tf):
    T, D = x.shape
    F = wg.shape[2]
    return pl.pallas_call(
        _ffn_kernel,
        grid=(T // tm, F // tf),
        in_specs=[
            pl.BlockSpec((tm, D), lambda i, j: (i, 0)),
            pl.BlockSpec((None, 1, D), lambda i, j: (l, 0, 0)),
            pl.BlockSpec((None, D, tf), lambda i, j: (l, 0, j)),
            pl.BlockSpec((None, D, tf), lambda i, j: (l, 0, j)),
            pl.BlockSpec((None, tf, D), lambda i, j: (l, j, 0)),
        ],
        out_specs=pl.BlockSpec((tm, D), lambda i, j: (i, 0)),
        out_shape=jax.ShapeDtypeStruct((T, D), F32),
        scratch_shapes=[pltpu.VMEM((tm, D), MXU_DTYPE), pltpu.VMEM((tm, D), F32)],
        compiler_params=_params("parallel", "arbitrary"),
        name="ffn",
    )(x, g, wg, wu, wd)


def _inproj_kernel(x_ref, g_ref, w_ref, o_ref, wb_ref):
    @pl.when(pl.program_id(0) == 0)
    def _():
        wb_ref[...] = w_ref[...].astype(wb_ref.dtype)

    o_ref[...] = _dot(_rms(x_ref[...], g_ref[...]), wb_ref[...])


def _inproj(x, g, w, l, tm):
    T, D = x.shape
    N = w.shape[2]
    return pl.pallas_call(
        _inproj_kernel,
        grid=(T // tm,),
        in_specs=[
            pl.BlockSpec((tm, D), lambda i: (i, 0)),
            pl.BlockSpec((None, 1, D), lambda i: (l, 0, 0)),
            pl.BlockSpec((None, D, N), lambda i: (l, 0, 0), pipeline_mode=pl.Buffered(1)),
        ],
        out_specs=pl.BlockSpec((tm, N), lambda i: (i, 0)),
        out_shape=jax.ShapeDtypeStruct((T, N), F32),
        scratch_shapes=[pltpu.VMEM((D, N), MXU_DTYPE)],
        compiler_params=_params("arbitrary"),
        name="inproj",
    )(x, g, w)


POS_SPLIT = 256
EXT_PAD_LANE = 4
EXT_MASK_LANE = 2 * SUBLANES


def _key_ext(pos, lane, mask_block=0, pad=None):
    ext = jnp.where(lane == 0, _div_pow2(pos, POS_SPLIT),
                    jnp.where(lane == 1, pos & (POS_SPLIT - 1),
                              jnp.where((lane == 2) | (lane == 3), 1, 0)))
    if mask_block:
        ext = jnp.where(lane - EXT_MASK_LANE == _div_pow2(pos, mask_block), 1, ext)
    if pad is not None:
        ext = jnp.where(pad, jnp.where(lane == EXT_PAD_LANE, 1, 0), ext)
    return ext.astype(F32)


def _query_ext(slope, t):
    r = _iota((SUBLANES, t.shape[1]), 0)
    big = slope * float(POS_SPLIT)
    hi = _div_pow2(t, POS_SPLIT).astype(F32)
    lo = (t & (POS_SPLIT - 1)).astype(F32)
    return jnp.where(r == 0, big, jnp.where(r == 1, slope, jnp.where(r == 2, -(big * hi),
                     jnp.where(r == 3, -(slope * lo), jnp.where(r == EXT_PAD_LANE, NEG, 0.0)))))


def _softmax_cols(s):
    p = jnp.exp(s - jnp.max(s, axis=0, keepdims=True))
    return p, 1.0 / jnp.sum(p, axis=0, keepdims=True)


def _rank_before(score, n, idx):
    rank = jnp.zeros(score.shape, jnp.int32)
    for m in range(n):
        row = score[m:m + 1, :]
        better = (row > score) | ((row == score) & (m < idx))
        rank = rank + better.astype(jnp.int32)
    return rank


def _moba_kernel(slopes_ref, q_ref, k_ref, v_ref, o_ref,
                 kmean_ref, kb_ref, vt_ref, row_ref, *, nb, kk):
    hp = pl.program_id(1)
    own = pl.program_id(2)
    blk = tq = MOBA_BLOCK
    cols = HEADS_PER_TILE * tq

    @pl.when(own == 0)
    def _():
        kmean_ref[...] = jnp.zeros_like(kmean_ref)
        for n in range(nb):
            rows = slice(n * blk, (n + 1) * blk)
            kmean_ref[n:n + 1, :] = jnp.mean(k_ref[0, rows, :], axis=0, keepdims=True)
            vt_ref[:, rows] = v_ref[0, rows, :].T.astype(vt_ref.dtype)
        kb_ref[...] = k_ref[0].astype(kb_ref.dtype)

    q0 = own * tq
    lo = _iota((tq, LANES), 1) < HEAD_DIM
    q = q_ref[0]
    qs = jnp.concatenate([jnp.where(lo, q, 0.0), jnp.where(lo, 0.0, q)], axis=0)
    col1 = _iota((1, cols), 1)
    slope = jnp.where(col1 < tq, slopes_ref[HEADS_PER_TILE * hp], slopes_ref[HEADS_PER_TILE * hp + 1])

    gate = _dot_nt(kmean_ref[...], qs, precision=HI)
    n_idx = _iota(gate.shape, 0)
    cand = n_idx < own
    gm = jnp.where(cand, gate, NEG)
    sel = cand & (_rank_before(gm, nb - 1, n_idx) < kk)
    row_ref[...] = slope * (n_idx * blk - q0).astype(F32) + jnp.where(sel, 0.0, NEG)

    qsb = (qs * SCALE).astype(MXU_DTYPE)
    rel = _iota((blk, cols), 0) - (_iota((blk, cols), 1) & (tq - 1))
    bias = slope * rel.astype(F32)

    def attend(c):
        nkeys = (c + 1) * blk
        s = _dot_nt(kb_ref[0:nkeys, :], qsb)
        pieces = [s[n * blk:(n + 1) * blk] + bias + row_ref[n:n + 1, :] for n in range(c)]
        pieces.append(jnp.where(rel <= 0, s[c * blk:nkeys] + bias, NEG))
        s = jnp.concatenate(pieces, axis=0)
        p = jnp.exp(s - jnp.max(s, axis=0, keepdims=True))
        inv = 1.0 / jnp.sum(p, axis=0, keepdims=True)
        pb = p.astype(MXU_DTYPE)
        halves = []
        for h in range(HEADS_PER_TILE):
            dims = slice(h * HEAD_DIM, (h + 1) * HEAD_DIM)
            qcols = slice(h * tq, (h + 1) * tq)
            halves.append(_dot(vt_ref[dims, 0:nkeys], pb[:, qcols]) * inv[:, qcols])
        o_ref[0] = jnp.concatenate(halves, axis=0).T

    for c in range(nb):
        @pl.when(own == c)
        def _(c=c):
            attend(c)


def _moba(slopes, proj):
    B, S, _ = proj.shape
    tq = MOBA_BLOCK
    nb = S // MOBA_BLOCK
    kk = max(1, min(MOBA_TOPK, nb - 1))
    n_tiles = MOBA_HEADS // HEADS_PER_TILE
    kern = functools.partial(_moba_kernel, nb=nb, kk=kk)
    return pl.pallas_call(
        kern,
        grid=(B, n_tiles, S // tq),
        in_specs=[
            pl.BlockSpec(memory_space=pltpu.SMEM),
            pl.BlockSpec((1, tq, LANES), lambda b, h, i: (b, i, COL_QM + h)),
            pl.BlockSpec((1, S, LANES), lambda b, h, i: (b, 0, COL_KM + h)),
            pl.BlockSpec((1, S, LANES), lambda b, h, i: (b, 0, COL_VM + h)),
        ],
        out_specs=pl.BlockSpec((1, tq, LANES), lambda b, h, i: (b, i, h)),
        out_shape=jax.ShapeDtypeStruct((B, S, MOBA_HEADS * HEAD_DIM), F32),
        scratch_shapes=[
            pltpu.VMEM((_round_up(nb, SUBLANES), LANES), F32),
            pltpu.VMEM((S, LANES), MXU_DTYPE),
            pltpu.VMEM((LANES, S), MXU_DTYPE),
            pltpu.VMEM((_round_up(nb, SUBLANES), HEADS_PER_TILE * tq), F32),
        ],
        compiler_params=_params("parallel", "parallel", "arbitrary"),
        name="moba",
    )(slopes, proj, proj, proj)


def _gelu_tanh(x):
    return x * (0.5 * (1.0 + jnp.tanh(0.7978845608028654 * (x + 0.044715 * (x * x * x)))))


def _compress_kernel(xk_ref, xv_ref, pk_ref, pv_ref, w1k_ref, w1v_ref, w2k_ref, w2v_ref, kc_ref, vc_ref):
    g = pl.program_id(0)
    half = (CMP_LEN // 2) * HEAD_DIM
    for x_ref, p_ref, w1_ref, w2_ref, o_ref in ((xk_ref, pk_ref, w1k_ref, w2k_ref, kc_ref),
                                               (xv_ref, pv_ref, w1v_ref, w2v_ref, vc_ref)):
        B, nr, W = x_ref.shape
        x = x_ref[...].reshape(B * nr, W)
        top = _dot_hi(x + p_ref[:, 0:half], w1_ref[0:half, :])
        bot = _dot_hi(x + p_ref[:, half:2 * half], w1_ref[half:2 * half, :])
        hid = top + pltpu.roll(bot, B * nr - 1, 0)
        out = _dot_hi(_gelu_tanh(hid), w2_ref[...]).reshape(B, nr, LANES)

        @pl.when(g == 0)
        def _(o_ref=o_ref, out=out):
            o_ref[...] = out

        @pl.when(g > 0)
        def _(o_ref=o_ref, out=out):
            o_ref[...] += out


def _compress(xk, xv, pk, pv, w1k, w1v, w2k, w2v, l):
    B, G, nr, W = xk.shape
    H = w1k.shape[2]
    layer = lambda *shape: pl.BlockSpec((None,) + shape, lambda g: (l,) + (0,) * len(shape))
    xspec = pl.BlockSpec((B, None, nr, W), lambda g: (0, g, 0, 0))
    w2spec = pl.BlockSpec((None, H, LANES), lambda g: (g, 0, 0))
    ospec = pl.BlockSpec((B, nr, LANES), lambda g: (0, 0, 0))
    oshape = jax.ShapeDtypeStruct((B, nr, LANES), F32)
    return pl.pallas_call(
        _compress_kernel,
        grid=(G,),
        in_specs=[xspec, xspec, layer(1, 2 * W), layer(1, 2 * W), layer(2 * W, H), layer(2 * W, H),
                  w2spec, w2spec],
        out_specs=(ospec, ospec),
        out_shape=(oshape, oshape),
        compiler_params=_params("arbitrary"),
        name="nsa_compress",
    )(xk, xv, pk, pv, w1k, w1v, w2k, w2v)


def _nsa_kernel(slopes_ref, qa_ref, qb_ref, g_ref, kc_ref, vc_ref, ks_ref, vs_ref, kw_ref, vw_ref,
                o_ref, kcx, vct, ksx, vst, kwx, vwt, qx_ref, gt_ref, ocmp_ref, owin_ref,
                *, S, tq, n_cmp, n_slc, k_top):
    g = pl.program_id(1)
    qi = pl.program_id(2)
    q0 = qi * tq
    cols = NSA_GROUP * tq
    sblk = SLC_CHUNK
    nr = kcx.shape[0]
    nj = _round_up(n_slc, 2 * SUBLANES)
    n_back = WINDOW // tq
    ext0 = HEAD_DIM

    @pl.when(qi == 0)
    def _():
        def both_halves(x):
            mine = (_iota(x.shape, 1) >= HEAD_DIM) == (g == 1)
            return jnp.where(mine, x, pltpu.roll(x, HEAD_DIM, 1))

        def with_ext(x, pos, mask_block=0):
            lane = _iota(x.shape, 1)
            return jnp.where(lane < HEAD_DIM, both_halves(x), _key_ext(pos, lane - ext0, mask_block)).astype(MXU_DTYPE)

        def transposed(x):
            return both_halves(x).T[0:HEAD_DIM].astype(MXU_DTYPE)

        qx_ref[...] = jnp.zeros_like(qx_ref)
        kcx[...] = with_ext(kc_ref[0], _iota((nr, LANES), 0) * CMP_STRIDE + (CMP_LEN - 1))
        vct[...] = transposed(vc_ref[0])
        lane = _iota((WINDOW, LANES), 1)
        kwx[0:WINDOW, :] = _key_ext(lane * 0, lane - ext0, pad=lane >= 0).astype(MXU_DTYPE)
        for c in range(n_back):
            vwt[c] = jnp.zeros(vwt.shape[1:], vwt.dtype)
        for c in range(S // sblk):
            rows = slice(c * sblk, (c + 1) * sblk)
            pos = c * sblk + _iota((sblk, LANES), 0)
            ksx[rows, :] = with_ext(ks_ref[0, rows, :], pos, SLC_BLOCK)
            kwx[WINDOW + c * sblk:WINDOW + (c + 1) * sblk, :] = with_ext(kw_ref[0, rows, :], pos)
            vst[:, rows] = transposed(vs_ref[0, rows, :])
        for c in range(S // tq):
            vwt[n_back + c] = transposed(vw_ref[0, c * tq:(c + 1) * tq, :])

    qat = qa_ref[0].T
    qbt = qb_ref[0].T
    qt = jnp.concatenate([qat[0:HEAD_DIM], qat[HEAD_DIM:LANES], qbt[0:HEAD_DIM], qbt[HEAD_DIM:LANES]], axis=1)
    col1 = _iota((1, cols), 1)
    h0 = NSA_GROUP * g
    slope = jnp.where(col1 < tq, slopes_ref[h0],
                      jnp.where(col1 < 2 * tq, slopes_ref[h0 + 1],
                                jnp.where(col1 < 3 * tq, slopes_ref[h0 + 2], slopes_ref[h0 + 3])))
    i1 = col1 & (tq - 1)
    t1 = q0 + i1
    qx_ref[0:HEAD_DIM, :] = (qt * SCALE).astype(qx_ref.dtype)
    qx_ref[ext0:ext0 + EXT_MASK_LANE, :] = jnp.concatenate(
        [_query_ext(slope, t1), jnp.zeros((EXT_MASK_LANE - SUBLANES, cols), F32)], axis=0).astype(qx_ref.dtype)
    qx = qx_ref[...]

    relw = _iota((tq, cols), 0) - i1
    s = jnp.dot(kwx[pl.ds(pl.multiple_of(q0, tq), WINDOW + tq), :], qx, preferred_element_type=F32)
    s = jnp.concatenate([jnp.where(relw > 0, s[0:tq], NEG), s[tq:WINDOW],
                         jnp.where(relw <= 0, s[WINDOW:WINDOW + tq], NEG)], axis=0)
    p, inv = _softmax_cols(s)
    pb = p.astype(MXU_DTYPE)
    acc = _dot(vwt[qi], pb[0:tq])
    for e in range(1, n_back + 1):
        acc = acc + _dot(vwt[qi + e], pb[e * tq:(e + 1) * tq])
    owin_ref[...] = acc * inv

    nidx = _iota((nr, cols), 0)
    maskc = (nidx * CMP_STRIDE + (CMP_LEN - 1) <= t1) & (nidx < n_cmp)
    s = jnp.where(maskc, jnp.dot(kcx[...], qx, preferred_element_type=F32), NEG)
    p = jnp.where(maskc, jnp.exp(s - jnp.max(s, axis=0, keepdims=True)), 0.0)
    p = p / jnp.maximum(jnp.sum(p, axis=0, keepdims=True), 1e-30)
    ocmp_ref[...] = _dot(vct[...], p)

    psum = p[:, 0:tq] + p[:, tq:2 * tq] + p[:, 2 * tq:3 * tq] + p[:, 3 * tq:4 * tq]
    jj = _iota((nj, nr), 0) * SLC_BLOCK
    nn = _iota((nj, nr), 1) * CMP_STRIDE
    overlap = ((nn < jj + SLC_BLOCK) & (nn + CMP_LEN > jj)
               & (nn < n_cmp * CMP_STRIDE) & (jj < n_slc * SLC_BLOCK)).astype(F32)
    imp = _dot_hi(overlap, psum)
    jl = _iota((nj, tq), 0)
    tb = _div_pow2(q0 + _iota((nj, tq), 1), SLC_BLOCK)
    cand = jl <= tb
    forced = (jl == 0) | (jl == tb) | (jl == tb - 1)
    score = jnp.where(cand, jnp.where(forced, SLC_FORCE, imp), NEG)
    keep = cand & (_rank_before(score, n_slc, jl) < k_top)
    mask0 = ext0 + EXT_MASK_LANE
    qx_ref[mask0:mask0 + nj, :] = jnp.concatenate([jnp.where(keep, 0.0, NEG)] * NSA_GROUP, axis=1).astype(qx_ref.dtype)

    z = g_ref[0].T
    gt_ref[...] = 1.0 / (1.0 + jnp.exp(-z))

    def combine(oslc):
        ocmp = ocmp_ref[...]
        owin = owin_ref[...]
        heads = []
        for r in range(NSA_GROUP):
            sl = slice(r * tq, (r + 1) * tq)
            c = NSA_BRANCHES * (NSA_GROUP * g + r)
            heads.append(gt_ref[pl.ds(c, 1), :] * ocmp[:, sl] + gt_ref[pl.ds(c + 1, 1), :] * oslc[:, sl]
                         + gt_ref[pl.ds(c + 2, 1), :] * owin[:, sl])
        o_ref[0, :, 0:LANES] = jnp.concatenate(heads[0:2], axis=0).T
        o_ref[0, :, LANES:2 * LANES] = jnp.concatenate(heads[2:4], axis=0).T

    def slc_attend(c):
        nkeys = (c + 1) * sblk
        s = jnp.dot(ksx[0:nkeys, :], qx_ref[...], preferred_element_type=F32)
        causal = _iota((sblk, cols), 0) + (c * sblk - q0) <= i1
        diag = jnp.where(causal, s[c * sblk:nkeys], NEG)
        s = jnp.concatenate([s[0:c * sblk], diag], axis=0) if c else diag
        p, inv = _softmax_cols(s)
        combine(_dot(vst[:, 0:nkeys], p) * inv)

    dc = _div_pow2(q0, sblk)
    for c in range(S // sblk):
        @pl.when(dc == c)
        def _(c=c):
            slc_attend(c)


def _nsa(slopes, proj, kc, vc):
    B, S, _ = proj.shape
    tq = NSA_QTILE
    n_cmp = (S - CMP_LEN) // CMP_STRIDE + 1
    n_slc = S // SLC_BLOCK
    k_top = min(SLC_TOPK, n_slc)
    nr = kc.shape[1]
    cols = NSA_GROUP * tq
    assert HEAD_DIM + EXT_MASK_LANE + _round_up(n_slc, 2 * SUBLANES) <= LANES and S // POS_SPLIT <= POS_SPLIT
    kern = functools.partial(_nsa_kernel, S=S, tq=tq, n_cmp=n_cmp, n_slc=n_slc, k_top=k_top)
    qtiles = NSA_GROUP // HEADS_PER_TILE

    def seq(col):
        return pl.BlockSpec((1, S, LANES), lambda b, g, i: (b, 0, col))

    small = pl.BlockSpec((1, nr, LANES), lambda b, g, i: (b, 0, 0))
    return pl.pallas_call(
        kern,
        grid=(B, NSA_KV_HEADS, S // tq),
        in_specs=[
            pl.BlockSpec(memory_space=pltpu.SMEM),
            pl.BlockSpec((1, tq, LANES), lambda b, g, i: (b, i, COL_QN + qtiles * g)),
            pl.BlockSpec((1, tq, LANES), lambda b, g, i: (b, i, COL_QN + qtiles * g + 1)),
            pl.BlockSpec((1, tq, LANES), lambda b, g, i: (b, i, COL_G)),
            small, small, seq(COL_KS), seq(COL_VS), seq(COL_KW), seq(COL_VW),
        ],
        out_specs=pl.BlockSpec((1, tq, qtiles * LANES), lambda b, g, i: (b, i, g)),
        out_shape=jax.ShapeDtypeStruct((B, S, NSA_HEADS * HEAD_DIM), F32),
        scratch_shapes=[
            pltpu.VMEM((nr, LANES), MXU_DTYPE), pltpu.VMEM((HEAD_DIM, nr), MXU_DTYPE),
            pltpu.VMEM((S, LANES), MXU_DTYPE), pltpu.VMEM((HEAD_DIM, S), MXU_DTYPE),
            pltpu.VMEM((WINDOW + S, LANES), MXU_DTYPE),
            pltpu.VMEM((WINDOW // tq + S // tq, HEAD_DIM, tq), MXU_DTYPE),
            pltpu.VMEM((LANES, cols), MXU_DTYPE),
            pltpu.VMEM((LANES, tq), F32),
            pltpu.VMEM((HEAD_DIM, cols), F32), pltpu.VMEM((HEAD_DIM, cols), F32),
        ],
        compiler_params=_params("parallel", "parallel", "arbitrary"),
        name="nsa",
    )(slopes, proj, proj, proj, kc, vc, proj, proj, proj, proj)


def _outproj_kernel(om_ref, on_ref, x_ref, gm_ref, gn_ref, wm_ref, wn_ref, o_ref):
    y = _dot(_rms(om_ref[...], gm_ref[...]), wm_ref[...]) + _dot(_rms(on_ref[...], gn_ref[...]), wn_ref[...])
    o_ref[...] = x_ref[...] + y


def _outproj(om, on, x, gm, gn, w, l, tm):
    T, D = x.shape
    Wm, Wn = om.shape[1], on.shape[1]
    assert Wm == Wn and w.shape[1] == Wm + Wn
    return pl.pallas_call(
        _outproj_kernel,
        grid=(T // tm,),
        in_specs=[
            pl.BlockSpec((tm, Wm), lambda i: (i, 0)),
            pl.BlockSpec((tm, Wn), lambda i: (i, 0)),
            pl.BlockSpec((tm, D), lambda i: (i, 0)),
            pl.BlockSpec((None, 1, Wm), lambda i: (l, 0, 0)),
            pl.BlockSpec((None, 1, Wn), lambda i: (l, 0, 0)),
            pl.BlockSpec((None, Wm, D), lambda i: (l, 0, 0)),
            pl.BlockSpec((None, Wn, D), lambda i: (l, 1, 0)),
        ],
        out_specs=pl.BlockSpec((tm, D), lambda i: (i, 0)),
        out_shape=jax.ShapeDtypeStruct((T, D), F32),
        compiler_params=_params("parallel"),
        name="outproj",
    )(om, on, x, gm, gn, w, w)


def _norm_kernel(x_ref, g_ref, o_ref):
    o_ref[...] = _rms(x_ref[...], g_ref[...])


def _final_norm(x, g, tm):
    T, D = x.shape
    return pl.pallas_call(
        _norm_kernel,
        grid=(T // tm,),
        in_specs=[pl.BlockSpec((tm, D), lambda i: (i, 0)), pl.BlockSpec((1, D), lambda i: (0, 0))],
        out_specs=pl.BlockSpec((tm, D), lambda i: (i, 0)),
        out_shape=jax.ShapeDtypeStruct((T, D), F32),
        compiler_params=_params("parallel"),
        name="final_norm",
    )(x, g)


def _alibi_slopes(n):
    slopes = 2.0 ** (-8.0 * np.arange(1, n + 1) / n)
    assert np.all(np.log2(slopes) == np.round(np.log2(slopes)))
    return jnp.asarray(slopes, dtype=F32)


def _token_tile(T, want):
    return want if T % want == 0 else T


def _mixer(x, B, S, l, mix_norm, w_in_p, pos_k, k_w1, k_w2, pos_v, v_w1, v_w2, moba_norm, nsa_norm, w_out):
    T, D = x.shape
    proj = _inproj(x, mix_norm, w_in_p, l, _token_tile(T, 512)).reshape(B, S, IN_PAD)

    o_m = _moba(_alibi_slopes(MOBA_HEADS), proj)

    nr = S // CMP_STRIDE

    def windows(col):
        z = proj[:, :, col * LANES:(col + 1) * LANES].reshape(B, S, NSA_KV_HEADS, HEAD_DIM)
        return z.transpose(0, 2, 1, 3).reshape(B, NSA_KV_HEADS, nr, CMP_STRIDE * HEAD_DIM)

    def placed(w2):
        return jnp.stack([jnp.pad(w2, ((0, 0), (g * HEAD_DIM, LANES - (g + 1) * HEAD_DIM)))
                          for g in range(NSA_KV_HEADS)])

    kc, vc = _compress(windows(COL_KC), windows(COL_VC), pos_k, pos_v, k_w1, v_w1, placed(k_w2), placed(v_w2), l)
    o_n = _nsa(_alibi_slopes(NSA_HEADS), proj, kc, vc)

    return _outproj(o_m.reshape(T, -1), o_n.reshape(T, -1), x, moba_norm, nsa_norm, w_out, l,
                    _token_tile(T, 512))


def _swiglu_step(x, l, norm, w_gate, w_up, w_down):
    T = x.shape[0]
    return _ffn(x, norm, w_gate, w_up, w_down, l, _token_tile(T, 1024), 256)


def _rows(p):
    return p.reshape(p.shape[0], 1, -1)


@jax.jit
def kernel(x, ffa_norm, ffa_w_gate, ffa_w_up, ffa_w_down, mix_norm, w_in, cmp_pos_k, cmp_k_w1, cmp_k_w2,
           cmp_pos_v, cmp_v_w1, cmp_v_w2, moba_out_norm, nsa_out_norm, w_out, ffb_norm, ffb_w_gate,
           ffb_w_up, ffb_w_down, final_norm):
    B, S, D = x.shape
    assert S % MOBA_BLOCK == 0 and S >= WINDOW and w_in.shape[-1] == IN_WIDTH
    h = x.reshape(B * S, D)
    w_in_p = jnp.pad(w_in, ((0, 0), (0, 0), (0, IN_PAD - IN_WIDTH)))
    ffa_norm, ffb_norm, mix_norm = _rows(ffa_norm), _rows(ffb_norm), _rows(mix_norm)
    moba_out_norm, nsa_out_norm = _rows(moba_out_norm), _rows(nsa_out_norm)
    cmp_pos_k, cmp_pos_v = _rows(cmp_pos_k), _rows(cmp_pos_v)
    for l in range(ffa_norm.shape[0]):
        h = _swiglu_step(h, l, ffa_norm, ffa_w_gate, ffa_w_up, ffa_w_down)
        h = _mixer(h, B, S, l, mix_norm, w_in_p, cmp_pos_k, cmp_k_w1, cmp_k_w2[l],
                   cmp_pos_v, cmp_v_w1, cmp_v_w2[l], moba_out_norm, nsa_out_norm, w_out)
        h = _swiglu_step(h, l, ffb_norm, ffb_w_gate, ffb_w_up, ffb_w_down)
    return _final_norm(h, final_norm[None], _token_tile(B * S, 1024)).reshape(B, S, D)
```

```python
import functools

import numpy as np
import jax
import jax.numpy as jnp
from jax import lax
from jax.experimental import pallas as pl
from jax.experimental.pallas import tpu as pltpu

HEAD_DIM = 64
MOBA_HEADS = 8
NSA_HEADS = 8
NSA_KV_HEADS = 2
NSA_GROUP = NSA_HEADS // NSA_KV_HEADS
NSA_BRANCHES = 3
MOBA_BLOCK = 256
MOBA_TOPK = 3
CMP_LEN = 32
CMP_STRIDE = 16
SLC_BLOCK = 64
SLC_TOPK = 16
WINDOW = 512
NEG = -1e30
SLC_FORCE = 1e4
EPS = 1e-6
SCALE = HEAD_DIM ** -0.5

LANES = 128
SUBLANES = 8
HEADS_PER_TILE = LANES // HEAD_DIM
COL_QM, COL_KM, COL_VM, COL_QN = 0, 4, 8, 12
COL_KC, COL_VC, COL_KS, COL_VS, COL_KW, COL_VW, COL_G = 16, 17, 18, 19, 20, 21, 22
IN_TILES = 24
IN_PAD = IN_TILES * LANES
IN_WIDTH = 2840
NSA_QTILE = 128
SLC_CHUNK = 4 * SLC_BLOCK

MXU_DTYPE = jnp.bfloat16
VMEM_LIMIT = 48 * 1024 * 1024
F32 = jnp.float32
HI = lax.Precision.HIGHEST


def _dot(a, b):
    return jnp.dot(a.astype(MXU_DTYPE), b.astype(MXU_DTYPE), preferred_element_type=F32)


def _dot_nt(a, b, precision=None):
    return lax.dot_general(a, b, (((1,), (1,)), ((), ())), precision=precision,
                           preferred_element_type=F32)


def _dot_hi(a, b):
    return jnp.dot(a, b, precision=HI, preferred_element_type=F32)


def _iota(shape, dim):
    return lax.broadcasted_iota(jnp.int32, shape, dim)


def _div_pow2(x, n):
    assert n & (n - 1) == 0
    return x >> (n.bit_length() - 1)


def _round_up(n, m):
    return -(-n // m) * m


def _params(*sem):
    return pltpu.CompilerParams(dimension_semantics=sem, vmem_limit_bytes=VMEM_LIMIT)


def _rms(x, g):
    return x * lax.rsqrt(jnp.mean(x * x, axis=-1, keepdims=True) + EPS) * g


def _ffn_kernel(x_ref, g_ref, wg_ref, wu_ref, wd_ref, gf_ref, o_ref, h_ref, wgb, wub, wdb, *, nj, final):
    step = pl.program_id(0)

    def start():
        h_ref[...] = _rms(x_ref[...], g_ref[...]).astype(h_ref.dtype)

    def chunk(wg, wu, wd):
        h = h_ref[...]
        a = _dot(h, wg)
        b = _dot(h, wu)
        return _dot(a / (1.0 + jnp.exp(-a)) * b, wd)

    def finish():
        y = x_ref[...] + 0.5 * o_ref[...]
        o_ref[...] = _rms(y, gf_ref[...]) if final else y

    @pl.when(step < nj)
    def _():
        wgb[step] = wg_ref[...].astype(wgb.dtype)
        wub[step] = wu_ref[...].astype(wub.dtype)
        wdb[step] = wd_ref[...].astype(wdb.dtype)

        @pl.when(step == 0)
        def _():
            start()
            o_ref[...] = chunk(wgb[0], wub[0], wdb[0])

        @pl.when(step > 0)
        def _():
            o_ref[...] += chunk(wgb[step], wub[step], wdb[step])

        @pl.when(step == nj - 1)
        def _():
            finish()

    @pl.when(step >= nj)
    def _():
        start()
        o_ref[...] = chunk(wgb[0], wub[0], wdb[0])

        def body(j, carry):
            o_ref[...] += chunk(wgb[j], wub[j], wdb[j])
            return carry

        lax.fori_loop(1, nj, body, 0)
        finish()


def _ffn(x, g, wg, wu, wd, l, tm, tf, final_g, final):
    T, D = x.shape
    F = wg.shape[2]
    nj = F // tf
    tile = lambda s: (jnp.maximum(s - (nj - 1), 0), 0)
    col = lambda s: jnp.minimum(s, nj - 1)
    return pl.pallas_call(
        functools.partial(_ffn_kernel, nj=nj, final=final),
        grid=(nj + T // tm - 1,),
        in_specs=[
            pl.BlockSpec((tm, D), tile),
            pl.BlockSpec((None, 1, D), lambda s: (l, 0, 0)),
            pl.BlockSpec((None, D, tf), lambda s: (l, 0, col(s))),
            pl.BlockSpec((None, D, tf), lambda s: (l, 0, col(s))),
            pl.BlockSpec((None, tf, D), lambda s: (l, col(s), 0)),
            pl.BlockSpec((1, D), lambda s: (0, 0)),
        ],
        out_specs=pl.BlockSpec((tm, D), tile),
        out_shape=jax.ShapeDtypeStruct((T, D), F32),
        scratch_shapes=[pltpu.VMEM((tm, D), MXU_DTYPE), pltpu.VMEM((nj, D, tf), MXU_DTYPE),
                        pltpu.VMEM((nj, D, tf), MXU_DTYPE), pltpu.VMEM((nj, tf, D), MXU_DTYPE)],
        compiler_params=_params("arbitrary"),
        name="ffn",
    )(x, g, wg, wu, wd, final_g)


def _inproj_kernel(x_ref, g_ref, w_ref, o_ref, wb_ref):
    @pl.when(pl.program_id(0) == 0)
    def _():
        wb_ref[...] = w_ref[...].astype(wb_ref.dtype)

    o_ref[...] = _dot(_rms(x_ref[...], g_ref[...]), wb_ref[...])


def _inproj(x, g, w, l, tm):
    T, D = x.shape
    N = w.shape[2]
    return pl.pallas_call(
        _inproj_kernel,
        grid=(T // tm,),
        in_specs=[
            pl.BlockSpec((tm, D), lambda i: (i, 0)),
            pl.BlockSpec((None, 1, D), lambda i: (l, 0, 0)),
            pl.BlockSpec((None, D, N), lambda i: (l, 0, 0), pipeline_mode=pl.Buffered(1)),
        ],
        out_specs=pl.BlockSpec((tm, N), lambda i: (i, 0)),
        out_shape=jax.ShapeDtypeStruct((T, N), F32),
        scratch_shapes=[pltpu.VMEM((D, N), MXU_DTYPE)],
        compiler_params=_params("arbitrary"),
        name="inproj",
    )(x, g, w)


POS_SPLIT = 256
EXT_PAD_LANE = 4
EXT_MASK_LANE = 2 * SUBLANES


def _key_ext(pos, lane, mask_block=0, pad=None):
    ext = jnp.where(lane == 0, _div_pow2(pos, POS_SPLIT),
                    jnp.where(lane == 1, pos & (POS_SPLIT - 1),
                              jnp.where((lane == 2) | (lane == 3), 1, 0)))
    if mask_block:
        ext = jnp.where(lane - EXT_MASK_LANE == _div_pow2(pos, mask_block), 1, ext)
    if pad is not None:
        ext = jnp.where(pad, jnp.where(lane == EXT_PAD_LANE, 1, 0), ext)
    return ext.astype(F32)


def _query_ext(slope, t):
    r = _iota((SUBLANES, t.shape[1]), 0)
    big = slope * float(POS_SPLIT)
    hi = _div_pow2(t, POS_SPLIT).astype(F32)
    lo = (t & (POS_SPLIT - 1)).astype(F32)
    return jnp.where(r == 0, big, jnp.where(r == 1, slope, jnp.where(r == 2, -(big * hi),
                     jnp.where(r == 3, -(slope * lo), jnp.where(r == EXT_PAD_LANE, NEG, 0.0)))))


def _softmax_cols(s):
    p = jnp.exp(s - jnp.max(s, axis=0, keepdims=True))
    return p, 1.0 / jnp.sum(p, axis=0, keepdims=True)


def _rank_before(score, n, idx):
    rank = jnp.zeros(score.shape, jnp.int32)
    for m in range(n):
        row = score[m:m + 1, :]
        better = (row > score) | ((row == score) & (m < idx))
        rank = rank + better.astype(jnp.int32)
    return rank


def _moba_kernel(slopes_ref, q_ref, k_ref, v_ref, o_ref,
                 kmean_ref, kb_ref, vt_ref, row_ref, *, nb, kk):
    hp = pl.program_id(1)
    own = pl.program_id(2)
    blk = tq = MOBA_BLOCK
    cols = HEADS_PER_TILE * tq

    @pl.when(own == 0)
    def _():
        kmean_ref[...] = jnp.zeros_like(kmean_ref)
        for n in range(nb):
            rows = slice(n * blk, (n + 1) * blk)
            kmean_ref[n:n + 1, :] = jnp.mean(k_ref[0, rows, :], axis=0, keepdims=True)
            vt_ref[:, rows] = v_ref[0, rows, :].T.astype(vt_ref.dtype)
        kb_ref[...] = k_ref[0].astype(kb_ref.dtype)

    q0 = own * tq
    lo = _iota((tq, LANES), 1) < HEAD_DIM
    q = q_ref[0]
    qs = jnp.concatenate([jnp.where(lo, q, 0.0), jnp.where(lo, 0.0, q)], axis=0)
    col1 = _iota((1, cols), 1)
    slope = jnp.where(col1 < tq, slopes_ref[HEADS_PER_TILE * hp], slopes_ref[HEADS_PER_TILE * hp + 1])

    gate = _dot_nt(kmean_ref[...], qs, precision=HI)
    n_idx = _iota(gate.shape, 0)
    cand = n_idx < own
    gm = jnp.where(cand, gate, NEG)
    sel = cand & (_rank_before(gm, nb - 1, n_idx) < kk)
    row_ref[...] = slope * (n_idx * blk - q0).astype(F32) + jnp.where(sel, 0.0, NEG)

    qsb = (qs * SCALE).astype(MXU_DTYPE)
    rel = _iota((blk, cols), 0) - (_iota((blk, cols), 1) & (tq - 1))
    bias = slope * rel.astype(F32)

    def attend(c):
        nkeys = (c + 1) * blk
        s = _dot_nt(kb_ref[0:nkeys, :], qsb)
        pieces = [s[n * blk:(n + 1) * blk] + bias + row_ref[n:n + 1, :] for n in range(c)]
        pieces.append(jnp.where(rel <= 0, s[c * blk:nkeys] + bias, NEG))
        s = jnp.concatenate(pieces, axis=0)
        p = jnp.exp(s - jnp.max(s, axis=0, keepdims=True))
        inv = 1.0 / jnp.sum(p, axis=0, keepdims=True)
        pb = p.astype(MXU_DTYPE)
        halves = []
        for h in range(HEADS_PER_TILE):
            dims = slice(h * HEAD_DIM, (h + 1) * HEAD_DIM)
            qcols = slice(h * tq, (h + 1) * tq)
            halves.append(_dot(vt_ref[dims, 0:nkeys], pb[:, qcols]) * inv[:, qcols])
        o_ref[0] = jnp.concatenate(halves, axis=0).T

    for c in range(nb):
        @pl.when(own == c)
        def _(c=c):
            attend(c)


def _moba(slopes, proj):
    B, S, _ = proj.shape
    tq = MOBA_BLOCK
    nb = S // MOBA_BLOCK
    kk = max(1, min(MOBA_TOPK, nb - 1))
    n_tiles = MOBA_HEADS // HEADS_PER_TILE
    kern = functools.partial(_moba_kernel, nb=nb, kk=kk)
    return pl.pallas_call(
        kern,
        grid=(B, n_tiles, S // tq),
        in_specs=[
            pl.BlockSpec(memory_space=pltpu.SMEM),
            pl.BlockSpec((1, tq, LANES), lambda b, h, i: (b, i, COL_QM + h)),
            pl.BlockSpec((1, S, LANES), lambda b, h, i: (b, 0, COL_KM + h)),
            pl.BlockSpec((1, S, LANES), lambda b, h, i: (b, 0, COL_VM + h)),
        ],
        out_specs=pl.BlockSpec((1, tq, LANES), lambda b, h, i: (b, i, h)),
        out_shape=jax.ShapeDtypeStruct((B, S, MOBA_HEADS * HEAD_DIM), F32),
        scratch_shapes=[
            pltpu.VMEM((_round_up(nb, SUBLANES), LANES), F32),
            pltpu.VMEM((S, LANES), MXU_DTYPE),
            pltpu.VMEM((LANES, S), MXU_DTYPE),
            pltpu.VMEM((_round_up(nb, SUBLANES), HEADS_PER_TILE * tq), F32),
        ],
        compiler_params=_params("parallel", "parallel", "arbitrary"),
        name="moba",
    )(slopes, proj, proj, proj)


def _gelu_tanh(x):
    return x * (0.5 * (1.0 + jnp.tanh(0.7978845608028654 * (x + 0.044715 * (x * x * x)))))


def _compress_kernel(xk_ref, xv_ref, pk_ref, pv_ref, w1k_ref, w1v_ref, w2k_ref, w2v_ref, kc_ref, vc_ref):
    g = pl.program_id(0)
    half = (CMP_LEN // 2) * HEAD_DIM
    for x_ref, p_ref, w1_ref, w2_ref, o_ref in ((xk_ref, pk_ref, w1k_ref, w2k_ref, kc_ref),
                                               (xv_ref, pv_ref, w1v_ref, w2v_ref, vc_ref)):
        B, nr, W = x_ref.shape
        x = x_ref[...].reshape(B * nr, W)
        top = _dot_hi(x + p_ref[:, 0:half], w1_ref[0:half, :])
        bot = _dot_hi(x + p_ref[:, half:2 * half], w1_ref[half:2 * half, :])
        hid = top + pltpu.roll(bot, B * nr - 1, 0)
        out = _dot_hi(_gelu_tanh(hid), w2_ref[...]).reshape(B, nr, LANES)

        @pl.when(g == 0)
        def _(o_ref=o_ref, out=out):
            o_ref[...] = out

        @pl.when(g > 0)
        def _(o_ref=o_ref, out=out):
            o_ref[...] += out


def _compress(xk, xv, pk, pv, w1k, w1v, w2k, w2v, l):
    B, G, nr, W = xk.shape
    H = w1k.shape[2]
    layer = lambda *shape: pl.BlockSpec((None,) + shape, lambda g: (l,) + (0,) * len(shape))
    xspec = pl.BlockSpec((B, None, nr, W), lambda g: (0, g, 0, 0))
    w2spec = pl.BlockSpec((None, H, LANES), lambda g: (g, 0, 0))
    ospec = pl.BlockSpec((B, nr, LANES), lambda g: (0, 0, 0))
    oshape = jax.ShapeDtypeStruct((B, nr, LANES), F32)
    return pl.pallas_call(
        _compress_kernel,
        grid=(G,),
        in_specs=[xspec, xspec, layer(1, 2 * W), layer(1, 2 * W), layer(2 * W, H), layer(2 * W, H),
                  w2spec, w2spec],
        out_specs=(ospec, ospec),
        out_shape=(oshape, oshape),
        compiler_params=_params("arbitrary"),
        name="nsa_compress",
    )(xk, xv, pk, pv, w1k, w1v, w2k, w2v)


def _nsa_kernel(slopes_ref, qa_ref, qb_ref, g_ref, kc_ref, vc_ref, ks_ref, vs_ref, kw_ref, vw_ref,
                o_ref, kcx, vct, ksx, vst, kwx, vwt, qx_ref, gt_ref, ocmp_ref, owin_ref,
                *, S, tq, n_cmp, n_slc, k_top):
    g = pl.program_id(1)
    qi = pl.program_id(2)
    q0 = qi * tq
    cols = NSA_GROUP * tq
    sblk = SLC_CHUNK
    nr = kcx.shape[0]
    nj = _round_up(n_slc, 2 * SUBLANES)
    n_back = WINDOW // tq
    ext0 = HEAD_DIM

    @pl.when(qi == 0)
    def _():
        def both_halves(x):
            mine = (_iota(x.shape, 1) >= HEAD_DIM) == (g == 1)
            return jnp.where(mine, x, pltpu.roll(x, HEAD_DIM, 1))

        def with_ext(x, pos, mask_block=0):
            lane = _iota(x.shape, 1)
            return jnp.where(lane < HEAD_DIM, both_halves(x), _key_ext(pos, lane - ext0, mask_block)).astype(MXU_DTYPE)

        def transposed(x):
            return both_halves(x).T[0:HEAD_DIM].astype(MXU_DTYPE)

        qx_ref[...] = jnp.zeros_like(qx_ref)
        kcx[...] = with_ext(kc_ref[0], _iota((nr, LANES), 0) * CMP_STRIDE + (CMP_LEN - 1))
        vct[...] = transposed(vc_ref[0])
        lane = _iota((WINDOW, LANES), 1)
        kwx[0:WINDOW, :] = _key_ext(lane * 0, lane - ext0, pad=lane >= 0).astype(MXU_DTYPE)
        for c in range(n_back):
            vwt[c] = jnp.zeros(vwt.shape[1:], vwt.dtype)
        for c in range(S // sblk):
            rows = slice(c * sblk, (c + 1) * sblk)
            pos = c * sblk + _iota((sblk, LANES), 0)
            ksx[rows, :] = with_ext(ks_ref[0, rows, :], pos, SLC_BLOCK)
            kwx[WINDOW + c * sblk:WINDOW + (c + 1) * sblk, :] = with_ext(kw_ref[0, rows, :], pos)
            vst[:, rows] = transposed(vs_ref[0, rows, :])
        for c in range(S // tq):
            vwt[n_back + c] = transposed(vw_ref[0, c * tq:(c + 1) * tq, :])

    qat = qa_ref[0].T
    qbt = qb_ref[0].T
    qt = jnp.concatenate([qat[0:HEAD_DIM], qat[HEAD_DIM:LANES], qbt[0:HEAD_DIM], qbt[HEAD_DIM:LANES]], axis=1)
    col1 = _iota((1, cols), 1)
    h0 = NSA_GROUP * g
    slope = jnp.where(col1 < tq, slopes_ref[h0],
                      jnp.where(col1 < 2 * tq, slopes_ref[h0 + 1],
                                jnp.where(col1 < 3 * tq, slopes_ref[h0 + 2], slopes_ref[h0 + 3])))
    i1 = col1 & (tq - 1)
    t1 = q0 + i1
    qx_ref[0:HEAD_DIM, :] = (qt * SCALE).astype(qx_ref.dtype)
    qx_ref[ext0:ext0 + EXT_MASK_LANE, :] = jnp.concatenate(
        [_query_ext(slope, t1), jnp.zeros((EXT_MASK_LANE - SUBLANES, cols), F32)], axis=0).astype(qx_ref.dtype)
    qx = qx_ref[...]

    relw = _iota((tq, cols), 0) - i1
    s = jnp.dot(kwx[pl.ds(pl.multiple_of(q0, tq), WINDOW + tq), :], qx, preferred_element_type=F32)
    s = jnp.concatenate([jnp.where(relw > 0, s[0:tq], NEG), s[tq:WINDOW],
                         jnp.where(relw <= 0, s[WINDOW:WINDOW + tq], NEG)], axis=0)
    p, inv = _softmax_cols(s)
    pb = p.astype(MXU_DTYPE)
    acc = _dot(vwt[qi], pb[0:tq])
    for e in range(1, n_back + 1):
        acc = acc + _dot(vwt[qi + e], pb[e * tq:(e + 1) * tq])
    owin_ref[...] = acc * inv

    nidx = _iota((nr, cols), 0)
    maskc = (nidx * CMP_STRIDE + (CMP_LEN - 1) <= t1) & (nidx < n_cmp)
    s = jnp.where(maskc, jnp.dot(kcx[...], qx, preferred_element_type=F32), NEG)
    p = jnp.where(maskc, jnp.exp(s - jnp.max(s, axis=0, keepdims=True)), 0.0)
    p = p / jnp.maximum(jnp.sum(p, axis=0, keepdims=True), 1e-30)
    ocmp_ref[...] = _dot(vct[...], p)

    psum = p[:, 0:tq] + p[:, tq:2 * tq] + p[:, 2 * tq:3 * tq] + p[:, 3 * tq:4 * tq]
    jj = _iota((nj, nr), 0) * SLC_BLOCK
    nn = _iota((nj, nr), 1) * CMP_STRIDE
    overlap = ((nn < jj + SLC_BLOCK) & (nn + CMP_LEN > jj)
               & (nn < n_cmp * CMP_STRIDE) & (jj < n_slc * SLC_BLOCK)).astype(F32)
    imp = _dot_hi(overlap, psum)
    jl = _iota((nj, tq), 0)
    tb = _div_pow2(q0 + _iota((nj, tq), 1), SLC_BLOCK)
    cand = jl <= tb
    forced = (jl == 0) | (jl == tb) | (jl == tb - 1)
    score = jnp.where(cand, jnp.where(forced, SLC_FORCE, imp), NEG)
    keep = cand & (_rank_before(score, n_slc, jl) < k_top)
    mask0 = ext0 + EXT_MASK_LANE
    qx_ref[mask0:mask0 + nj, :] = jnp.concatenate([jnp.where(keep, 0.0, NEG)] * NSA_GROUP, axis=1).astype(qx_ref.dtype)

    z = g_ref[0].T
    gt_ref[...] = 1.0 / (1.0 + jnp.exp(-z))

    def combine(oslc):
        ocmp = ocmp_ref[...]
        owin = owin_ref[...]
        heads = []
        for r in range(NSA_GROUP):
            sl = slice(r * tq, (r + 1) * tq)
            c = NSA_BRANCHES * (NSA_GROUP * g + r)
            heads.append(gt_ref[pl.ds(c, 1), :] * ocmp[:, sl] + gt_ref[pl.ds(c + 1, 1), :] * oslc[:, sl]
                         + gt_ref[pl.ds(c + 2, 1), :] * owin[:, sl])
        o_ref[0, :, 0:LANES] = jnp.concatenate(heads[0:2], axis=0).T
        o_ref[0, :, LANES:2 * LANES] = jnp.concatenate(heads[2:4], axis=0).T

    def slc_attend(c):
        nkeys = (c + 1) * sblk
        s = jnp.dot(ksx[0:nkeys, :], qx_ref[...], preferred_element_type=F32)
        causal = _iota((sblk, cols), 0) + (c * sblk - q0) <= i1
        diag = jnp.where(causal, s[c * sblk:nkeys], NEG)
        s = jnp.concatenate([s[0:c * sblk], diag], axis=0) if c else diag
        p, inv = _softmax_cols(s)
        combine(_dot(vst[:, 0:nkeys], p) * inv)

    dc = _div_pow2(q0, sblk)
    for c in range(S // sblk):
        @pl.when(dc == c)
        def _(c=c):
            slc_attend(c)


def _nsa(slopes, proj, kc, vc):
    B, S, _ = proj.shape
    tq = NSA_QTILE
    n_cmp = (S - CMP_LEN) // CMP_STRIDE + 1
    n_slc = S // SLC_BLOCK
    k_top = min(SLC_TOPK, n_slc)
    nr = kc.shape[1]
    cols = NSA_GROUP * tq
    assert HEAD_DIM + EXT_MASK_LANE + _round_up(n_slc, 2 * SUBLANES) <= LANES and S // POS_SPLIT <= POS_SPLIT
    kern = functools.partial(_nsa_kernel, S=S, tq=tq, n_cmp=n_cmp, n_slc=n_slc, k_top=k_top)
    qtiles = NSA_GROUP // HEADS_PER_TILE

    def seq(col):
        return pl.BlockSpec((1, S, LANES), lambda b, g, i: (b, 0, col))

    small = pl.BlockSpec((1, nr, LANES), lambda b, g, i: (b, 0, 0))
    return pl.pallas_call(
        kern,
        grid=(B, NSA_KV_HEADS, S // tq),
        in_specs=[
            pl.BlockSpec(memory_space=pltpu.SMEM),
            pl.BlockSpec((1, tq, LANES), lambda b, g, i: (b, i, COL_QN + qtiles * g)),
            pl.BlockSpec((1, tq, LANES), lambda b, g, i: (b, i, COL_QN + qtiles * g + 1)),
            pl.BlockSpec((1, tq, LANES), lambda b, g, i: (b, i, COL_G)),
            small, small, seq(COL_KS), seq(COL_VS), seq(COL_KW), seq(COL_VW),
        ],
        out_specs=pl.BlockSpec((1, tq, qtiles * LANES), lambda b, g, i: (b, i, g)),
        out_shape=jax.ShapeDtypeStruct((B, S, NSA_HEADS * HEAD_DIM), F32),
        scratch_shapes=[
            pltpu.VMEM((nr, LANES), MXU_DTYPE), pltpu.VMEM((HEAD_DIM, nr), MXU_DTYPE),
            pltpu.VMEM((S, LANES), MXU_DTYPE), pltpu.VMEM((HEAD_DIM, S), MXU_DTYPE),
            pltpu.VMEM((WINDOW + S, LANES), MXU_DTYPE),
            pltpu.VMEM((WINDOW // tq + S // tq, HEAD_DIM, tq), MXU_DTYPE),
            pltpu.VMEM((LANES, cols), MXU_DTYPE),
            pltpu.VMEM((LANES, tq), F32),
            pltpu.VMEM((HEAD_DIM, cols), F32), pltpu.VMEM((HEAD_DIM, cols), F32),
        ],
        compiler_params=_params("parallel", "parallel", "arbitrary"),
        name="nsa",
    )(slopes, proj, proj, proj, kc, vc, proj, proj, proj, proj)


def _outproj_kernel(om_ref, on_ref, x_ref, gm_ref, gn_ref, wm_ref, wn_ref, o_ref):
    y = _dot(_rms(om_ref[...], gm_ref[...]), wm_ref[...]) + _dot(_rms(on_ref[...], gn_ref[...]), wn_ref[...])
    o_ref[...] = x_ref[...] + y


def _outproj(om, on, x, gm, gn, w, l, tm):
    T, D = x.shape
    Wm, Wn = om.shape[1], on.shape[1]
    assert Wm == Wn and w.shape[1] == Wm + Wn
    return pl.pallas_call(
        _outproj_kernel,
        grid=(T // tm,),
        in_specs=[
            pl.BlockSpec((tm, Wm), lambda i: (i, 0)),
            pl.BlockSpec((tm, Wn), lambda i: (i, 0)),
            pl.BlockSpec((tm, D), lambda i: (i, 0)),
            pl.BlockSpec((None, 1, Wm), lambda i: (l, 0, 0)),
            pl.BlockSpec((None, 1, Wn), lambda i: (l, 0, 0)),
            pl.BlockSpec((None, Wm, D), lambda i: (l, 0, 0)),
            pl.BlockSpec((None, Wn, D), lambda i: (l, 1, 0)),
        ],
        out_specs=pl.BlockSpec((tm, D), lambda i: (i, 0)),
        out_shape=jax.ShapeDtypeStruct((T, D), F32),
        compiler_params=_params("parallel"),
        name="outproj",
    )(om, on, x, gm, gn, w, w)


def _alibi_slopes(n):
    slopes = 2.0 ** (-8.0 * np.arange(1, n + 1) / n)
    assert np.all(np.log2(slopes) == np.round(np.log2(slopes)))
    return jnp.asarray(slopes, dtype=F32)


def _token_tile(T, want):
    return want if T % want == 0 else T


def _mixer(x, B, S, l, mix_norm, w_in_p, pos_k, k_w1, k_w2, pos_v, v_w1, v_w2, moba_norm, nsa_norm, w_out):
    T, D = x.shape
    proj = _inproj(x, mix_norm, w_in_p, l, _token_tile(T, 512)).reshape(B, S, IN_PAD)

    o_m = _moba(_alibi_slopes(MOBA_HEADS), proj)

    nr = S // CMP_STRIDE

    def windows(col):
        z = proj[:, :, col * LANES:(col + 1) * LANES].reshape(B, S, NSA_KV_HEADS, HEAD_DIM)
        return z.transpose(0, 2, 1, 3).reshape(B, NSA_KV_HEADS, nr, CMP_STRIDE * HEAD_DIM)

    def placed(w2):
        return jnp.stack([jnp.pad(w2, ((0, 0), (g * HEAD_DIM, LANES - (g + 1) * HEAD_DIM)))
                          for g in range(NSA_KV_HEADS)])

    kc, vc = _compress(windows(COL_KC), windows(COL_VC), pos_k, pos_v, k_w1, v_w1, placed(k_w2), placed(v_w2), l)
    o_n = _nsa(_alibi_slopes(NSA_HEADS), proj, kc, vc)

    return _outproj(o_m.reshape(T, -1), o_n.reshape(T, -1), x, moba_norm, nsa_norm, w_out, l,
                    _token_tile(T, 512))


def _swiglu_step(x, l, norm, w_gate, w_up, w_down, final_g, final=False):
    T = x.shape[0]
    return _ffn(x, norm, w_gate, w_up, w_down, l, _token_tile(T, 1024), 256, final_g, final)


def _rows(p):
    return p.reshape(p.shape[0], 1, -1)


@jax.jit
def kernel(x, ffa_norm, ffa_w_gate, ffa_w_up, ffa_w_down, mix_norm, w_in, cmp_pos_k, cmp_k_w1, cmp_k_w2,
           cmp_pos_v, cmp_v_w1, cmp_v_w2, moba_out_norm, nsa_out_norm, w_out, ffb_norm, ffb_w_gate,
           ffb_w_up, ffb_w_down, final_norm):
    B, S, D = x.shape
    assert S % MOBA_BLOCK == 0 and S >= WINDOW and w_in.shape[-1] == IN_WIDTH
    h = x.reshape(B * S, D)
    w_in_p = jnp.pad(w_in, ((0, 0), (0, 0), (0, IN_PAD - IN_WIDTH)))
    ffa_norm, ffb_norm, mix_norm = _rows(ffa_norm), _rows(ffb_norm), _rows(mix_norm)
    moba_out_norm, nsa_out_norm = _rows(moba_out_norm), _rows(nsa_out_norm)
    cmp_pos_k, cmp_pos_v = _rows(cmp_pos_k), _rows(cmp_pos_v)
    depth = ffa_norm.shape[0]
    final_g = final_norm[None]
    for l in range(depth):
        h = _swiglu_step(h, l, ffa_norm, ffa_w_gate, ffa_w_up, ffa_w_down, final_g)
        h = _mixer(h, B, S, l, mix_norm, w_in_p, cmp_pos_k, cmp_k_w1, cmp_k_w2[l],
                   cmp_pos_v, cmp_v_w1, cmp_v_w2[l], moba_out_norm, nsa_out_norm, w_out)
        h = _swiglu_step(h, l, ffb_norm, ffb_w_gate, ffb_w_up, ffb_w_down, final_g, final=l == depth - 1)
    return h.reshape(B, S, D)
```

```python
import functools

import numpy as np
import jax
import jax.numpy as jnp
from jax import lax
from jax.experimental import pallas as pl
from jax.experimental.pallas import tpu as pltpu

HEAD_DIM = 64
MOBA_HEADS = 8
NSA_HEADS = 8
NSA_KV_HEADS = 2
NSA_GROUP = NSA_HEADS // NSA_KV_HEADS
NSA_BRANCHES = 3
MOBA_BLOCK = 256
MOBA_TOPK = 3
CMP_LEN = 32
CMP_STRIDE = 16
SLC_BLOCK = 64
SLC_TOPK = 16
WINDOW = 512
NEG = -1e30
SLC_FORCE = 1e4
EPS = 1e-6
SCALE = HEAD_DIM ** -0.5

LANES = 128
SUBLANES = 8
HEADS_PER_TILE = LANES // HEAD_DIM
COL_QM, COL_KM, COL_VM, COL_QN = 0, 4, 8, 12
COL_KC, COL_VC, COL_KS, COL_VS, COL_KW, COL_VW, COL_G = 16, 17, 18, 19, 20, 21, 22
IN_TILES = 24
IN_PAD = IN_TILES * LANES
IN_WIDTH = 2840
NSA_QTILE = 256
SLC_CHUNK = 4 * SLC_BLOCK

MXU_DTYPE = jnp.bfloat16
VMEM_LIMIT = 48 * 1024 * 1024
F32 = jnp.float32
HI = lax.Precision.HIGHEST


def _dot(a, b):
    return jnp.dot(a.astype(MXU_DTYPE), b.astype(MXU_DTYPE), preferred_element_type=F32)


def _dot_nt(a, b, precision=None):
    return lax.dot_general(a, b, (((1,), (1,)), ((), ())), precision=precision,
                           preferred_element_type=F32)


def _dot_hi(a, b):
    return jnp.dot(a, b, precision=HI, preferred_element_type=F32)


def _iota(shape, dim):
    return lax.broadcasted_iota(jnp.int32, shape, dim)


def _div_pow2(x, n):
    assert n & (n - 1) == 0
    return x >> (n.bit_length() - 1)


def _round_up(n, m):
    return -(-n // m) * m


def _params(*sem):
    return pltpu.CompilerParams(dimension_semantics=sem, vmem_limit_bytes=VMEM_LIMIT)


def _rms(x, g):
    return x * lax.rsqrt(jnp.mean(x * x, axis=-1, keepdims=True) + EPS) * g


def _ffn_kernel(x_ref, g_ref, wg_ref, wu_ref, wd_ref, gf_ref, o_ref, h_ref, wgb, wub, wdb, *, nj, final):
    step = pl.program_id(0)

    def start():
        h_ref[...] = _rms(x_ref[...], g_ref[...]).astype(h_ref.dtype)

    def chunk(wg, wu, wd):
        h = h_ref[...]
        a = _dot(h, wg)
        b = _dot(h, wu)
        return _dot(a / (1.0 + jnp.exp(-a)) * b, wd)

    def finish():
        y = x_ref[...] + 0.5 * o_ref[...]
        o_ref[...] = _rms(y, gf_ref[...]) if final else y

    @pl.when(step < nj)
    def _():
        wgb[step] = wg_ref[...].astype(wgb.dtype)
        wub[step] = wu_ref[...].astype(wub.dtype)
        wdb[step] = wd_ref[...].astype(wdb.dtype)

        @pl.when(step == 0)
        def _():
            start()
            o_ref[...] = chunk(wgb[0], wub[0], wdb[0])

        @pl.when(step > 0)
        def _():
            o_ref[...] += chunk(wgb[step], wub[step], wdb[step])

        @pl.when(step == nj - 1)
        def _():
            finish()

    @pl.when(step >= nj)
    def _():
        start()
        o_ref[...] = chunk(wgb[0], wub[0], wdb[0])

        def body(j, carry):
            o_ref[...] += chunk(wgb[j], wub[j], wdb[j])
            return carry

        lax.fori_loop(1, nj, body, 0, unroll=2)
        finish()


def _ffn(x, g, wg, wu, wd, l, tm, tf, final_g, final):
    T, D = x.shape
    F = wg.shape[2]
    nj = F // tf
    tile = lambda s: (jnp.maximum(s - (nj - 1), 0), 0)
    col = lambda s: jnp.minimum(s, nj - 1)
    return pl.pallas_call(
        functools.partial(_ffn_kernel, nj=nj, final=final),
        grid=(nj + T // tm - 1,),
        in_specs=[
            pl.BlockSpec((tm, D), tile),
            pl.BlockSpec((None, 1, D), lambda s: (l, 0, 0)),
            pl.BlockSpec((None, D, tf), lambda s: (l, 0, col(s))),
            pl.BlockSpec((None, D, tf), lambda s: (l, 0, col(s))),
            pl.BlockSpec((None, tf, D), lambda s: (l, col(s), 0)),
            pl.BlockSpec((1, D), lambda s: (0, 0)),
        ],
        out_specs=pl.BlockSpec((tm, D), tile),
        out_shape=jax.ShapeDtypeStruct((T, D), F32),
        scratch_shapes=[pltpu.VMEM((tm, D), MXU_DTYPE), pltpu.VMEM((nj, D, tf), MXU_DTYPE),
                        pltpu.VMEM((nj, D, tf), MXU_DTYPE), pltpu.VMEM((nj, tf, D), MXU_DTYPE)],
        compiler_params=_params("arbitrary"),
        name="ffn",
    )(x, g, wg, wu, wd, final_g)


def _inproj_kernel(x_ref, g_ref, w_ref, o_ref, wb_ref):
    @pl.when(pl.program_id(0) == 0)
    def _():
        wb_ref[...] = w_ref[...].astype(wb_ref.dtype)

    o_ref[...] = _dot(_rms(x_ref[...], g_ref[...]), wb_ref[...])


def _inproj(x, g, w, l, tm):
    T, D = x.shape
    N = w.shape[2]
    return pl.pallas_call(
        _inproj_kernel,
        grid=(T // tm,),
        in_specs=[
            pl.BlockSpec((tm, D), lambda i: (i, 0)),
            pl.BlockSpec((None, 1, D), lambda i: (l, 0, 0)),
            pl.BlockSpec((None, D, N), lambda i: (l, 0, 0), pipeline_mode=pl.Buffered(1)),
        ],
        out_specs=pl.BlockSpec((tm, N), lambda i: (i, 0)),
        out_shape=jax.ShapeDtypeStruct((T, N), F32),
        scratch_shapes=[pltpu.VMEM((D, N), MXU_DTYPE)],
        compiler_params=_params("arbitrary"),
        name="inproj",
    )(x, g, w)


POS_SPLIT = 256
EXT_PAD_LANE = 4
EXT_MASK_LANE = 2 * SUBLANES


def _key_ext(pos, lane, mask_block=0, pad=None):
    ext = jnp.where(lane == 0, _div_pow2(pos, POS_SPLIT),
                    jnp.where(lane == 1, pos & (POS_SPLIT - 1),
                              jnp.where((lane == 2) | (lane == 3), 1, 0)))
    if mask_block:
        ext = jnp.where(lane - EXT_MASK_LANE == _div_pow2(pos, mask_block), 1, ext)
    if pad is not None:
        ext = jnp.where(pad, jnp.where(lane == EXT_PAD_LANE, 1, 0), ext)
    return ext.astype(F32)


def _query_ext(slope, t):
    r = _iota((SUBLANES, t.shape[1]), 0)
    big = slope * float(POS_SPLIT)
    hi = _div_pow2(t, POS_SPLIT).astype(F32)
    lo = (t & (POS_SPLIT - 1)).astype(F32)
    return jnp.where(r == 0, big, jnp.where(r == 1, slope, jnp.where(r == 2, -(big * hi),
                     jnp.where(r == 3, -(slope * lo), jnp.where(r == EXT_PAD_LANE, NEG, 0.0)))))


def _softmax_cols(s):
    p = jnp.exp(s - jnp.max(s, axis=0, keepdims=True))
    return p, 1.0 / jnp.sum(p, axis=0, keepdims=True)


def _rank_before(score, n, idx):
    rank = jnp.zeros(score.shape, jnp.int32)
    for m in range(n):
        row = score[m:m + 1, :]
        better = (row > score) | ((row == score) & (m < idx))
        rank = rank + better.astype(jnp.int32)
    return rank


def _moba_kernel(slopes_ref, q_ref, k_ref, v_ref, o_ref,
                 kmean_ref, kb_ref, vt_ref, row_ref, *, nb, kk):
    hp = pl.program_id(1)
    own = pl.program_id(2)
    blk = tq = MOBA_BLOCK
    cols = HEADS_PER_TILE * tq

    @pl.when(own == 0)
    def _():
        kmean_ref[...] = jnp.zeros_like(kmean_ref)
        for n in range(nb):
            rows = slice(n * blk, (n + 1) * blk)
            kmean_ref[n:n + 1, :] = jnp.mean(k_ref[0, rows, :], axis=0, keepdims=True)
            vt_ref[:, rows] = v_ref[0, rows, :].T.astype(vt_ref.dtype)
        kb_ref[...] = k_ref[0].astype(kb_ref.dtype)

    q0 = own * tq
    lo = _iota((tq, LANES), 1) < HEAD_DIM
    q = q_ref[0]
    qs = jnp.concatenate([jnp.where(lo, q, 0.0), jnp.where(lo, 0.0, q)], axis=0)
    col1 = _iota((1, cols), 1)
    slope = jnp.where(col1 < tq, slopes_ref[HEADS_PER_TILE * hp], slopes_ref[HEADS_PER_TILE * hp + 1])

    gate = _dot_nt(kmean_ref[...], qs, precision=HI)
    n_idx = _iota(gate.shape, 0)
    cand = n_idx < own
    gm = jnp.where(cand, gate, NEG)
    sel = cand & (_rank_before(gm, nb - 1, n_idx) < kk)
    row_ref[...] = slope * (n_idx * blk - q0).astype(F32) + jnp.where(sel, 0.0, NEG)

    qsb = (qs * SCALE).astype(MXU_DTYPE)
    rel = _iota((blk, cols), 0) - (_iota((blk, cols), 1) & (tq - 1))
    bias = slope * rel.astype(F32)

    def attend(c):
        nkeys = (c + 1) * blk
        s = _dot_nt(kb_ref[0:nkeys, :], qsb)
        pieces = [s[n * blk:(n + 1) * blk] + bias + row_ref[n:n + 1, :] for n in range(c)]
        pieces.append(jnp.where(rel <= 0, s[c * blk:nkeys] + bias, NEG))
        s = jnp.concatenate(pieces, axis=0)
        p = jnp.exp(s - jnp.max(s, axis=0, keepdims=True))
        inv = 1.0 / jnp.sum(p, axis=0, keepdims=True)
        pb = p.astype(MXU_DTYPE)
        halves = []
        for h in range(HEADS_PER_TILE):
            dims = slice(h * HEAD_DIM, (h + 1) * HEAD_DIM)
            qcols = slice(h * tq, (h + 1) * tq)
            halves.append(_dot(vt_ref[dims, 0:nkeys], pb[:, qcols]) * inv[:, qcols])
        o_ref[0] = jnp.concatenate(halves, axis=0).T

    for c in range(nb):
        @pl.when(own == c)
        def _(c=c):
            attend(c)


def _moba(slopes, proj):
    B, S, _ = proj.shape
    tq = MOBA_BLOCK
    nb = S // MOBA_BLOCK
    kk = max(1, min(MOBA_TOPK, nb - 1))
    n_tiles = MOBA_HEADS // HEADS_PER_TILE
    kern = functools.partial(_moba_kernel, nb=nb, kk=kk)
    return pl.pallas_call(
        kern,
        grid=(B, n_tiles, S // tq),
        in_specs=[
            pl.BlockSpec(memory_space=pltpu.SMEM),
            pl.BlockSpec((1, tq, LANES), lambda b, h, i: (b, i, COL_QM + h)),
            pl.BlockSpec((1, S, LANES), lambda b, h, i: (b, 0, COL_KM + h)),
            pl.BlockSpec((1, S, LANES), lambda b, h, i: (b, 0, COL_VM + h)),
        ],
        out_specs=pl.BlockSpec((1, tq, LANES), lambda b, h, i: (b, i, h)),
        out_shape=jax.ShapeDtypeStruct((B, S, MOBA_HEADS * HEAD_DIM), F32),
        scratch_shapes=[
            pltpu.VMEM((_round_up(nb, SUBLANES), LANES), F32),
            pltpu.VMEM((S, LANES), MXU_DTYPE),
            pltpu.VMEM((LANES, S), MXU_DTYPE),
            pltpu.VMEM((_round_up(nb, SUBLANES), HEADS_PER_TILE * tq), F32),
        ],
        compiler_params=_params("parallel", "parallel", "arbitrary"),
        name="moba",
    )(slopes, proj, proj, proj)


def _gelu_tanh(x):
    return x * (0.5 * (1.0 + jnp.tanh(0.7978845608028654 * (x + 0.044715 * (x * x * x)))))


def _compress_kernel(xk_ref, xv_ref, pk_ref, pv_ref, w1k_ref, w1v_ref, w2k_ref, w2v_ref, kc_ref, vc_ref):
    g = pl.program_id(0)
    half = (CMP_LEN // 2) * HEAD_DIM
    for x_ref, p_ref, w1_ref, w2_ref, o_ref in ((xk_ref, pk_ref, w1k_ref, w2k_ref, kc_ref),
                                               (xv_ref, pv_ref, w1v_ref, w2v_ref, vc_ref)):
        B, nr, W = x_ref.shape
        x = x_ref[...].reshape(B * nr, W)
        top = _dot_hi(x + p_ref[:, 0:half], w1_ref[0:half, :])
        bot = _dot_hi(x + p_ref[:, half:2 * half], w1_ref[half:2 * half, :])
        hid = top + pltpu.roll(bot, B * nr - 1, 0)
        out = _dot_hi(_gelu_tanh(hid), w2_ref[...]).reshape(B, nr, LANES)

        @pl.when(g == 0)
        def _(o_ref=o_ref, out=out):
            o_ref[...] = out

        @pl.when(g > 0)
        def _(o_ref=o_ref, out=out):
            o_ref[...] += out


def _compress(xk, xv, pk, pv, w1k, w1v, w2k, w2v, l):
    B, G, nr, W = xk.shape
    H = w1k.shape[2]
    layer = lambda *shape: pl.BlockSpec((None,) + shape, lambda g: (l,) + (0,) * len(shape))
    xspec = pl.BlockSpec((B, None, nr, W), lambda g: (0, g, 0, 0))
    w2spec = pl.BlockSpec((None, H, LANES), lambda g: (g, 0, 0))
    ospec = pl.BlockSpec((B, nr, LANES), lambda g: (0, 0, 0))
    oshape = jax.ShapeDtypeStruct((B, nr, LANES), F32)
    return pl.pallas_call(
        _compress_kernel,
        grid=(G,),
        in_specs=[xspec, xspec, layer(1, 2 * W), layer(1, 2 * W), layer(2 * W, H), layer(2 * W, H),
                  w2spec, w2spec],
        out_specs=(ospec, ospec),
        out_shape=(oshape, oshape),
        compiler_params=_params("arbitrary"),
        name="nsa_compress",
    )(xk, xv, pk, pv, w1k, w1v, w2k, w2v)


def _nsa_kernel(slopes_ref, qa_ref, qb_ref, g_ref, kc_ref, vc_ref, ks_ref, vs_ref, kw_ref, vw_ref,
                o_ref, kcx, vct, ksx, vst, kwx, vwt, qx_ref, gt_ref, ocmp_ref, owin_ref,
                *, S, tq, n_cmp, n_slc, k_top):
    g = pl.program_id(1)
    qi = pl.program_id(2)
    q0 = qi * tq
    cols = NSA_GROUP * tq
    sblk = SLC_CHUNK
    nr = kcx.shape[0]
    nj = _round_up(n_slc, 2 * SUBLANES)
    n_back = WINDOW // tq
    ext0 = HEAD_DIM

    @pl.when(qi == 0)
    def _():
        def both_halves(x):
            mine = (_iota(x.shape, 1) >= HEAD_DIM) == (g == 1)
            return jnp.where(mine, x, pltpu.roll(x, HEAD_DIM, 1))

        def with_ext(x, pos, mask_block=0):
            lane = _iota(x.shape, 1)
            return jnp.where(lane < HEAD_DIM, both_halves(x), _key_ext(pos, lane - ext0, mask_block)).astype(MXU_DTYPE)

        def transposed(x):
            return both_halves(x).T[0:HEAD_DIM].astype(MXU_DTYPE)

        qx_ref[...] = jnp.zeros_like(qx_ref)
        kcx[...] = with_ext(kc_ref[0], _iota((nr, LANES), 0) * CMP_STRIDE + (CMP_LEN - 1))
        vct[...] = transposed(vc_ref[0])
        lane = _iota((WINDOW, LANES), 1)
        kwx[0:WINDOW, :] = _key_ext(lane * 0, lane - ext0, pad=lane >= 0).astype(MXU_DTYPE)
        for c in range(n_back):
            vwt[c] = jnp.zeros(vwt.shape[1:], vwt.dtype)
        for c in range(S // sblk):
            rows = slice(c * sblk, (c + 1) * sblk)
            pos = c * sblk + _iota((sblk, LANES), 0)
            ksx[rows, :] = with_ext(ks_ref[0, rows, :], pos, SLC_BLOCK)
            kwx[WINDOW + c * sblk:WINDOW + (c + 1) * sblk, :] = with_ext(kw_ref[0, rows, :], pos)
            vst[:, rows] = transposed(vs_ref[0, rows, :])
        for c in range(S // tq):
            vwt[n_back + c] = transposed(vw_ref[0, c * tq:(c + 1) * tq, :])

    qat = qa_ref[0].T
    qbt = qb_ref[0].T
    qt = jnp.concatenate([qat[0:HEAD_DIM], qat[HEAD_DIM:LANES], qbt[0:HEAD_DIM], qbt[HEAD_DIM:LANES]], axis=1)
    col1 = _iota((1, cols), 1)
    h0 = NSA_GROUP * g
    slope = jnp.where(col1 < tq, slopes_ref[h0],
                      jnp.where(col1 < 2 * tq, slopes_ref[h0 + 1],
                                jnp.where(col1 < 3 * tq, slopes_ref[h0 + 2], slopes_ref[h0 + 3])))
    i1 = col1 & (tq - 1)
    t1 = q0 + i1
    qx_ref[0:HEAD_DIM, :] = (qt * SCALE).astype(qx_ref.dtype)
    qx_ref[ext0:ext0 + EXT_MASK_LANE, :] = jnp.concatenate(
        [_query_ext(slope, t1), jnp.zeros((EXT_MASK_LANE - SUBLANES, cols), F32)], axis=0).astype(qx_ref.dtype)
    qx = qx_ref[...]

    relw = _iota((tq, cols), 0) - i1
    s = jnp.dot(kwx[pl.ds(pl.multiple_of(q0, tq), WINDOW + tq), :], qx, preferred_element_type=F32)
    s = jnp.concatenate([jnp.where(relw > 0, s[0:tq], NEG), s[tq:WINDOW],
                         jnp.where(relw <= 0, s[WINDOW:WINDOW + tq], NEG)], axis=0)
    p, inv = _softmax_cols(s)
    pb = p.astype(MXU_DTYPE)
    acc = _dot(vwt[qi], pb[0:tq])
    for e in range(1, n_back + 1):
        acc = acc + _dot(vwt[qi + e], pb[e * tq:(e + 1) * tq])
    owin_ref[...] = acc * inv

    nidx = _iota((nr, cols), 0)
    maskc = (nidx * CMP_STRIDE + (CMP_LEN - 1) <= t1) & (nidx < n_cmp)
    s = jnp.where(maskc, jnp.dot(kcx[...], qx, preferred_element_type=F32), NEG)
    p = jnp.where(maskc, jnp.exp(s - jnp.max(s, axis=0, keepdims=True)), 0.0)
    p = p / jnp.maximum(jnp.sum(p, axis=0, keepdims=True), 1e-30)
    ocmp_ref[...] = _dot(vct[...], p)

    psum = p[:, 0:tq] + p[:, tq:2 * tq] + p[:, 2 * tq:3 * tq] + p[:, 3 * tq:4 * tq]
    jj = _iota((nj, nr), 0) * SLC_BLOCK
    nn = _iota((nj, nr), 1) * CMP_STRIDE
    overlap = ((nn < jj + SLC_BLOCK) & (nn + CMP_LEN > jj)
               & (nn < n_cmp * CMP_STRIDE) & (jj < n_slc * SLC_BLOCK)).astype(F32)
    imp = _dot_hi(overlap, psum)
    jl = _iota((nj, tq), 0)
    tb = _div_pow2(q0 + _iota((nj, tq), 1), SLC_BLOCK)
    cand = jl <= tb
    forced = (jl == 0) | (jl == tb) | (jl == tb - 1)
    score = jnp.where(cand, jnp.where(forced, SLC_FORCE, imp), NEG)
    keep = cand & (_rank_before(score, n_slc, jl) < k_top)
    mask0 = ext0 + EXT_MASK_LANE
    qx_ref[mask0:mask0 + nj, :] = jnp.concatenate([jnp.where(keep, 0.0, NEG)] * NSA_GROUP, axis=1).astype(qx_ref.dtype)

    z = g_ref[0].T
    gt_ref[...] = 1.0 / (1.0 + jnp.exp(-z))

    def combine(oslc):
        ocmp = ocmp_ref[...]
        owin = owin_ref[...]
        heads = []
        for r in range(NSA_GROUP):
            sl = slice(r * tq, (r + 1) * tq)
            c = NSA_BRANCHES * (NSA_GROUP * g + r)
            heads.append(gt_ref[pl.ds(c, 1), :] * ocmp[:, sl] + gt_ref[pl.ds(c + 1, 1), :] * oslc[:, sl]
                         + gt_ref[pl.ds(c + 2, 1), :] * owin[:, sl])
        o_ref[0, :, 0:LANES] = jnp.concatenate(heads[0:2], axis=0).T
        o_ref[0, :, LANES:2 * LANES] = jnp.concatenate(heads[2:4], axis=0).T

    def slc_attend(c):
        nkeys = (c + 1) * sblk
        s = jnp.dot(ksx[0:nkeys, :], qx_ref[...], preferred_element_type=F32)
        causal = _iota((sblk, cols), 0) + (c * sblk - q0) <= i1
        diag = jnp.where(causal, s[c * sblk:nkeys], NEG)
        s = jnp.concatenate([s[0:c * sblk], diag], axis=0) if c else diag
        p, inv = _softmax_cols(s)
        combine(_dot(vst[:, 0:nkeys], p) * inv)

    dc = _div_pow2(q0, sblk)
    for c in range(S // sblk):
        @pl.when(dc == c)
        def _(c=c):
            slc_attend(c)


def _nsa(slopes, proj, kc, vc):
    B, S, _ = proj.shape
    tq = NSA_QTILE
    n_cmp = (S - CMP_LEN) // CMP_STRIDE + 1
    n_slc = S // SLC_BLOCK
    k_top = min(SLC_TOPK, n_slc)
    nr = kc.shape[1]
    cols = NSA_GROUP * tq
    assert HEAD_DIM + EXT_MASK_LANE + _round_up(n_slc, 2 * SUBLANES) <= LANES and S // POS_SPLIT <= POS_SPLIT
    assert SLC_CHUNK % tq == 0 and WINDOW % tq == 0 and S % SLC_CHUNK == 0
    kern = functools.partial(_nsa_kernel, S=S, tq=tq, n_cmp=n_cmp, n_slc=n_slc, k_top=k_top)
    qtiles = NSA_GROUP // HEADS_PER_TILE

    def seq(col):
        return pl.BlockSpec((1, S, LANES), lambda b, g, i: (b, 0, col))

    small = pl.BlockSpec((1, nr, LANES), lambda b, g, i: (b, 0, 0))
    return pl.pallas_call(
        kern,
        grid=(B, NSA_KV_HEADS, S // tq),
        in_specs=[
            pl.BlockSpec(memory_space=pltpu.SMEM),
            pl.BlockSpec((1, tq, LANES), lambda b, g, i: (b, i, COL_QN + qtiles * g)),
            pl.BlockSpec((1, tq, LANES), lambda b, g, i: (b, i, COL_QN + qtiles * g + 1)),
            pl.BlockSpec((1, tq, LANES), lambda b, g, i: (b, i, COL_G)),
            small, small, seq(COL_KS), seq(COL_VS), seq(COL_KW), seq(COL_VW),
        ],
        out_specs=pl.BlockSpec((1, tq, qtiles * LANES), lambda b, g, i: (b, i, g)),
        out_shape=jax.ShapeDtypeStruct((B, S, NSA_HEADS * HEAD_DIM), F32),
        scratch_shapes=[
            pltpu.VMEM((nr, LANES), MXU_DTYPE), pltpu.VMEM((HEAD_DIM, nr), MXU_DTYPE),
            pltpu.VMEM((S, LANES), MXU_DTYPE), pltpu.VMEM((HEAD_DIM, S), MXU_DTYPE),
            pltpu.VMEM((WINDOW + S, LANES), MXU_DTYPE),
            pltpu.VMEM((WINDOW // tq + S // tq, HEAD_DIM, tq), MXU_DTYPE),
            pltpu.VMEM((LANES, cols), MXU_DTYPE),
            pltpu.VMEM((LANES, tq), F32),
            pltpu.VMEM((HEAD_DIM, cols), F32), pltpu.VMEM((HEAD_DIM, cols), F32),
        ],
        compiler_params=_params("parallel", "parallel", "arbitrary"),
        name="nsa",
    )(slopes, proj, proj, proj, kc, vc, proj, proj, proj, proj)


def _outproj_kernel(om_ref, on_ref, x_ref, gm_ref, gn_ref, wm_ref, wn_ref, o_ref):
    y = _dot(_rms(om_ref[...], gm_ref[...]), wm_ref[...]) + _dot(_rms(on_ref[...], gn_ref[...]), wn_ref[...])
    o_ref[...] = x_ref[...] + y


def _outproj(om, on, x, gm, gn, w, l, tm):
    T, D = x.shape
    Wm, Wn = om.shape[1], on.shape[1]
    assert Wm == Wn and w.shape[1] == Wm + Wn
    return pl.pallas_call(
        _outproj_kernel,
        grid=(T // tm,),
        in_specs=[
            pl.BlockSpec((tm, Wm), lambda i: (i, 0)),
            pl.BlockSpec((tm, Wn), lambda i: (i, 0)),
            pl.BlockSpec((tm, D), lambda i: (i, 0)),
            pl.BlockSpec((None, 1, Wm), lambda i: (l, 0, 0)),
            pl.BlockSpec((None, 1, Wn), lambda i: (l, 0, 0)),
            pl.BlockSpec((None, Wm, D), lambda i: (l, 0, 0)),
            pl.BlockSpec((None, Wn, D), lambda i: (l, 1, 0)),
        ],
        out_specs=pl.BlockSpec((tm, D), lambda i: (i, 0)),
        out_shape=jax.ShapeDtypeStruct((T, D), F32),
        compiler_params=_params("parallel"),
        name="outproj",
    )(om, on, x, gm, gn, w, w)


def _alibi_slopes(n):
    slopes = 2.0 ** (-8.0 * np.arange(1, n + 1) / n)
    assert np.all(np.log2(slopes) == np.round(np.log2(slopes)))
    return jnp.asarray(slopes, dtype=F32)


def _token_tile(T, want):
    return want if T % want == 0 else T


def _mixer(x, B, S, l, mix_norm, w_in_p, pos_k, k_w1, k_w2, pos_v, v_w1, v_w2, moba_norm, nsa_norm, w_out):
    T, D = x.shape
    proj = _inproj(x, mix_norm, w_in_p, l, _token_tile(T, 512)).reshape(B, S, IN_PAD)

    o_m = _moba(_alibi_slopes(MOBA_HEADS), proj)

    nr = S // CMP_STRIDE

    def windows(col):
        z = proj[:, :, col * LANES:(col + 1) * LANES].reshape(B, S, NSA_KV_HEADS, HEAD_DIM)
        return z.transpose(0, 2, 1, 3).reshape(B, NSA_KV_HEADS, nr, CMP_STRIDE * HEAD_DIM)

    def placed(w2):
        return jnp.stack([jnp.pad(w2, ((0, 0), (g * HEAD_DIM, LANES - (g + 1) * HEAD_DIM)))
                          for g in range(NSA_KV_HEADS)])

    kc, vc = _compress(windows(COL_KC), windows(COL_VC), pos_k, pos_v, k_w1, v_w1, placed(k_w2), placed(v_w2), l)
    o_n = _nsa(_alibi_slopes(NSA_HEADS), proj, kc, vc)

    return _outproj(o_m.reshape(T, -1), o_n.reshape(T, -1), x, moba_norm, nsa_norm, w_out, l,
                    _token_tile(T, 512))


def _swiglu_step(x, l, norm, w_gate, w_up, w_down, final_g, final=False):
    T = x.shape[0]
    return _ffn(x, norm, w_gate, w_up, w_down, l, _token_tile(T, 1024), 256, final_g, final)


def _rows(p):
    return p.reshape(p.shape[0], 1, -1)


@jax.jit
def kernel(x, ffa_norm, ffa_w_gate, ffa_w_up, ffa_w_down, mix_norm, w_in, cmp_pos_k, cmp_k_w1, cmp_k_w2,
           cmp_pos_v, cmp_v_w1, cmp_v_w2, moba_out_norm, nsa_out_norm, w_out, ffb_norm, ffb_w_gate,
           ffb_w_up, ffb_w_down, final_norm):
    B, S, D = x.shape
    assert S % MOBA_BLOCK == 0 and S >= WINDOW and w_in.shape[-1] == IN_WIDTH
    h = x.reshape(B * S, D)
    w_in_p = jnp.pad(w_in, ((0, 0), (0, 0), (0, IN_PAD - IN_WIDTH)))
    ffa_norm, ffb_norm, mix_norm = _rows(ffa_norm), _rows(ffb_norm), _rows(mix_norm)
    moba_out_norm, nsa_out_norm = _rows(moba_out_norm), _rows(nsa_out_norm)
    cmp_pos_k, cmp_pos_v = _rows(cmp_pos_k), _rows(cmp_pos_v)
    depth = ffa_norm.shape[0]
    final_g = final_norm[None]
    for l in range(depth):
        h = _swiglu_step(h, l, ffa_norm, ffa_w_gate, ffa_w_up, ffa_w_down, final_g)
        h = _mixer(h, B, S, l, mix_norm, w_in_p, cmp_pos_k, cmp_k_w1, cmp_k_w2[l],
                   cmp_pos_v, cmp_v_w1, cmp_v_w2[l], moba_out_norm, nsa_out_norm, w_out)
        h = _swiglu_step(h, l, ffb_norm, ffb_w_gate, ffb_w_up, ffb_w_down, final_g, final=l == depth - 1)
    return h.reshape(B, S, D)
```

```python
import functools

import numpy as np
import jax
import jax.numpy as jnp
from jax import lax
from jax.experimental import pallas as pl
from jax.experimental.pallas import tpu as pltpu

HEAD_DIM = 64
MOBA_HEADS = 8
NSA_HEADS = 8
NSA_KV_HEADS = 2
NSA_GROUP = NSA_HEADS // NSA_KV_HEADS
NSA_BRANCHES = 3
MOBA_BLOCK = 256
MOBA_TOPK = 3
CMP_LEN = 32
CMP_STRIDE = 16
SLC_BLOCK = 64
SLC_TOPK = 16
WINDOW = 512
NEG = -1e30
SLC_FORCE = 1e4
EPS = 1e-6
SCALE = HEAD_DIM ** -0.5

LANES = 128
SUBLANES = 8
HEADS_PER_TILE = LANES // HEAD_DIM
COL_QM, COL_KM, COL_VM, COL_QN = 0, 4, 8, 12
COL_KC, COL_VC, COL_KS, COL_VS, COL_KW, COL_VW, COL_G = 16, 17, 18, 19, 20, 21, 22
IN_TILES = 24
IN_PAD = IN_TILES * LANES
IN_WIDTH = 2840
NSA_QTILE = 256
MOBA_TILE_BLOCKS = 2
SLC_CHUNK = 4 * SLC_BLOCK

MXU_DTYPE = jnp.bfloat16
VMEM_LIMIT = 48 * 1024 * 1024
F32 = jnp.float32
HI = lax.Precision.HIGHEST


def _dot(a, b):
    return jnp.dot(a.astype(MXU_DTYPE), b.astype(MXU_DTYPE), preferred_element_type=F32)


def _dot_nt(a, b, precision=None):
    return lax.dot_general(a, b, (((1,), (1,)), ((), ())), precision=precision,
                           preferred_element_type=F32)


def _dot_hi(a, b):
    return jnp.dot(a, b, precision=HI, preferred_element_type=F32)


def _iota(shape, dim):
    return lax.broadcasted_iota(jnp.int32, shape, dim)


def _div_pow2(x, n):
    assert n & (n - 1) == 0
    return x >> (n.bit_length() - 1)


def _round_up(n, m):
    return -(-n // m) * m


def _params(*sem):
    return pltpu.CompilerParams(dimension_semantics=sem, vmem_limit_bytes=VMEM_LIMIT)


def _rms(x, g):
    return x * lax.rsqrt(jnp.mean(x * x, axis=-1, keepdims=True) + EPS) * g


def _ffn_kernel(x_ref, g_ref, wg_ref, wu_ref, wd_ref, gf_ref, o_ref, h_ref, wgb, wub, wdb, *, nj, final):
    step = pl.program_id(0)

    def start():
        h_ref[...] = _rms(x_ref[...], g_ref[...]).astype(h_ref.dtype)

    def chunk(wg, wu, wd):
        h = h_ref[...]
        a = _dot(h, wg)
        b = _dot(h, wu)
        return _dot(a / (1.0 + jnp.exp(-a)) * b, wd)

    def finish():
        y = x_ref[...] + 0.5 * o_ref[...]
        o_ref[...] = _rms(y, gf_ref[...]) if final else y

    @pl.when(step < nj)
    def _():
        wgb[step] = wg_ref[...].astype(wgb.dtype)
        wub[step] = wu_ref[...].astype(wub.dtype)
        wdb[step] = wd_ref[...].astype(wdb.dtype)

        @pl.when(step == 0)
        def _():
            start()
            o_ref[...] = chunk(wgb[0], wub[0], wdb[0])

        @pl.when(step > 0)
        def _():
            o_ref[...] += chunk(wgb[step], wub[step], wdb[step])

        @pl.when(step == nj - 1)
        def _():
            finish()

    @pl.when(step >= nj)
    def _():
        start()
        o_ref[...] = chunk(wgb[0], wub[0], wdb[0])

        def body(j, carry):
            o_ref[...] += chunk(wgb[j], wub[j], wdb[j])
            return carry

        lax.fori_loop(1, nj, body, 0, unroll=2)
        finish()


def _ffn(x, g, wg, wu, wd, l, tm, tf, final_g, final):
    T, D = x.shape
    F = wg.shape[2]
    nj = F // tf
    tile = lambda s: (jnp.maximum(s - (nj - 1), 0), 0)
    col = lambda s: jnp.minimum(s, nj - 1)
    return pl.pallas_call(
        functools.partial(_ffn_kernel, nj=nj, final=final),
        grid=(nj + T // tm - 1,),
        in_specs=[
            pl.BlockSpec((tm, D), tile),
            pl.BlockSpec((None, 1, D), lambda s: (l, 0, 0)),
            pl.BlockSpec((None, D, tf), lambda s: (l, 0, col(s))),
            pl.BlockSpec((None, D, tf), lambda s: (l, 0, col(s))),
            pl.BlockSpec((None, tf, D), lambda s: (l, col(s), 0)),
            pl.BlockSpec((1, D), lambda s: (0, 0)),
        ],
        out_specs=pl.BlockSpec((tm, D), tile),
        out_shape=jax.ShapeDtypeStruct((T, D), F32),
        scratch_shapes=[pltpu.VMEM((tm, D), MXU_DTYPE), pltpu.VMEM((nj, D, tf), MXU_DTYPE),
                        pltpu.VMEM((nj, D, tf), MXU_DTYPE), pltpu.VMEM((nj, tf, D), MXU_DTYPE)],
        compiler_params=_params("arbitrary"),
        name="ffn",
    )(x, g, wg, wu, wd, final_g)


def _inproj_kernel(x_ref, g_ref, w_ref, o_ref, ck_ref, cv_ref, wb_ref):
    @pl.when(pl.program_id(0) == 0)
    def _():
        wb_ref[...] = w_ref[...].astype(wb_ref.dtype)

    y = _dot(_rms(x_ref[...], g_ref[...]), wb_ref[...])
    o_ref[...] = y
    for grp in range(NSA_KV_HEADS):
        lanes = slice(grp * HEAD_DIM, (grp + 1) * HEAD_DIM)
        ck_ref[grp] = y[:, COL_KC * LANES:(COL_KC + 1) * LANES][:, lanes]
        cv_ref[grp] = y[:, COL_VC * LANES:(COL_VC + 1) * LANES][:, lanes]


def _inproj(x, g, w, l, tm):
    T, D = x.shape
    N = w.shape[2]
    slab = pl.BlockSpec((NSA_KV_HEADS, tm, HEAD_DIM), lambda i: (0, i, 0))
    slab_shape = jax.ShapeDtypeStruct((NSA_KV_HEADS, T, HEAD_DIM), F32)
    return pl.pallas_call(
        _inproj_kernel,
        grid=(T // tm,),
        in_specs=[
            pl.BlockSpec((tm, D), lambda i: (i, 0)),
            pl.BlockSpec((None, 1, D), lambda i: (l, 0, 0)),
            pl.BlockSpec((None, D, N), lambda i: (l, 0, 0), pipeline_mode=pl.Buffered(1)),
        ],
        out_specs=(pl.BlockSpec((tm, N), lambda i: (i, 0)), slab, slab),
        out_shape=(jax.ShapeDtypeStruct((T, N), F32), slab_shape, slab_shape),
        scratch_shapes=[pltpu.VMEM((D, N), MXU_DTYPE)],
        compiler_params=_params("arbitrary"),
        name="inproj",
    )(x, g, w)


POS_SPLIT = 256
EXT_PAD_LANE = 4
EXT_MASK_LANE = 2 * SUBLANES


def _key_ext(pos, lane, mask_block=0, pad=None):
    ext = jnp.where(lane == 0, _div_pow2(pos, POS_SPLIT),
                    jnp.where(lane == 1, pos & (POS_SPLIT - 1),
                              jnp.where((lane == 2) | (lane == 3), 1, 0)))
    if mask_block:
        ext = jnp.where(lane - EXT_MASK_LANE == _div_pow2(pos, mask_block), 1, ext)
    if pad is not None:
        ext = jnp.where(pad, jnp.where(lane == EXT_PAD_LANE, 1, 0), ext)
    return ext.astype(F32)


def _query_ext(slope, t):
    r = _iota((SUBLANES, t.shape[1]), 0)
    big = slope * float(POS_SPLIT)
    hi = _div_pow2(t, POS_SPLIT).astype(F32)
    lo = (t & (POS_SPLIT - 1)).astype(F32)
    return jnp.where(r == 0, big, jnp.where(r == 1, slope, jnp.where(r == 2, -(big * hi),
                     jnp.where(r == 3, -(slope * lo), jnp.where(r == EXT_PAD_LANE, NEG, 0.0)))))


def _softmax_cols(s):
    p = jnp.exp(s - jnp.max(s, axis=0, keepdims=True))
    return p, 1.0 / jnp.sum(p, axis=0, keepdims=True)


def _rank_before(score, n, idx):
    rank = jnp.zeros(score.shape, jnp.int32)
    for m in range(n):
        row = score[m:m + 1, :]
        better = (row > score) | ((row == score) & (m < idx))
        rank = rank + better.astype(jnp.int32)
    return rank


def _moba_kernel(slopes_ref, q_ref, k_ref, v_ref, o_ref,
                 kmean_ref, kb_ref, vt_ref, row_ref, *, nb, kk, tq):
    hp = pl.program_id(1)
    ti = pl.program_id(2)
    blk = MOBA_BLOCK
    per = tq // blk
    cols = HEADS_PER_TILE * tq

    @pl.when(ti == 0)
    def _():
        kmean_ref[...] = jnp.zeros_like(kmean_ref)
        for n in range(nb):
            rows = slice(n * blk, (n + 1) * blk)
            kmean_ref[n:n + 1, :] = jnp.mean(k_ref[0, rows, :], axis=0, keepdims=True)
            vt_ref[:, rows] = v_ref[0, rows, :].T.astype(vt_ref.dtype)
        kb_ref[...] = k_ref[0].astype(kb_ref.dtype)

    q0 = ti * tq
    lo = _iota((tq, LANES), 1) < HEAD_DIM
    q = q_ref[0]
    qs = jnp.concatenate([jnp.where(lo, q, 0.0), jnp.where(lo, 0.0, q)], axis=0)
    col1 = _iota((1, cols), 1)
    slope = jnp.where(col1 < tq, slopes_ref[HEADS_PER_TILE * hp], slopes_ref[HEADS_PER_TILE * hp + 1])
    i1 = col1 & (tq - 1)
    own = ti * per + _div_pow2(i1, blk)

    gate = _dot_nt(kmean_ref[...], qs, precision=HI)
    n_idx = _iota(gate.shape, 0)
    cand = n_idx < own
    gm = jnp.where(cand, gate, NEG)
    keep = (cand & (_rank_before(gm, nb - 1, n_idx) < kk)) | (n_idx == own)
    row_ref[...] = slope * (n_idx * blk - q0).astype(F32) + jnp.where(keep, 0.0, NEG)

    qsb = (qs * SCALE).astype(MXU_DTYPE)
    rel = _iota((blk, cols), 0) - i1
    bias = slope * rel.astype(F32)

    def attend(t):
        first = per * t
        nkeys = (first + per) * blk
        s = _dot_nt(kb_ref[0:nkeys, :], qsb)
        pieces = []
        for n in range(first + per):
            piece = s[n * blk:(n + 1) * blk] + bias + row_ref[n:n + 1, :]
            if n >= first:
                piece = jnp.where(rel + (n - first) * blk <= 0, piece, NEG)
            pieces.append(piece)
        s = jnp.concatenate(pieces, axis=0)
        p = jnp.exp(s - jnp.max(s, axis=0, keepdims=True))
        inv = 1.0 / jnp.sum(p, axis=0, keepdims=True)
        pb = p.astype(MXU_DTYPE)
        halves = []
        for h in range(HEADS_PER_TILE):
            dims = slice(h * HEAD_DIM, (h + 1) * HEAD_DIM)
            qcols = slice(h * tq, (h + 1) * tq)
            halves.append(_dot(vt_ref[dims, 0:nkeys], pb[:, qcols]) * inv[:, qcols])
        o_ref[0] = jnp.concatenate(halves, axis=0).T

    for t in range(nb // per):
        @pl.when(ti == t)
        def _(t=t):
            attend(t)


def _moba(slopes, proj):
    B, S, _ = proj.shape
    tq = MOBA_TILE_BLOCKS * MOBA_BLOCK
    nb = S // MOBA_BLOCK
    assert S % tq == 0
    kk = max(1, min(MOBA_TOPK, nb - 1))
    n_tiles = MOBA_HEADS // HEADS_PER_TILE
    kern = functools.partial(_moba_kernel, nb=nb, kk=kk, tq=tq)
    return pl.pallas_call(
        kern,
        grid=(B, n_tiles, S // tq),
        in_specs=[
            pl.BlockSpec(memory_space=pltpu.SMEM),
            pl.BlockSpec((1, tq, LANES), lambda b, h, i: (b, i, COL_QM + h)),
            pl.BlockSpec((1, S, LANES), lambda b, h, i: (b, 0, COL_KM + h)),
            pl.BlockSpec((1, S, LANES), lambda b, h, i: (b, 0, COL_VM + h)),
        ],
        out_specs=pl.BlockSpec((1, tq, LANES), lambda b, h, i: (b, i, h)),
        out_shape=jax.ShapeDtypeStruct((B, S, MOBA_HEADS * HEAD_DIM), F32),
        scratch_shapes=[
            pltpu.VMEM((_round_up(nb, SUBLANES), LANES), F32),
            pltpu.VMEM((S, LANES), MXU_DTYPE),
            pltpu.VMEM((LANES, S), MXU_DTYPE),
            pltpu.VMEM((_round_up(nb, SUBLANES), HEADS_PER_TILE * tq), F32),
        ],
        compiler_params=_params("parallel", "parallel", "arbitrary"),
        name="moba",
    )(slopes, proj, proj, proj)


def _gelu_tanh(x):
    return x * (0.5 * (1.0 + jnp.tanh(0.7978845608028654 * (x + 0.044715 * (x * x * x)))))


def _compress_kernel(xk_ref, xv_ref, pk_ref, pv_ref, w1k_ref, w1v_ref, w2k_ref, w2v_ref, kc_ref, vc_ref):
    g = pl.program_id(0)
    half = (CMP_LEN // 2) * HEAD_DIM
    for x_ref, p_ref, w1_ref, w2_ref, o_ref in ((xk_ref, pk_ref, w1k_ref, w2k_ref, kc_ref),
                                               (xv_ref, pv_ref, w1v_ref, w2v_ref, vc_ref)):
        B, nr, W = x_ref.shape
        x = x_ref[...].reshape(B * nr, W)
        top = _dot_hi(x + p_ref[:, 0:half], w1_ref[0:half, :])
        bot = _dot_hi(x + p_ref[:, half:2 * half], w1_ref[half:2 * half, :])
        hid = top + pltpu.roll(bot, B * nr - 1, 0)
        out = _dot_hi(_gelu_tanh(hid), w2_ref[...]).reshape(B, nr, LANES)

        @pl.when(g == 0)
        def _(o_ref=o_ref, out=out):
            o_ref[...] = out

        @pl.when(g > 0)
        def _(o_ref=o_ref, out=out):
            o_ref[...] += out


def _compress(xk, xv, pk, pv, w1k, w1v, w2k, w2v, l):
    G, B, nr, W = xk.shape
    H = w1k.shape[2]
    layer = lambda *shape: pl.BlockSpec((None,) + shape, lambda g: (l,) + (0,) * len(shape))
    xspec = pl.BlockSpec((None, B, nr, W), lambda g: (g, 0, 0, 0))
    w2spec = pl.BlockSpec((None, H, LANES), lambda g: (g, 0, 0))
    ospec = pl.BlockSpec((B, nr, LANES), lambda g: (0, 0, 0))
    oshape = jax.ShapeDtypeStruct((B, nr, LANES), F32)
    return pl.pallas_call(
        _compress_kernel,
        grid=(G,),
        in_specs=[xspec, xspec, layer(1, 2 * W), layer(1, 2 * W), layer(2 * W, H), layer(2 * W, H),
                  w2spec, w2spec],
        out_specs=(ospec, ospec),
        out_shape=(oshape, oshape),
        compiler_params=_params("arbitrary"),
        name="nsa_compress",
    )(xk, xv, pk, pv, w1k, w1v, w2k, w2v)


def _nsa_kernel(slopes_ref, qa_ref, qb_ref, g_ref, kc_ref, vc_ref, ks_ref, vs_ref, kw_ref, vw_ref,
                o_ref, kcx, vct, ksx, vst, kwx, vwt, qx_ref, gt_ref, ocmp_ref, owin_ref,
                *, S, tq, n_cmp, n_slc, k_top):
    g = pl.program_id(1)
    qi = pl.program_id(2)
    q0 = qi * tq
    cols = NSA_GROUP * tq
    sblk = SLC_CHUNK
    nr = kcx.shape[0]
    nj = _round_up(n_slc, 2 * SUBLANES)
    n_back = WINDOW // tq
    ext0 = HEAD_DIM

    @pl.when(qi == 0)
    def _():
        def both_halves(x):
            mine = (_iota(x.shape, 1) >= HEAD_DIM) == (g == 1)
            return jnp.where(mine, x, pltpu.roll(x, HEAD_DIM, 1))

        def with_ext(x, pos, mask_block=0):
            lane = _iota(x.shape, 1)
            return jnp.where(lane < HEAD_DIM, both_halves(x), _key_ext(pos, lane - ext0, mask_block)).astype(MXU_DTYPE)

        def transposed(x):
            return both_halves(x).T[0:HEAD_DIM].astype(MXU_DTYPE)

        qx_ref[...] = jnp.zeros_like(qx_ref)
        kcx[...] = with_ext(kc_ref[0], _iota((nr, LANES), 0) * CMP_STRIDE + (CMP_LEN - 1))
        vct[...] = transposed(vc_ref[0])
        lane = _iota((WINDOW, LANES), 1)
        kwx[0:WINDOW, :] = _key_ext(lane * 0, lane - ext0, pad=lane >= 0).astype(MXU_DTYPE)
        for c in range(n_back):
            vwt[c] = jnp.zeros(vwt.shape[1:], vwt.dtype)
        for c in range(S // sblk):
            rows = slice(c * sblk, (c + 1) * sblk)
            pos = c * sblk + _iota((sblk, LANES), 0)
            ksx[rows, :] = with_ext(ks_ref[0, rows, :], pos, SLC_BLOCK)
            kwx[WINDOW + c * sblk:WINDOW + (c + 1) * sblk, :] = with_ext(kw_ref[0, rows, :], pos)
            vst[:, rows] = transposed(vs_ref[0, rows, :])
        for c in range(S // tq):
            vwt[n_back + c] = transposed(vw_ref[0, c * tq:(c + 1) * tq, :])

    qat = qa_ref[0].T
    qbt = qb_ref[0].T
    qt = jnp.concatenate([qat[0:HEAD_DIM], qat[HEAD_DIM:LANES], qbt[0:HEAD_DIM], qbt[HEAD_DIM:LANES]], axis=1)
    col1 = _iota((1, cols), 1)
    h0 = NSA_GROUP * g
    slope = jnp.where(col1 < tq, slopes_ref[h0],
                      jnp.where(col1 < 2 * tq, slopes_ref[h0 + 1],
                                jnp.where(col1 < 3 * tq, slopes_ref[h0 + 2], slopes_ref[h0 + 3])))
    i1 = col1 & (tq - 1)
    t1 = q0 + i1
    qx_ref[0:HEAD_DIM, :] = (qt * SCALE).astype(qx_ref.dtype)
    qx_ref[ext0:ext0 + EXT_MASK_LANE, :] = jnp.concatenate(
        [_query_ext(slope, t1), jnp.zeros((EXT_MASK_LANE - SUBLANES, cols), F32)], axis=0).astype(qx_ref.dtype)
    qx = qx_ref[...]

    relw = _iota((tq, cols), 0) - i1
    s = jnp.dot(kwx[pl.ds(pl.multiple_of(q0, tq), WINDOW + tq), :], qx, preferred_element_type=F32)
    s = jnp.concatenate([jnp.where(relw > 0, s[0:tq], NEG), s[tq:WINDOW],
                         jnp.where(relw <= 0, s[WINDOW:WINDOW + tq], NEG)], axis=0)
    p, inv = _softmax_cols(s)
    pb = p.astype(MXU_DTYPE)
    acc = _dot(vwt[qi], pb[0:tq])
    for e in range(1, n_back + 1):
        acc = acc + _dot(vwt[qi + e], pb[e * tq:(e + 1) * tq])
    owin_ref[...] = acc * inv

    nidx = _iota((nr, cols), 0)
    maskc = (nidx * CMP_STRIDE + (CMP_LEN - 1) <= t1) & (nidx < n_cmp)
    s = jnp.where(maskc, jnp.dot(kcx[...], qx, preferred_element_type=F32), NEG)
    p = jnp.where(maskc, jnp.exp(s - jnp.max(s, axis=0, keepdims=True)), 0.0)
    p = p / jnp.maximum(jnp.sum(p, axis=0, keepdims=True), 1e-30)
    ocmp_ref[...] = _dot(vct[...], p)

    psum = p[:, 0:tq] + p[:, tq:2 * tq] + p[:, 2 * tq:3 * tq] + p[:, 3 * tq:4 * tq]
    jj = _iota((nj, nr), 0) * SLC_BLOCK
    nn = _iota((nj, nr), 1) * CMP_STRIDE
    overlap = ((nn < jj + SLC_BLOCK) & (nn + CMP_LEN > jj)
               & (nn < n_cmp * CMP_STRIDE) & (jj < n_slc * SLC_BLOCK)).astype(F32)
    imp = _dot_hi(overlap, psum)
    jl = _iota((nj, tq), 0)
    tb = _div_pow2(q0 + _iota((nj, tq), 1), SLC_BLOCK)
    cand = jl <= tb
    forced = (jl == 0) | (jl == tb) | (jl == tb - 1)
    score = jnp.where(cand, jnp.where(forced, SLC_FORCE, imp), NEG)
    keep = cand & (_rank_before(score, n_slc, jl) < k_top)
    mask0 = ext0 + EXT_MASK_LANE
    qx_ref[mask0:mask0 + nj, :] = jnp.concatenate([jnp.where(keep, 0.0, NEG)] * NSA_GROUP, axis=1).astype(qx_ref.dtype)

    z = g_ref[0].T
    gt_ref[...] = 1.0 / (1.0 + jnp.exp(-z))

    def combine(oslc):
        ocmp = ocmp_ref[...]
        owin = owin_ref[...]
        heads = []
        for r in range(NSA_GROUP):
            sl = slice(r * tq, (r + 1) * tq)
            c = NSA_BRANCHES * (NSA_GROUP * g + r)
            heads.append(gt_ref[pl.ds(c, 1), :] * ocmp[:, sl] + gt_ref[pl.ds(c + 1, 1), :] * oslc[:, sl]
                         + gt_ref[pl.ds(c + 2, 1), :] * owin[:, sl])
        o_ref[0, :, 0:LANES] = jnp.concatenate(heads[0:2], axis=0).T
        o_ref[0, :, LANES:2 * LANES] = jnp.concatenate(heads[2:4], axis=0).T

    def slc_attend(c):
        nkeys = (c + 1) * sblk
        s = jnp.dot(ksx[0:nkeys, :], qx_ref[...], preferred_element_type=F32)
        causal = _iota((sblk, cols), 0) + (c * sblk - q0) <= i1
        diag = jnp.where(causal, s[c * sblk:nkeys], NEG)
        s = jnp.concatenate([s[0:c * sblk], diag], axis=0) if c else diag
        p, inv = _softmax_cols(s)
        combine(_dot(vst[:, 0:nkeys], p) * inv)

    dc = _div_pow2(q0, sblk)
    for c in range(S // sblk):
        @pl.when(dc == c)
        def _(c=c):
            slc_attend(c)


def _nsa(slopes, proj, kc, vc):
    B, S, _ = proj.shape
    tq = NSA_QTILE
    n_cmp = (S - CMP_LEN) // CMP_STRIDE + 1
    n_slc = S // SLC_BLOCK
    k_top = min(SLC_TOPK, n_slc)
    nr = kc.shape[1]
    cols = NSA_GROUP * tq
    assert HEAD_DIM + EXT_MASK_LANE + _round_up(n_slc, 2 * SUBLANES) <= LANES and S // POS_SPLIT <= POS_SPLIT
    assert SLC_CHUNK % tq == 0 and WINDOW % tq == 0 and S % SLC_CHUNK == 0
    kern = functools.partial(_nsa_kernel, S=S, tq=tq, n_cmp=n_cmp, n_slc=n_slc, k_top=k_top)
    qtiles = NSA_GROUP // HEADS_PER_TILE

    def seq(col):
        return pl.BlockSpec((1, S, LANES), lambda b, g, i: (b, 0, col))

    small = pl.BlockSpec((1, nr, LANES), lambda b, g, i: (b, 0, 0))
    return pl.pallas_call(
        kern,
        grid=(B, NSA_KV_HEADS, S // tq),
        in_specs=[
            pl.BlockSpec(memory_space=pltpu.SMEM),
            pl.BlockSpec((1, tq, LANES), lambda b, g, i: (b, i, COL_QN + qtiles * g)),
            pl.BlockSpec((1, tq, LANES), lambda b, g, i: (b, i, COL_QN + qtiles * g + 1)),
            pl.BlockSpec((1, tq, LANES), lambda b, g, i: (b, i, COL_G)),
            small, small, seq(COL_KS), seq(COL_VS), seq(COL_KW), seq(COL_VW),
        ],
        out_specs=pl.BlockSpec((1, tq, qtiles * LANES), lambda b, g, i: (b, i, g)),
        out_shape=jax.ShapeDtypeStruct((B, S, NSA_HEADS * HEAD_DIM), F32),
        scratch_shapes=[
            pltpu.VMEM((nr, LANES), MXU_DTYPE), pltpu.VMEM((HEAD_DIM, nr), MXU_DTYPE),
            pltpu.VMEM((S, LANES), MXU_DTYPE), pltpu.VMEM((HEAD_DIM, S), MXU_DTYPE),
            pltpu.VMEM((WINDOW + S, LANES), MXU_DTYPE),
            pltpu.VMEM((WINDOW // tq + S // tq, HEAD_DIM, tq), MXU_DTYPE),
            pltpu.VMEM((LANES, cols), MXU_DTYPE),
            pltpu.VMEM((LANES, tq), F32),
            pltpu.VMEM((HEAD_DIM, cols), F32), pltpu.VMEM((HEAD_DIM, cols), F32),
        ],
        compiler_params=_params("parallel", "parallel", "arbitrary"),
        name="nsa",
    )(slopes, proj, proj, proj, kc, vc, proj, proj, proj, proj)


def _outproj_kernel(om_ref, on_ref, x_ref, gm_ref, gn_ref, wm_ref, wn_ref, o_ref):
    y = _dot(_rms(om_ref[...], gm_ref[...]), wm_ref[...]) + _dot(_rms(on_ref[...], gn_ref[...]), wn_ref[...])
    o_ref[...] = x_ref[...] + y


def _outproj(om, on, x, gm, gn, w, l, tm):
    T, D = x.shape
    Wm, Wn = om.shape[1], on.shape[1]
    assert Wm == Wn and w.shape[1] == Wm + Wn
    return pl.pallas_call(
        _outproj_kernel,
        grid=(T // tm,),
        in_specs=[
            pl.BlockSpec((tm, Wm), lambda i: (i, 0)),
            pl.BlockSpec((tm, Wn), lambda i: (i, 0)),
            pl.BlockSpec((tm, D), lambda i: (i, 0)),
            pl.BlockSpec((None, 1, Wm), lambda i: (l, 0, 0)),
            pl.BlockSpec((None, 1, Wn), lambda i: (l, 0, 0)),
            pl.BlockSpec((None, Wm, D), lambda i: (l, 0, 0)),
            pl.BlockSpec((None, Wn, D), lambda i: (l, 1, 0)),
        ],
        out_specs=pl.BlockSpec((tm, D), lambda i: (i, 0)),
        out_shape=jax.ShapeDtypeStruct((T, D), F32),
        compiler_params=_params("parallel"),
        name="outproj",
    )(om, on, x, gm, gn, w, w)


def _alibi_slopes(n):
    slopes = 2.0 ** (-8.0 * np.arange(1, n + 1) / n)
    assert np.all(np.log2(slopes) == np.round(np.log2(slopes)))
    return jnp.asarray(slopes, dtype=F32)


def _token_tile(T, want):
    return want if T % want == 0 else T


def _mixer(x, B, S, l, mix_norm, w_in_p, pos_k, k_w1, k_w2, pos_v, v_w1, v_w2, moba_norm, nsa_norm, w_out):
    T, D = x.shape
    proj, ck, cv = _inproj(x, mix_norm, w_in_p, l, _token_tile(T, 512))
    proj = proj.reshape(B, S, IN_PAD)

    o_m = _moba(_alibi_slopes(MOBA_HEADS), proj)

    def windows(slab):
        return slab.reshape(NSA_KV_HEADS, B, S // CMP_STRIDE, CMP_STRIDE * HEAD_DIM)

    def placed(w2):
        return jnp.stack([jnp.pad(w2, ((0, 0), (g * HEAD_DIM, LANES - (g + 1) * HEAD_DIM)))
                          for g in range(NSA_KV_HEADS)])

    kc, vc = _compress(windows(ck), windows(cv), pos_k, pos_v, k_w1, v_w1, placed(k_w2), placed(v_w2), l)
    o_n = _nsa(_alibi_slopes(NSA_HEADS), proj, kc, vc)

    return _outproj(o_m.reshape(T, -1), o_n.reshape(T, -1), x, moba_norm, nsa_norm, w_out, l,
                    _token_tile(T, 512))


def _swiglu_step(x, l, norm, w_gate, w_up, w_down, final_g, final=False):
    T = x.shape[0]
    return _ffn(x, norm, w_gate, w_up, w_down, l, _token_tile(T, 1024), 256, final_g, final)


def _rows(p):
    return p.reshape(p.shape[0], 1, -1)


@jax.jit
def kernel(x, ffa_norm, ffa_w_gate, ffa_w_up, ffa_w_down, mix_norm, w_in, cmp_pos_k, cmp_k_w1, cmp_k_w2,
           cmp_pos_v, cmp_v_w1, cmp_v_w2, moba_out_norm, nsa_out_norm, w_out, ffb_norm, ffb_w_gate,
           ffb_w_up, ffb_w_down, final_norm):
    B, S, D = x.shape
    assert S % MOBA_BLOCK == 0 and S >= WINDOW and w_in.shape[-1] == IN_WIDTH
    h = x.reshape(B * S, D)
    w_in_p = jnp.pad(w_in, ((0, 0), (0, 0), (0, IN_PAD - IN_WIDTH)))
    ffa_norm, ffb_norm, mix_norm = _rows(ffa_norm), _rows(ffb_norm), _rows(mix_norm)
    moba_out_norm, nsa_out_norm = _rows(moba_out_norm), _rows(nsa_out_norm)
    cmp_pos_k, cmp_pos_v = _rows(cmp_pos_k), _rows(cmp_pos_v)
    depth = ffa_norm.shape[0]
    final_g = final_norm[None]
    for l in range(depth):
        h = _swiglu_step(h, l, ffa_norm, ffa_w_gate, ffa_w_up, ffa_w_down, final_g)
        h = _mixer(h, B, S, l, mix_norm, w_in_p, cmp_pos_k, cmp_k_w1, cmp_k_w2[l],
                   cmp_pos_v, cmp_v_w1, cmp_v_w2[l], moba_out_norm, nsa_out_norm, w_out)
        h = _swiglu_step(h, l, ffb_norm, ffb_w_gate, ffb_w_up, ffb_w_down, final_g, final=l == depth - 1)
    return h.reshape(B, S, D)
```

```python
import functools

import numpy as np
import jax
import jax.numpy as jnp
from jax import lax
from jax.experimental import pallas as pl
from jax.experimental.pallas import tpu as pltpu

HEAD_DIM = 64
MOBA_HEADS = 8
NSA_HEADS = 8
NSA_KV_HEADS = 2
NSA_GROUP = NSA_HEADS // NSA_KV_HEADS
NSA_BRANCHES = 3
MOBA_BLOCK = 256
MOBA_TOPK = 3
CMP_LEN = 32
CMP_STRIDE = 16
SLC_BLOCK = 64
SLC_TOPK = 16
WINDOW = 512
NEG = -1e30
SLC_FORCE = 1e4
EPS = 1e-6
SCALE = HEAD_DIM ** -0.5

LANES = 128
SUBLANES = 8
HEADS_PER_TILE = LANES // HEAD_DIM
COL_QM, COL_KM, COL_VM, COL_QN = 0, 4, 8, 12
COL_KC, COL_VC, COL_KS, COL_VS, COL_KW, COL_VW, COL_G = 16, 17, 18, 19, 20, 21, 22
IN_TILES = 24
IN_PAD = IN_TILES * LANES
IN_WIDTH = 2840
NSA_QTILE = 256
MOBA_TILE_BLOCKS = 2
SLC_CHUNK = 4 * SLC_BLOCK

MXU_DTYPE = jnp.bfloat16
VMEM_LIMIT = 48 * 1024 * 1024
F32 = jnp.float32
HI = lax.Precision.HIGHEST


def _dot(a, b):
    return jnp.dot(a.astype(MXU_DTYPE), b.astype(MXU_DTYPE), preferred_element_type=F32)


def _dot_nt(a, b, precision=None):
    return lax.dot_general(a, b, (((1,), (1,)), ((), ())), precision=precision,
                           preferred_element_type=F32)


def _dot_hi(a, b):
    return jnp.dot(a, b, precision=HI, preferred_element_type=F32)


def _iota(shape, dim):
    return lax.broadcasted_iota(jnp.int32, shape, dim)


def _div_pow2(x, n):
    assert n & (n - 1) == 0
    return x >> (n.bit_length() - 1)


def _round_up(n, m):
    return -(-n // m) * m


def _params(*sem):
    return pltpu.CompilerParams(dimension_semantics=sem, vmem_limit_bytes=VMEM_LIMIT)


def _rms(x, g):
    return x * lax.rsqrt(jnp.mean(x * x, axis=-1, keepdims=True) + EPS) * g


def _ffn_kernel(x_ref, g_ref, wg_ref, wu_ref, wd_ref, gf_ref, o_ref, h_ref, wgb, wub, wdb, *, nj, final):
    step = pl.program_id(0)

    def start():
        h_ref[...] = _rms(x_ref[...], g_ref[...]).astype(h_ref.dtype)

    def chunk(wg, wu, wd):
        h = h_ref[...]
        a = _dot(h, wg)
        b = _dot(h, wu)
        return _dot(a / (1.0 + jnp.exp(-a)) * b, wd)

    def finish():
        y = x_ref[...] + 0.5 * o_ref[...]
        o_ref[...] = _rms(y, gf_ref[...]) if final else y

    @pl.when(step < nj)
    def _():
        wgb[step] = wg_ref[...].astype(wgb.dtype)
        wub[step] = wu_ref[...].astype(wub.dtype)
        wdb[step] = wd_ref[...].astype(wdb.dtype)

        @pl.when(step == 0)
        def _():
            start()
            o_ref[...] = chunk(wgb[0], wub[0], wdb[0])

        @pl.when(step > 0)
        def _():
            o_ref[...] += chunk(wgb[step], wub[step], wdb[step])

        @pl.when(step == nj - 1)
        def _():
            finish()

    @pl.when(step >= nj)
    def _():
        start()
        o_ref[...] = chunk(wgb[0], wub[0], wdb[0])

        def body(j, carry):
            o_ref[...] += chunk(wgb[j], wub[j], wdb[j])
            return carry

        lax.fori_loop(1, nj, body, 0, unroll=2)
        finish()


def _ffn(x, g, wg, wu, wd, l, tm, tf, final_g, final):
    T, D = x.shape
    F = wg.shape[2]
    nj = F // tf
    tile = lambda s: (jnp.maximum(s - (nj - 1), 0), 0)
    col = lambda s: jnp.minimum(s, nj - 1)
    return pl.pallas_call(
        functools.partial(_ffn_kernel, nj=nj, final=final),
        grid=(nj + T // tm - 1,),
        in_specs=[
            pl.BlockSpec((tm, D), tile),
            pl.BlockSpec((None, 1, D), lambda s: (l, 0, 0)),
            pl.BlockSpec((None, D, tf), lambda s: (l, 0, col(s))),
            pl.BlockSpec((None, D, tf), lambda s: (l, 0, col(s))),
            pl.BlockSpec((None, tf, D), lambda s: (l, col(s), 0)),
            pl.BlockSpec((1, D), lambda s: (0, 0)),
        ],
        out_specs=pl.BlockSpec((tm, D), tile),
        out_shape=jax.ShapeDtypeStruct((T, D), F32),
        scratch_shapes=[pltpu.VMEM((tm, D), MXU_DTYPE), pltpu.VMEM((nj, D, tf), MXU_DTYPE),
                        pltpu.VMEM((nj, D, tf), MXU_DTYPE), pltpu.VMEM((nj, tf, D), MXU_DTYPE)],
        compiler_params=_params("arbitrary"),
        name="ffn",
    )(x, g, wg, wu, wd, final_g)


def _inproj_kernel(x_ref, g_ref, w_ref, o_ref, ck_ref, cv_ref, wb_ref, kc_tile, vc_tile):
    @pl.when(pl.program_id(0) == 0)
    def _():
        wb_ref[...] = jnp.zeros_like(wb_ref)
        wb_ref[:, 0:w_ref.shape[1]] = w_ref[...].astype(wb_ref.dtype)

    y = _dot(_rms(x_ref[...], g_ref[...]), wb_ref[...])
    o_ref[...] = y
    kc_tile[...] = y[:, COL_KC * LANES:(COL_KC + 1) * LANES]
    vc_tile[...] = y[:, COL_VC * LANES:(COL_VC + 1) * LANES]
    rows = o_ref.shape[0] // CMP_STRIDE
    for t in range(CMP_STRIDE):
        for ref, tile in ((ck_ref, kc_tile), (cv_ref, vc_tile)):
            tok = tile[pl.ds(t, rows, stride=CMP_STRIDE), :]
            for grp in range(NSA_KV_HEADS):
                ref[grp, :, t * HEAD_DIM:(t + 1) * HEAD_DIM] = tok[:, grp * HEAD_DIM:(grp + 1) * HEAD_DIM]


def _inproj(x, g, w, l, tm):
    T, D = x.shape
    W = w.shape[2]
    N = _round_up(W, LANES * 4)
    slab = pl.BlockSpec((NSA_KV_HEADS, tm // CMP_STRIDE, CMP_STRIDE * HEAD_DIM), lambda i: (0, i, 0))
    slab_shape = jax.ShapeDtypeStruct((NSA_KV_HEADS, T // CMP_STRIDE, CMP_STRIDE * HEAD_DIM), F32)
    return pl.pallas_call(
        _inproj_kernel,
        grid=(T // tm,),
        in_specs=[
            pl.BlockSpec((tm, D), lambda i: (i, 0)),
            pl.BlockSpec((None, 1, D), lambda i: (l, 0, 0)),
            pl.BlockSpec((None, D, W), lambda i: (l, 0, 0), pipeline_mode=pl.Buffered(1)),
        ],
        out_specs=(pl.BlockSpec((tm, N), lambda i: (i, 0)), slab, slab),
        out_shape=(jax.ShapeDtypeStruct((T, N), F32), slab_shape, slab_shape),
        scratch_shapes=[pltpu.VMEM((D, N), MXU_DTYPE), pltpu.VMEM((tm, LANES), F32), pltpu.VMEM((tm, LANES), F32)],
        compiler_params=_params("arbitrary"),
        name="inproj",
    )(x, g, w)


POS_SPLIT = 256
EXT_PAD_LANE = 4
EXT_MASK_LANE = 2 * SUBLANES


def _key_ext(pos, lane, mask_block=0, pad=None):
    ext = jnp.where(lane == 0, _div_pow2(pos, POS_SPLIT),
                    jnp.where(lane == 1, pos & (POS_SPLIT - 1),
                              jnp.where((lane == 2) | (lane == 3), 1, 0)))
    if mask_block:
        ext = jnp.where(lane - EXT_MASK_LANE == _div_pow2(pos, mask_block), 1, ext)
    if pad is not None:
        ext = jnp.where(pad, jnp.where(lane == EXT_PAD_LANE, 1, 0), ext)
    return ext.astype(F32)


def _query_ext(slope, t):
    r = _iota((SUBLANES, t.shape[1]), 0)
    big = slope * float(POS_SPLIT)
    hi = _div_pow2(t, POS_SPLIT).astype(F32)
    lo = (t & (POS_SPLIT - 1)).astype(F32)
    return jnp.where(r == 0, big, jnp.where(r == 1, slope, jnp.where(r == 2, -(big * hi),
                     jnp.where(r == 3, -(slope * lo), jnp.where(r == EXT_PAD_LANE, NEG, 0.0)))))


def _softmax_cols(s):
    p = jnp.exp(s - jnp.max(s, axis=0, keepdims=True))
    return p, 1.0 / jnp.sum(p, axis=0, keepdims=True)


def _rank_before(score, n, idx):
    rank = jnp.zeros(score.shape, jnp.int32)
    for m in range(n):
        row = score[m:m + 1, :]
        better = (row > score) | ((row == score) & (m < idx))
        rank = rank + better.astype(jnp.int32)
    return rank


def _moba_kernel(slopes_ref, q_ref, k_ref, v_ref, o_ref,
                 kmean_ref, kb_ref, vt_ref, row_ref, *, nb, kk, tq):
    hp = pl.program_id(1)
    ti = pl.program_id(2)
    blk = MOBA_BLOCK
    per = tq // blk
    cols = HEADS_PER_TILE * tq

    @pl.when(ti == 0)
    def _():
        kmean_ref[...] = jnp.zeros_like(kmean_ref)
        for n in range(nb):
            rows = slice(n * blk, (n + 1) * blk)
            kmean_ref[n:n + 1, :] = jnp.mean(k_ref[0, rows, :], axis=0, keepdims=True)
            vt_ref[:, rows] = v_ref[0, rows, :].T.astype(vt_ref.dtype)
        kb_ref[...] = k_ref[0].astype(kb_ref.dtype)

    q0 = ti * tq
    lo = _iota((tq, LANES), 1) < HEAD_DIM
    q = q_ref[0]
    qs = jnp.concatenate([jnp.where(lo, q, 0.0), jnp.where(lo, 0.0, q)], axis=0)
    col1 = _iota((1, cols), 1)
    slope = jnp.where(col1 < tq, slopes_ref[HEADS_PER_TILE * hp], slopes_ref[HEADS_PER_TILE * hp + 1])
    i1 = col1 & (tq - 1)
    own = ti * per + _div_pow2(i1, blk)

    gate = _dot_nt(kmean_ref[...], qs, precision=HI)
    n_idx = _iota(gate.shape, 0)
    cand = n_idx < own
    gm = jnp.where(cand, gate, NEG)
    keep = (cand & (_rank_before(gm, nb - 1, n_idx) < kk)) | (n_idx == own)
    row_ref[...] = slope * (n_idx * blk - q0).astype(F32) + jnp.where(keep, 0.0, NEG)

    qsb = (qs * SCALE).astype(MXU_DTYPE)
    rel = _iota((blk, cols), 0) - i1
    bias = slope * rel.astype(F32)

    def attend(t):
        first = per * t
        nkeys = (first + per) * blk
        s = _dot_nt(kb_ref[0:nkeys, :], qsb)
        pieces = []
        for n in range(first + per):
            piece = s[n * blk:(n + 1) * blk] + bias + row_ref[n:n + 1, :]
            if n >= first:
                piece = jnp.where(rel + (n - first) * blk <= 0, piece, NEG)
            pieces.append(piece)
        s = jnp.concatenate(pieces, axis=0)
        p = jnp.exp(s - jnp.max(s, axis=0, keepdims=True))
        inv = 1.0 / jnp.sum(p, axis=0, keepdims=True)
        pb = p.astype(MXU_DTYPE)
        halves = []
        for h in range(HEADS_PER_TILE):
            dims = slice(h * HEAD_DIM, (h + 1) * HEAD_DIM)
            qcols = slice(h * tq, (h + 1) * tq)
            halves.append(_dot(vt_ref[dims, 0:nkeys], pb[:, qcols]) * inv[:, qcols])
        o_ref[0] = jnp.concatenate(halves, axis=0).T

    for t in range(nb // per):
        @pl.when(ti == t)
        def _(t=t):
            attend(t)


def _moba(slopes, proj):
    B, S, _ = proj.shape
    tq = MOBA_TILE_BLOCKS * MOBA_BLOCK
    nb = S // MOBA_BLOCK
    assert S % tq == 0
    kk = max(1, min(MOBA_TOPK, nb - 1))
    n_tiles = MOBA_HEADS // HEADS_PER_TILE
    kern = functools.partial(_moba_kernel, nb=nb, kk=kk, tq=tq)
    return pl.pallas_call(
        kern,
        grid=(B, n_tiles, S // tq),
        in_specs=[
            pl.BlockSpec(memory_space=pltpu.SMEM),
            pl.BlockSpec((1, tq, LANES), lambda b, h, i: (b, i, COL_QM + h)),
            pl.BlockSpec((1, S, LANES), lambda b, h, i: (b, 0, COL_KM + h)),
            pl.BlockSpec((1, S, LANES), lambda b, h, i: (b, 0, COL_VM + h)),
        ],
        out_specs=pl.BlockSpec((1, tq, LANES), lambda b, h, i: (b, i, h)),
        out_shape=jax.ShapeDtypeStruct((B, S, MOBA_HEADS * HEAD_DIM), F32),
        scratch_shapes=[
            pltpu.VMEM((_round_up(nb, SUBLANES), LANES), F32),
            pltpu.VMEM((S, LANES), MXU_DTYPE),
            pltpu.VMEM((LANES, S), MXU_DTYPE),
            pltpu.VMEM((_round_up(nb, SUBLANES), HEADS_PER_TILE * tq), F32),
        ],
        compiler_params=_params("parallel", "parallel", "arbitrary"),
        name="moba",
    )(slopes, proj, proj, proj)


def _gelu_tanh(x):
    return x * (0.5 * (1.0 + jnp.tanh(0.7978845608028654 * (x + 0.044715 * (x * x * x)))))


def _compress_kernel(xk_ref, xv_ref, pk_ref, pv_ref, w1k_ref, w1v_ref, w2k_ref, w2v_ref, kc_ref, vc_ref):
    g = pl.program_id(0)
    half = (CMP_LEN // 2) * HEAD_DIM
    for x_ref, p_ref, w1_ref, w2_ref, o_ref in ((xk_ref, pk_ref, w1k_ref, w2k_ref, kc_ref),
                                               (xv_ref, pv_ref, w1v_ref, w2v_ref, vc_ref)):
        B, nr, W = x_ref.shape
        x = x_ref[...].reshape(B * nr, W)
        top = _dot_hi(x + p_ref[:, 0:half], w1_ref[0:half, :])
        bot = _dot_hi(x + p_ref[:, half:2 * half], w1_ref[half:2 * half, :])
        hid = top + pltpu.roll(bot, B * nr - 1, 0)
        out = _dot_hi(_gelu_tanh(hid), w2_ref[...]).reshape(B, nr, LANES)

        @pl.when(g == 0)
        def _(o_ref=o_ref, out=out):
            o_ref[...] = out

        @pl.when(g > 0)
        def _(o_ref=o_ref, out=out):
            o_ref[...] += out


def _compress(xk, xv, pk, pv, w1k, w1v, w2k, w2v, l):
    G, B, nr, W = xk.shape
    H = w1k.shape[2]
    layer = lambda *shape: pl.BlockSpec((None,) + shape, lambda g: (l,) + (0,) * len(shape))
    xspec = pl.BlockSpec((None, B, nr, W), lambda g: (g, 0, 0, 0))
    w2spec = pl.BlockSpec((None, H, LANES), lambda g: (g, 0, 0))
    ospec = pl.BlockSpec((B, nr, LANES), lambda g: (0, 0, 0))
    oshape = jax.ShapeDtypeStruct((B, nr, LANES), F32)
    return pl.pallas_call(
        _compress_kernel,
        grid=(G,),
        in_specs=[xspec, xspec, layer(1, 2 * W), layer(1, 2 * W), layer(2 * W, H), layer(2 * W, H),
                  w2spec, w2spec],
        out_specs=(ospec, ospec),
        out_shape=(oshape, oshape),
        compiler_params=_params("arbitrary"),
        name="nsa_compress",
    )(xk, xv, pk, pv, w1k, w1v, w2k, w2v)


def _nsa_kernel(slopes_ref, qa_ref, qb_ref, g_ref, kc_ref, vc_ref, ks_ref, vs_ref, kw_ref, vw_ref,
                o_ref, kcx, vct, ksx, vst, kwx, vwt, qx_ref, gt_ref, ocmp_ref, owin_ref,
                *, S, tq, n_cmp, n_slc, k_top):
    g = pl.program_id(1)
    qi = pl.program_id(2)
    q0 = qi * tq
    cols = NSA_GROUP * tq
    sblk = SLC_CHUNK
    nr = kcx.shape[0]
    nj = _round_up(n_slc, 2 * SUBLANES)
    n_back = WINDOW // tq
    ext0 = HEAD_DIM

    @pl.when(qi == 0)
    def _():
        def both_halves(x):
            mine = (_iota(x.shape, 1) >= HEAD_DIM) == (g == 1)
            return jnp.where(mine, x, pltpu.roll(x, HEAD_DIM, 1))

        def with_ext(x, pos, mask_block=0):
            lane = _iota(x.shape, 1)
            return jnp.where(lane < HEAD_DIM, both_halves(x), _key_ext(pos, lane - ext0, mask_block)).astype(MXU_DTYPE)

        def transposed(x):
            return both_halves(x).T[0:HEAD_DIM].astype(MXU_DTYPE)

        qx_ref[...] = jnp.zeros_like(qx_ref)
        kcx[...] = with_ext(kc_ref[0], _iota((nr, LANES), 0) * CMP_STRIDE + (CMP_LEN - 1))
        vct[...] = transposed(vc_ref[0])
        lane = _iota((WINDOW, LANES), 1)
        kwx[0:WINDOW, :] = _key_ext(lane * 0, lane - ext0, pad=lane >= 0).astype(MXU_DTYPE)
        for c in range(n_back):
            vwt[c] = jnp.zeros(vwt.shape[1:], vwt.dtype)
        for c in range(S // sblk):
            rows = slice(c * sblk, (c + 1) * sblk)
            pos = c * sblk + _iota((sblk, LANES), 0)
            ksx[rows, :] = with_ext(ks_ref[0, rows, :], pos, SLC_BLOCK)
            kwx[WINDOW + c * sblk:WINDOW + (c + 1) * sblk, :] = with_ext(kw_ref[0, rows, :], pos)
            vst[:, rows] = transposed(vs_ref[0, rows, :])
        for c in range(S // tq):
            vwt[n_back + c] = transposed(vw_ref[0, c * tq:(c + 1) * tq, :])

    qat = qa_ref[0].T
    qbt = qb_ref[0].T
    qt = jnp.concatenate([qat[0:HEAD_DIM], qat[HEAD_DIM:LANES], qbt[0:HEAD_DIM], qbt[HEAD_DIM:LANES]], axis=1)
    col1 = _iota((1, cols), 1)
    h0 = NSA_GROUP * g
    slope = jnp.where(col1 < tq, slopes_ref[h0],
                      jnp.where(col1 < 2 * tq, slopes_ref[h0 + 1],
                                jnp.where(col1 < 3 * tq, slopes_ref[h0 + 2], slopes_ref[h0 + 3])))
    i1 = col1 & (tq - 1)
    t1 = q0 + i1
    qx_ref[0:HEAD_DIM, :] = (qt * SCALE).astype(qx_ref.dtype)
    qx_ref[ext0:ext0 + EXT_MASK_LANE, :] = jnp.concatenate(
        [_query_ext(slope, t1), jnp.zeros((EXT_MASK_LANE - SUBLANES, cols), F32)], axis=0).astype(qx_ref.dtype)
    qx = qx_ref[...]

    relw = _iota((tq, cols), 0) - i1
    s = jnp.dot(kwx[pl.ds(pl.multiple_of(q0, tq), WINDOW + tq), :], qx, preferred_element_type=F32)
    s = jnp.concatenate([jnp.where(relw > 0, s[0:tq], NEG), s[tq:WINDOW],
                         jnp.where(relw <= 0, s[WINDOW:WINDOW + tq], NEG)], axis=0)
    p, inv = _softmax_cols(s)
    pb = p.astype(MXU_DTYPE)
    acc = _dot(vwt[qi], pb[0:tq])
    for e in range(1, n_back + 1):
        acc = acc + _dot(vwt[qi + e], pb[e * tq:(e + 1) * tq])
    owin_ref[...] = acc * inv

    nidx = _iota((nr, cols), 0)
    maskc = (nidx * CMP_STRIDE + (CMP_LEN - 1) <= t1) & (nidx < n_cmp)
    s = jnp.where(maskc, jnp.dot(kcx[...], qx, preferred_element_type=F32), NEG)
    p = jnp.where(maskc, jnp.exp(s - jnp.max(s, axis=0, keepdims=True)), 0.0)
    p = p / jnp.maximum(jnp.sum(p, axis=0, keepdims=True), 1e-30)
    ocmp_ref[...] = _dot(vct[...], p)

    psum = p[:, 0:tq] + p[:, tq:2 * tq] + p[:, 2 * tq:3 * tq] + p[:, 3 * tq:4 * tq]
    jj = _iota((nj, nr), 0) * SLC_BLOCK
    nn = _iota((nj, nr), 1) * CMP_STRIDE
    overlap = ((nn < jj + SLC_BLOCK) & (nn + CMP_LEN > jj)
               & (nn < n_cmp * CMP_STRIDE) & (jj < n_slc * SLC_BLOCK)).astype(F32)
    imp = _dot_hi(overlap, psum)
    jl = _iota((nj, tq), 0)
    tb = _div_pow2(q0 + _iota((nj, tq), 1), SLC_BLOCK)
    cand = jl <= tb
    forced = (jl == 0) | (jl == tb) | (jl == tb - 1)
    score = jnp.where(cand, jnp.where(forced, SLC_FORCE, imp), NEG)
    keep = cand & (_rank_before(score, n_slc, jl) < k_top)
    mask0 = ext0 + EXT_MASK_LANE
    qx_ref[mask0:mask0 + nj, :] = jnp.concatenate([jnp.where(keep, 0.0, NEG)] * NSA_GROUP, axis=1).astype(qx_ref.dtype)

    z = g_ref[0].T
    gt_ref[...] = 1.0 / (1.0 + jnp.exp(-z))

    def combine(oslc):
        ocmp = ocmp_ref[...]
        owin = owin_ref[...]
        heads = []
        for r in range(NSA_GROUP):
            sl = slice(r * tq, (r + 1) * tq)
            c = NSA_BRANCHES * (NSA_GROUP * g + r)
            heads.append(gt_ref[pl.ds(c, 1), :] * ocmp[:, sl] + gt_ref[pl.ds(c + 1, 1), :] * oslc[:, sl]
                         + gt_ref[pl.ds(c + 2, 1), :] * owin[:, sl])
        o_ref[0, :, 0:LANES] = jnp.concatenate(heads[0:2], axis=0).T
        o_ref[0, :, LANES:2 * LANES] = jnp.concatenate(heads[2:4], axis=0).T

    def slc_attend(c):
        nkeys = (c + 1) * sblk
        s = jnp.dot(ksx[0:nkeys, :], qx_ref[...], preferred_element_type=F32)
        causal = _iota((sblk, cols), 0) + (c * sblk - q0) <= i1
        diag = jnp.where(causal, s[c * sblk:nkeys], NEG)
        s = jnp.concatenate([s[0:c * sblk], diag], axis=0) if c else diag
        p, inv = _softmax_cols(s)
        combine(_dot(vst[:, 0:nkeys], p) * inv)

    dc = _div_pow2(q0, sblk)
    for c in range(S // sblk):
        @pl.when(dc == c)
        def _(c=c):
            slc_attend(c)


def _nsa(slopes, proj, kc, vc):
    B, S, _ = proj.shape
    tq = NSA_QTILE
    n_cmp = (S - CMP_LEN) // CMP_STRIDE + 1
    n_slc = S // SLC_BLOCK
    k_top = min(SLC_TOPK, n_slc)
    nr = kc.shape[1]
    cols = NSA_GROUP * tq
    assert HEAD_DIM + EXT_MASK_LANE + _round_up(n_slc, 2 * SUBLANES) <= LANES and S // POS_SPLIT <= POS_SPLIT
    assert SLC_CHUNK % tq == 0 and WINDOW % tq == 0 and S % SLC_CHUNK == 0
    kern = functools.partial(_nsa_kernel, S=S, tq=tq, n_cmp=n_cmp, n_slc=n_slc, k_top=k_top)
    qtiles = NSA_GROUP // HEADS_PER_TILE

    def seq(col):
        return pl.BlockSpec((1, S, LANES), lambda b, g, i: (b, 0, col))

    small = pl.BlockSpec((1, nr, LANES), lambda b, g, i: (b, 0, 0))
    return pl.pallas_call(
        kern,
        grid=(B, NSA_KV_HEADS, S // tq),
        in_specs=[
            pl.BlockSpec(memory_space=pltpu.SMEM),
            pl.BlockSpec((1, tq, LANES), lambda b, g, i: (b, i, COL_QN + qtiles * g)),
            pl.BlockSpec((1, tq, LANES), lambda b, g, i: (b, i, COL_QN + qtiles * g + 1)),
            pl.BlockSpec((1, tq, LANES), lambda b, g, i: (b, i, COL_G)),
            small, small, seq(COL_KS), seq(COL_VS), seq(COL_KW), seq(COL_VW),
        ],
        out_specs=pl.BlockSpec((1, tq, qtiles * LANES), lambda b, g, i: (b, i, g)),
        out_shape=jax.ShapeDtypeStruct((B, S, NSA_HEADS * HEAD_DIM), F32),
        scratch_shapes=[
            pltpu.VMEM((nr, LANES), MXU_DTYPE), pltpu.VMEM((HEAD_DIM, nr), MXU_DTYPE),
            pltpu.VMEM((S, LANES), MXU_DTYPE), pltpu.VMEM((HEAD_DIM, S), MXU_DTYPE),
            pltpu.VMEM((WINDOW + S, LANES), MXU_DTYPE),
            pltpu.VMEM((WINDOW // tq + S // tq, HEAD_DIM, tq), MXU_DTYPE),
            pltpu.VMEM((LANES, cols), MXU_DTYPE),
            pltpu.VMEM((LANES, tq), F32),
            pltpu.VMEM((HEAD_DIM, cols), F32), pltpu.VMEM((HEAD_DIM, cols), F32),
        ],
        compiler_params=_params("parallel", "parallel", "arbitrary"),
        name="nsa",
    )(slopes, proj, proj, proj, kc, vc, proj, proj, proj, proj)


def _outproj_kernel(om_ref, on_ref, x_ref, gm_ref, gn_ref, wm_ref, wn_ref, o_ref):
    y = _dot(_rms(om_ref[...], gm_ref[...]), wm_ref[...]) + _dot(_rms(on_ref[...], gn_ref[...]), wn_ref[...])
    o_ref[...] = x_ref[...] + y


def _outproj(om, on, x, gm, gn, w, l, tm):
    T, D = x.shape
    Wm, Wn = om.shape[1], on.shape[1]
    assert Wm == Wn and w.shape[1] == Wm + Wn
    return pl.pallas_call(
        _outproj_kernel,
        grid=(T // tm,),
        in_specs=[
            pl.BlockSpec((tm, Wm), lambda i: (i, 0)),
            pl.BlockSpec((tm, Wn), lambda i: (i, 0)),
            pl.BlockSpec((tm, D), lambda i: (i, 0)),
            pl.BlockSpec((None, 1, Wm), lambda i: (l, 0, 0)),
            pl.BlockSpec((None, 1, Wn), lambda i: (l, 0, 0)),
            pl.BlockSpec((None, Wm, D), lambda i: (l, 0, 0)),
            pl.BlockSpec((None, Wn, D), lambda i: (l, 1, 0)),
        ],
        out_specs=pl.BlockSpec((tm, D), lambda i: (i, 0)),
        out_shape=jax.ShapeDtypeStruct((T, D), F32),
        compiler_params=_params("parallel"),
        name="outproj",
    )(om, on, x, gm, gn, w, w)


def _alibi_slopes(n):
    slopes = 2.0 ** (-8.0 * np.arange(1, n + 1) / n)
    assert np.all(np.log2(slopes) == np.round(np.log2(slopes)))
    return jnp.asarray(slopes, dtype=F32)


def _token_tile(T, want):
    return want if T % want == 0 else T


def _mixer(x, B, S, l, mix_norm, w_in, pos_k, k_w1, k_w2, pos_v, v_w1, v_w2, moba_norm, nsa_norm, w_out):
    T, D = x.shape
    proj, ck, cv = _inproj(x, mix_norm, w_in, l, _token_tile(T, 512))
    proj = proj.reshape(B, S, IN_PAD)

    o_m = _moba(_alibi_slopes(MOBA_HEADS), proj)

    def windows(slab):
        return slab.reshape(NSA_KV_HEADS, B, S // CMP_STRIDE, CMP_STRIDE * HEAD_DIM)

    def placed(w2):
        return jnp.stack([jnp.pad(w2, ((0, 0), (g * HEAD_DIM, LANES - (g + 1) * HEAD_DIM)))
                          for g in range(NSA_KV_HEADS)])

    kc, vc = _compress(windows(ck), windows(cv), pos_k, pos_v, k_w1, v_w1, placed(k_w2), placed(v_w2), l)
    o_n = _nsa(_alibi_slopes(NSA_HEADS), proj, kc, vc)

    return _outproj(o_m.reshape(T, -1), o_n.reshape(T, -1), x, moba_norm, nsa_norm, w_out, l,
                    _token_tile(T, 512))


def _swiglu_step(x, l, norm, w_gate, w_up, w_down, final_g, final=False):
    T = x.shape[0]
    return _ffn(x, norm, w_gate, w_up, w_down, l, _token_tile(T, 1024), 256, final_g, final)


def _rows(p):
    return p.reshape(p.shape[0], 1, -1)


@jax.jit
def kernel(x, ffa_norm, ffa_w_gate, ffa_w_up, ffa_w_down, mix_norm, w_in, cmp_pos_k, cmp_k_w1, cmp_k_w2,
           cmp_pos_v, cmp_v_w1, cmp_v_w2, moba_out_norm, nsa_out_norm, w_out, ffb_norm, ffb_w_gate,
           ffb_w_up, ffb_w_down, final_norm):
    B, S, D = x.shape
    assert S % MOBA_BLOCK == 0 and S >= WINDOW and w_in.shape[-1] == IN_WIDTH
    h = x.reshape(B * S, D)
    ffa_norm, ffb_norm, mix_norm = _rows(ffa_norm), _rows(ffb_norm), _rows(mix_norm)
    moba_out_norm, nsa_out_norm = _rows(moba_out_norm), _rows(nsa_out_norm)
    cmp_pos_k, cmp_pos_v = _rows(cmp_pos_k), _rows(cmp_pos_v)
    depth = ffa_norm.shape[0]
    final_g = final_norm[None]
    for l in range(depth):
        h = _swiglu_step(h, l, ffa_norm, ffa_w_gate, ffa_w_up, ffa_w_down, final_g)
        h = _mixer(h, B, S, l, mix_norm, w_in, cmp_pos_k, cmp_k_w1, cmp_k_w2[l],
                   cmp_pos_v, cmp_v_w1, cmp_v_w2[l], moba_out_norm, nsa_out_norm, w_out)
        h = _swiglu_step(h, l, ffb_norm, ffb_w_gate, ffb_w_up, ffb_w_down, final_g, final=l == depth - 1)
    return h.reshape(B, S, D)
```

```python
import functools

import numpy as np
import jax
import jax.numpy as jnp
from jax import lax
from jax.experimental import pallas as pl
from jax.experimental.pallas import tpu as pltpu

HEAD_DIM = 64
MOBA_HEADS = 8
NSA_HEADS = 8
NSA_KV_HEADS = 2
NSA_GROUP = NSA_HEADS // NSA_KV_HEADS
NSA_BRANCHES = 3
MOBA_BLOCK = 256
MOBA_TOPK = 3
CMP_LEN = 32
CMP_STRIDE = 16
SLC_BLOCK = 64
SLC_TOPK = 16
WINDOW = 512
NEG = -1e30
SLC_FORCE = 1e4
EPS = 1e-6
SCALE = HEAD_DIM ** -0.5

LANES = 128
SUBLANES = 8
HEADS_PER_TILE = LANES // HEAD_DIM
COL_QM, COL_KM, COL_VM, COL_QN = 0, 4, 8, 12
COL_KC, COL_VC, COL_KS, COL_VS, COL_KW, COL_VW, COL_G = 16, 17, 18, 19, 20, 21, 22
IN_TILES = 24
IN_PAD = IN_TILES * LANES
IN_WIDTH = 2840
NSA_QTILE = 256
MOBA_TILE_BLOCKS = 2
SLC_CHUNK = 4 * SLC_BLOCK

MXU_DTYPE = jnp.bfloat16
VMEM_LIMIT = 48 * 1024 * 1024
F32 = jnp.float32
HI = lax.Precision.HIGHEST


def _dot(a, b):
    return jnp.dot(a.astype(MXU_DTYPE), b.astype(MXU_DTYPE), preferred_element_type=F32)


def _dot_nt(a, b, precision=None):
    return lax.dot_general(a, b, (((1,), (1,)), ((), ())), precision=precision,
                           preferred_element_type=F32)


def _dot_hi(a, b):
    return jnp.dot(a, b, precision=HI, preferred_element_type=F32)


def _iota(shape, dim):
    return lax.broadcasted_iota(jnp.int32, shape, dim)


def _div_pow2(x, n):
    assert n & (n - 1) == 0
    return x >> (n.bit_length() - 1)


def _round_up(n, m):
    return -(-n // m) * m


def _params(*sem):
    return pltpu.CompilerParams(dimension_semantics=sem, vmem_limit_bytes=VMEM_LIMIT)


def _rms(x, g):
    return x * lax.rsqrt(jnp.mean(x * x, axis=-1, keepdims=True) + EPS) * g


def _ffn_kernel(x_ref, g_ref, wg_ref, wu_ref, wd_ref, gf_ref, o_ref, h_ref, wgb, wub, wdb, *, nj, final):
    step = pl.program_id(0)

    def start():
        h_ref[...] = _rms(x_ref[...], g_ref[...]).astype(h_ref.dtype)

    def chunk(wg, wu, wd):
        h = h_ref[...]
        a = _dot(h, wg)
        b = _dot(h, wu)
        return _dot(a / (1.0 + jnp.exp(-a)) * b, wd)

    def finish():
        y = x_ref[...] + 0.5 * o_ref[...]
        o_ref[...] = _rms(y, gf_ref[...]) if final else y

    @pl.when(step < nj)
    def _():
        wgb[step] = wg_ref[...].astype(wgb.dtype)
        wub[step] = wu_ref[...].astype(wub.dtype)
        wdb[step] = wd_ref[...].astype(wdb.dtype)

        @pl.when(step == 0)
        def _():
            start()
            o_ref[...] = chunk(wgb[0], wub[0], wdb[0])

        @pl.when(step > 0)
        def _():
            o_ref[...] += chunk(wgb[step], wub[step], wdb[step])

        @pl.when(step == nj - 1)
        def _():
            finish()

    @pl.when(step >= nj)
    def _():
        start()
        o_ref[...] = chunk(wgb[0], wub[0], wdb[0])

        def body(j, carry):
            o_ref[...] += chunk(wgb[j], wub[j], wdb[j])
            return carry

        lax.fori_loop(1, nj, body, 0, unroll=2)
        finish()


def _ffn(x, g, wg, wu, wd, l, tm, tf, final_g, final):
    T, D = x.shape
    F = wg.shape[2]
    nj = F // tf
    tile = lambda s: (jnp.maximum(s - (nj - 1), 0), 0)
    col = lambda s: jnp.minimum(s, nj - 1)
    return pl.pallas_call(
        functools.partial(_ffn_kernel, nj=nj, final=final),
        grid=(nj + T // tm - 1,),
        in_specs=[
            pl.BlockSpec((tm, D), tile),
            pl.BlockSpec((None, 1, D), lambda s: (l, 0, 0)),
            pl.BlockSpec((None, D, tf), lambda s: (l, 0, col(s))),
            pl.BlockSpec((None, D, tf), lambda s: (l, 0, col(s))),
            pl.BlockSpec((None, tf, D), lambda s: (l, col(s), 0)),
            pl.BlockSpec((1, D), lambda s: (0, 0)),
        ],
        out_specs=pl.BlockSpec((tm, D), tile),
        out_shape=jax.ShapeDtypeStruct((T, D), F32),
        scratch_shapes=[pltpu.VMEM((tm, D), MXU_DTYPE), pltpu.VMEM((nj, D, tf), MXU_DTYPE),
                        pltpu.VMEM((nj, D, tf), MXU_DTYPE), pltpu.VMEM((nj, tf, D), MXU_DTYPE)],
        compiler_params=_params("arbitrary"),
        name="ffn",
    )(x, g, wg, wu, wd, final_g)


def _inproj_kernel(x_ref, g_ref, w_ref, o_ref, ck_ref, cv_ref, wb_ref, kc_tile, vc_tile):
    @pl.when(pl.program_id(0) == 0)
    def _():
        wb_ref[...] = jnp.zeros_like(wb_ref)
        wb_ref[:, 0:w_ref.shape[1]] = w_ref[...].astype(wb_ref.dtype)

    y = _dot(_rms(x_ref[...], g_ref[...]), wb_ref[...])
    o_ref[...] = y
    kc_tile[...] = y[:, COL_KC * LANES:(COL_KC + 1) * LANES]
    vc_tile[...] = y[:, COL_VC * LANES:(COL_VC + 1) * LANES]
    rows = o_ref.shape[0] // CMP_STRIDE
    for t in range(CMP_STRIDE):
        for ref, tile in ((ck_ref, kc_tile), (cv_ref, vc_tile)):
            tok = tile[pl.ds(t, rows, stride=CMP_STRIDE), :]
            for grp in range(NSA_KV_HEADS):
                ref[grp, :, t * HEAD_DIM:(t + 1) * HEAD_DIM] = tok[:, grp * HEAD_DIM:(grp + 1) * HEAD_DIM]


def _inproj(x, g, w, l, tm):
    T, D = x.shape
    W = w.shape[2]
    N = _round_up(W, LANES * 4)
    slab = pl.BlockSpec((NSA_KV_HEADS, tm // CMP_STRIDE, CMP_STRIDE * HEAD_DIM), lambda i: (0, i, 0))
    slab_shape = jax.ShapeDtypeStruct((NSA_KV_HEADS, T // CMP_STRIDE, CMP_STRIDE * HEAD_DIM), F32)
    return pl.pallas_call(
        _inproj_kernel,
        grid=(T // tm,),
        in_specs=[
            pl.BlockSpec((tm, D), lambda i: (i, 0)),
            pl.BlockSpec((None, 1, D), lambda i: (l, 0, 0)),
            pl.BlockSpec((None, D, W), lambda i: (l, 0, 0), pipeline_mode=pl.Buffered(1)),
        ],
        out_specs=(pl.BlockSpec((tm, N), lambda i: (i, 0)), slab, slab),
        out_shape=(jax.ShapeDtypeStruct((T, N), F32), slab_shape, slab_shape),
        scratch_shapes=[pltpu.VMEM((D, N), MXU_DTYPE), pltpu.VMEM((tm, LANES), F32), pltpu.VMEM((tm, LANES), F32)],
        compiler_params=_params("arbitrary"),
        name="inproj",
    )(x, g, w)


POS_SPLIT = 256
EXT_PAD_LANE = 4
EXT_MASK_LANE = 2 * SUBLANES


def _key_ext(pos, lane, mask_block=0, pad=None):
    ext = jnp.where(lane == 0, _div_pow2(pos, POS_SPLIT),
                    jnp.where(lane == 1, pos & (POS_SPLIT - 1),
                              jnp.where((lane == 2) | (lane == 3), 1, 0)))
    if mask_block:
        ext = jnp.where(lane - EXT_MASK_LANE == _div_pow2(pos, mask_block), 1, ext)
    if pad is not None:
        ext = jnp.where(pad, jnp.where(lane == EXT_PAD_LANE, 1, 0), ext)
    return ext.astype(F32)


def _query_ext(slope, t):
    r = _iota((SUBLANES, t.shape[1]), 0)
    big = slope * float(POS_SPLIT)
    hi = _div_pow2(t, POS_SPLIT).astype(F32)
    lo = (t & (POS_SPLIT - 1)).astype(F32)
    return jnp.where(r == 0, big, jnp.where(r == 1, slope, jnp.where(r == 2, -(big * hi),
                     jnp.where(r == 3, -(slope * lo), jnp.where(r == EXT_PAD_LANE, NEG, 0.0)))))


def _softmax_cols(s):
    p = jnp.exp(s - jnp.max(s, axis=0, keepdims=True))
    return p, 1.0 / jnp.sum(p, axis=0, keepdims=True)


def _rank_before(score, n, idx):
    rank = jnp.zeros(score.shape, jnp.int32)
    for m in range(n):
        row = score[m:m + 1, :]
        better = (row > score) | ((row == score) & (m < idx))
        rank = rank + better.astype(jnp.int32)
    return rank


V_ROWS = HEAD_DIM + 2 * SUBLANES


def _with_ones(vt):
    pad = (_iota((V_ROWS - HEAD_DIM, vt.shape[1]), 0) == 0).astype(vt.dtype)
    return jnp.concatenate([vt, pad], axis=0)


def _moba_kernel(slopes_ref, q_ref, k_ref, v_ref, o_ref,
                 kmean_ref, kb_ref, vt_ref, row_ref, *, nb, kk, tq):
    hp = pl.program_id(1)
    ti = pl.program_id(2)
    blk = MOBA_BLOCK
    per = tq // blk
    cols = HEADS_PER_TILE * tq

    @pl.when(ti == 0)
    def _():
        kmean_ref[...] = jnp.zeros_like(kmean_ref)
        for n in range(nb):
            rows = slice(n * blk, (n + 1) * blk)
            kmean_ref[n:n + 1, :] = jnp.mean(k_ref[0, rows, :], axis=0, keepdims=True)
            vt = v_ref[0, rows, :].T.astype(vt_ref.dtype)
            for h in range(HEADS_PER_TILE):
                vt_ref[h, :, rows] = _with_ones(vt[h * HEAD_DIM:(h + 1) * HEAD_DIM])
        kb_ref[...] = k_ref[0].astype(kb_ref.dtype)

    q0 = ti * tq
    lo = _iota((tq, LANES), 1) < HEAD_DIM
    q = q_ref[0]
    qs = jnp.concatenate([jnp.where(lo, q, 0.0), jnp.where(lo, 0.0, q)], axis=0)
    col1 = _iota((1, cols), 1)
    slope = jnp.where(col1 < tq, slopes_ref[HEADS_PER_TILE * hp], slopes_ref[HEADS_PER_TILE * hp + 1])
    i1 = col1 & (tq - 1)
    own = ti * per + _div_pow2(i1, blk)

    gate = _dot_nt(kmean_ref[...], qs, precision=HI)
    n_idx = _iota(gate.shape, 0)
    cand = n_idx < own
    gm = jnp.where(cand, gate, NEG)
    keep = (cand & (_rank_before(gm, nb - 1, n_idx) < kk)) | (n_idx == own)
    row_ref[...] = slope * (n_idx * blk - q0).astype(F32) + jnp.where(keep, 0.0, NEG)

    qsb = (qs * SCALE).astype(MXU_DTYPE)
    rel = _iota((blk, cols), 0) - i1
    bias = slope * rel.astype(F32)

    def attend(t):
        first = per * t
        nkeys = (first + per) * blk
        s = _dot_nt(kb_ref[0:nkeys, :], qsb)
        pieces = []
        for n in range(first + per):
            piece = s[n * blk:(n + 1) * blk] + bias + row_ref[n:n + 1, :]
            if n >= first:
                piece = jnp.where(rel + (n - first) * blk <= 0, piece, NEG)
            pieces.append(piece)
        s = jnp.concatenate(pieces, axis=0)
        pb = jnp.exp((s - jnp.max(s, axis=0, keepdims=True)).astype(MXU_DTYPE))
        halves = []
        for h in range(HEADS_PER_TILE):
            acc = jnp.dot(vt_ref[h, :, 0:nkeys], pb[:, h * tq:(h + 1) * tq], preferred_element_type=F32)
            halves.append(acc[0:HEAD_DIM] * (1.0 / acc[HEAD_DIM:HEAD_DIM + 1]))
        o_ref[0] = jnp.concatenate(halves, axis=0).T

    for t in range(nb // per):
        @pl.when(ti == t)
        def _(t=t):
            attend(t)


def _moba(slopes, proj):
    B, S, _ = proj.shape
    tq = MOBA_TILE_BLOCKS * MOBA_BLOCK
    nb = S // MOBA_BLOCK
    assert S % tq == 0
    kk = max(1, min(MOBA_TOPK, nb - 1))
    n_tiles = MOBA_HEADS // HEADS_PER_TILE
    kern = functools.partial(_moba_kernel, nb=nb, kk=kk, tq=tq)
    return pl.pallas_call(
        kern,
        grid=(B, n_tiles, S // tq),
        in_specs=[
            pl.BlockSpec(memory_space=pltpu.SMEM),
            pl.BlockSpec((1, tq, LANES), lambda b, h, i: (b, i, COL_QM + h)),
            pl.BlockSpec((1, S, LANES), lambda b, h, i: (b, 0, COL_KM + h)),
            pl.BlockSpec((1, S, LANES), lambda b, h, i: (b, 0, COL_VM + h)),
        ],
        out_specs=pl.BlockSpec((1, tq, LANES), lambda b, h, i: (b, i, h)),
        out_shape=jax.ShapeDtypeStruct((B, S, MOBA_HEADS * HEAD_DIM), F32),
        scratch_shapes=[
            pltpu.VMEM((_round_up(nb, SUBLANES), LANES), F32),
            pltpu.VMEM((S, LANES), MXU_DTYPE),
            pltpu.VMEM((HEADS_PER_TILE, V_ROWS, S), MXU_DTYPE),
            pltpu.VMEM((_round_up(nb, SUBLANES), HEADS_PER_TILE * tq), F32),
        ],
        compiler_params=_params("parallel", "parallel", "arbitrary"),
        name="moba",
    )(slopes, proj, proj, proj)


def _gelu_tanh(x):
    return x * (0.5 * (1.0 + jnp.tanh(0.7978845608028654 * (x + 0.044715 * (x * x * x)))))


def _compress_kernel(xk_ref, xv_ref, pk_ref, pv_ref, w1k_ref, w1v_ref, w2k_ref, w2v_ref, kc_ref, vc_ref):
    g = pl.program_id(0)
    half = (CMP_LEN // 2) * HEAD_DIM
    for x_ref, p_ref, w1_ref, w2_ref, o_ref in ((xk_ref, pk_ref, w1k_ref, w2k_ref, kc_ref),
                                               (xv_ref, pv_ref, w1v_ref, w2v_ref, vc_ref)):
        B, nr, W = x_ref.shape
        x = x_ref[...].reshape(B * nr, W)
        top = _dot_hi(x + p_ref[:, 0:half], w1_ref[0:half, :])
        bot = _dot_hi(x + p_ref[:, half:2 * half], w1_ref[half:2 * half, :])
        hid = top + pltpu.roll(bot, B * nr - 1, 0)
        out = _dot_hi(_gelu_tanh(hid), w2_ref[...]).reshape(B, nr, LANES)

        @pl.when(g == 0)
        def _(o_ref=o_ref, out=out):
            o_ref[...] = out

        @pl.when(g > 0)
        def _(o_ref=o_ref, out=out):
            o_ref[...] += out


def _compress(xk, xv, pk, pv, w1k, w1v, w2k, w2v, l):
    G, B, nr, W = xk.shape
    H = w1k.shape[2]
    layer = lambda *shape: pl.BlockSpec((None,) + shape, lambda g: (l,) + (0,) * len(shape))
    xspec = pl.BlockSpec((None, B, nr, W), lambda g: (g, 0, 0, 0))
    w2spec = pl.BlockSpec((None, H, LANES), lambda g: (g, 0, 0))
    ospec = pl.BlockSpec((B, nr, LANES), lambda g: (0, 0, 0))
    oshape = jax.ShapeDtypeStruct((B, nr, LANES), F32)
    return pl.pallas_call(
        _compress_kernel,
        grid=(G,),
        in_specs=[xspec, xspec, layer(1, 2 * W), layer(1, 2 * W), layer(2 * W, H), layer(2 * W, H),
                  w2spec, w2spec],
        out_specs=(ospec, ospec),
        out_shape=(oshape, oshape),
        compiler_params=_params("arbitrary"),
        name="nsa_compress",
    )(xk, xv, pk, pv, w1k, w1v, w2k, w2v)


def _nsa_kernel(slopes_ref, qa_ref, qb_ref, g_ref, kc_ref, vc_ref, ks_ref, vs_ref, kw_ref, vw_ref,
                o_ref, kcx, vct, ksx, vst, kwx, vwt, qx_ref, gt_ref, ocmp_ref, owin_ref,
                *, S, tq, n_cmp, n_slc, k_top):
    g = pl.program_id(1)
    qi = pl.program_id(2)
    q0 = qi * tq
    cols = NSA_GROUP * tq
    sblk = SLC_CHUNK
    nr = kcx.shape[0]
    nj = _round_up(n_slc, 2 * SUBLANES)
    n_back = WINDOW // tq
    ext0 = HEAD_DIM

    @pl.when(qi == 0)
    def _():
        def both_halves(x):
            mine = (_iota(x.shape, 1) >= HEAD_DIM) == (g == 1)
            return jnp.where(mine, x, pltpu.roll(x, HEAD_DIM, 1))

        def with_ext(x, pos, mask_block=0):
            lane = _iota(x.shape, 1)
            return jnp.where(lane < HEAD_DIM, both_halves(x), _key_ext(pos, lane - ext0, mask_block)).astype(MXU_DTYPE)

        def transposed(x):
            return both_halves(x).T[0:HEAD_DIM].astype(MXU_DTYPE)

        qx_ref[...] = jnp.zeros_like(qx_ref)
        kcx[...] = with_ext(kc_ref[0], _iota((nr, LANES), 0) * CMP_STRIDE + (CMP_LEN - 1))
        vct[...] = transposed(vc_ref[0])
        lane = _iota((WINDOW, LANES), 1)
        kwx[0:WINDOW, :] = _key_ext(lane * 0, lane - ext0, pad=lane >= 0).astype(MXU_DTYPE)
        for c in range(n_back):
            vwt[c] = jnp.zeros(vwt.shape[1:], vwt.dtype)
        for c in range(S // sblk):
            rows = slice(c * sblk, (c + 1) * sblk)
            pos = c * sblk + _iota((sblk, LANES), 0)
            ksx[rows, :] = with_ext(ks_ref[0, rows, :], pos, SLC_BLOCK)
            kwx[WINDOW + c * sblk:WINDOW + (c + 1) * sblk, :] = with_ext(kw_ref[0, rows, :], pos)
            vst[:, rows] = transposed(vs_ref[0, rows, :])
        for c in range(S // tq):
            vwt[n_back + c] = transposed(vw_ref[0, c * tq:(c + 1) * tq, :])

    qat = qa_ref[0].T
    qbt = qb_ref[0].T
    qt = jnp.concatenate([qat[0:HEAD_DIM], qat[HEAD_DIM:LANES], qbt[0:HEAD_DIM], qbt[HEAD_DIM:LANES]], axis=1)
    col1 = _iota((1, cols), 1)
    h0 = NSA_GROUP * g
    slope = jnp.where(col1 < tq, slopes_ref[h0],
                      jnp.where(col1 < 2 * tq, slopes_ref[h0 + 1],
                                jnp.where(col1 < 3 * tq, slopes_ref[h0 + 2], slopes_ref[h0 + 3])))
    i1 = col1 & (tq - 1)
    t1 = q0 + i1
    qx_ref[0:HEAD_DIM, :] = (qt * SCALE).astype(qx_ref.dtype)
    qx_ref[ext0:ext0 + EXT_MASK_LANE, :] = jnp.concatenate(
        [_query_ext(slope, t1), jnp.zeros((EXT_MASK_LANE - SUBLANES, cols), F32)], axis=0).astype(qx_ref.dtype)
    qx = qx_ref[...]

    relw = _iota((tq, cols), 0) - i1
    s = jnp.dot(kwx[pl.ds(pl.multiple_of(q0, tq), WINDOW + tq), :], qx, preferred_element_type=F32)
    s = jnp.concatenate([jnp.where(relw > 0, s[0:tq], NEG), s[tq:WINDOW],
                         jnp.where(relw <= 0, s[WINDOW:WINDOW + tq], NEG)], axis=0)
    p, inv = _softmax_cols(s)
    pb = p.astype(MXU_DTYPE)
    acc = _dot(vwt[qi], pb[0:tq])
    for e in range(1, n_back + 1):
        acc = acc + _dot(vwt[qi + e], pb[e * tq:(e + 1) * tq])
    owin_ref[...] = acc * inv

    nidx = _iota((nr, cols), 0)
    maskc = (nidx * CMP_STRIDE + (CMP_LEN - 1) <= t1) & (nidx < n_cmp)
    s = jnp.where(maskc, jnp.dot(kcx[...], qx, preferred_element_type=F32), NEG)
    p = jnp.where(maskc, jnp.exp(s - jnp.max(s, axis=0, keepdims=True)), 0.0)
    p = p / jnp.maximum(jnp.sum(p, axis=0, keepdims=True), 1e-30)
    ocmp_ref[...] = _dot(vct[...], p)

    psum = p[:, 0:tq] + p[:, tq:2 * tq] + p[:, 2 * tq:3 * tq] + p[:, 3 * tq:4 * tq]
    jj = _iota((nj, nr), 0) * SLC_BLOCK
    nn = _iota((nj, nr), 1) * CMP_STRIDE
    overlap = ((nn < jj + SLC_BLOCK) & (nn + CMP_LEN > jj)
               & (nn < n_cmp * CMP_STRIDE) & (jj < n_slc * SLC_BLOCK)).astype(F32)
    imp = _dot_hi(overlap, psum)
    jl = _iota((nj, tq), 0)
    tb = _div_pow2(q0 + _iota((nj, tq), 1), SLC_BLOCK)
    cand = jl <= tb
    forced = (jl == 0) | (jl == tb) | (jl == tb - 1)
    score = jnp.where(cand, jnp.where(forced, SLC_FORCE, imp), NEG)
    keep = cand & (_rank_before(score, n_slc, jl) < k_top)
    mask0 = ext0 + EXT_MASK_LANE
    qx_ref[mask0:mask0 + nj, :] = jnp.concatenate([jnp.where(keep, 0.0, NEG)] * NSA_GROUP, axis=1).astype(qx_ref.dtype)

    z = g_ref[0].T
    gt_ref[...] = 1.0 / (1.0 + jnp.exp(-z))

    def combine(oslc):
        ocmp = ocmp_ref[...]
        owin = owin_ref[...]
        heads = []
        for r in range(NSA_GROUP):
            sl = slice(r * tq, (r + 1) * tq)
            c = NSA_BRANCHES * (NSA_GROUP * g + r)
            heads.append(gt_ref[pl.ds(c, 1), :] * ocmp[:, sl] + gt_ref[pl.ds(c + 1, 1), :] * oslc[:, sl]
                         + gt_ref[pl.ds(c + 2, 1), :] * owin[:, sl])
        o_ref[0, :, 0:LANES] = jnp.concatenate(heads[0:2], axis=0).T
        o_ref[0, :, LANES:2 * LANES] = jnp.concatenate(heads[2:4], axis=0).T

    def slc_attend(c):
        nkeys = (c + 1) * sblk
        s = jnp.dot(ksx[0:nkeys, :], qx_ref[...], preferred_element_type=F32)
        causal = _iota((sblk, cols), 0) + (c * sblk - q0) <= i1
        diag = jnp.where(causal, s[c * sblk:nkeys], NEG)
        s = jnp.concatenate([s[0:c * sblk], diag], axis=0) if c else diag
        p, inv = _softmax_cols(s)
        combine(_dot(vst[:, 0:nkeys], p) * inv)

    dc = _div_pow2(q0, sblk)
    for c in range(S // sblk):
        @pl.when(dc == c)
        def _(c=c):
            slc_attend(c)


def _nsa(slopes, proj, kc, vc):
    B, S, _ = proj.shape
    tq = NSA_QTILE
    n_cmp = (S - CMP_LEN) // CMP_STRIDE + 1
    n_slc = S // SLC_BLOCK
    k_top = min(SLC_TOPK, n_slc)
    nr = kc.shape[1]
    cols = NSA_GROUP * tq
    assert HEAD_DIM + EXT_MASK_LANE + _round_up(n_slc, 2 * SUBLANES) <= LANES and S // POS_SPLIT <= POS_SPLIT
    assert SLC_CHUNK % tq == 0 and WINDOW % tq == 0 and S % SLC_CHUNK == 0
    kern = functools.partial(_nsa_kernel, S=S, tq=tq, n_cmp=n_cmp, n_slc=n_slc, k_top=k_top)
    qtiles = NSA_GROUP // HEADS_PER_TILE

    def seq(col):
        return pl.BlockSpec((1, S, LANES), lambda b, g, i: (b, 0, col))

    small = pl.BlockSpec((1, nr, LANES), lambda b, g, i: (b, 0, 0))
    return pl.pallas_call(
        kern,
        grid=(B, NSA_KV_HEADS, S // tq),
        in_specs=[
            pl.BlockSpec(memory_space=pltpu.SMEM),
            pl.BlockSpec((1, tq, LANES), lambda b, g, i: (b, i, COL_QN + qtiles * g)),
            pl.BlockSpec((1, tq, LANES), lambda b, g, i: (b, i, COL_QN + qtiles * g + 1)),
            pl.BlockSpec((1, tq, LANES), lambda b, g, i: (b, i, COL_G)),
            small, small, seq(COL_KS), seq(COL_VS), seq(COL_KW), seq(COL_VW),
        ],
        out_specs=pl.BlockSpec((1, tq, qtiles * LANES), lambda b, g, i: (b, i, g)),
        out_shape=jax.ShapeDtypeStruct((B, S, NSA_HEADS * HEAD_DIM), F32),
        scratch_shapes=[
            pltpu.VMEM((nr, LANES), MXU_DTYPE), pltpu.VMEM((HEAD_DIM, nr), MXU_DTYPE),
            pltpu.VMEM((S, LANES), MXU_DTYPE), pltpu.VMEM((HEAD_DIM, S), MXU_DTYPE),
            pltpu.VMEM((WINDOW + S, LANES), MXU_DTYPE),
            pltpu.VMEM((WINDOW // tq + S // tq, HEAD_DIM, tq), MXU_DTYPE),
            pltpu.VMEM((LANES, cols), MXU_DTYPE),
            pltpu.VMEM((LANES, tq), F32),
            pltpu.VMEM((HEAD_DIM, cols), F32), pltpu.VMEM((HEAD_DIM, cols), F32),
        ],
        compiler_params=_params("parallel", "parallel", "arbitrary"),
        name="nsa",
    )(slopes, proj, proj, proj, kc, vc, proj, proj, proj, proj)


def _outproj_kernel(om_ref, on_ref, x_ref, gm_ref, gn_ref, wm_ref, wn_ref, o_ref):
    y = _dot(_rms(om_ref[...], gm_ref[...]), wm_ref[...]) + _dot(_rms(on_ref[...], gn_ref[...]), wn_ref[...])
    o_ref[...] = x_ref[...] + y


def _outproj(om, on, x, gm, gn, w, l, tm):
    T, D = x.shape
    Wm, Wn = om.shape[1], on.shape[1]
    assert Wm == Wn and w.shape[1] == Wm + Wn
    return pl.pallas_call(
        _outproj_kernel,
        grid=(T // tm,),
        in_specs=[
            pl.BlockSpec((tm, Wm), lambda i: (i, 0)),
            pl.BlockSpec((tm, Wn), lambda i: (i, 0)),
            pl.BlockSpec((tm, D), lambda i: (i, 0)),
            pl.BlockSpec((None, 1, Wm), lambda i: (l, 0, 0)),
            pl.BlockSpec((None, 1, Wn), lambda i: (l, 0, 0)),
            pl.BlockSpec((None, Wm, D), lambda i: (l, 0, 0)),
            pl.BlockSpec((None, Wn, D), lambda i: (l, 1, 0)),
        ],
        out_specs=pl.BlockSpec((tm, D), lambda i: (i, 0)),
        out_shape=jax.ShapeDtypeStruct((T, D), F32),
        compiler_params=_params("parallel"),
        name="outproj",
    )(om, on, x, gm, gn, w, w)


def _alibi_slopes(n):
    slopes = 2.0 ** (-8.0 * np.arange(1, n + 1) / n)
    assert np.all(np.log2(slopes) == np.round(np.log2(slopes)))
    return jnp.asarray(slopes, dtype=F32)


def _token_tile(T, want):
    return want if T % want == 0 else T


def _mixer(x, B, S, l, mix_norm, w_in, pos_k, k_w1, k_w2, pos_v, v_w1, v_w2, moba_norm, nsa_norm, w_out):
    T, D = x.shape
    proj, ck, cv = _inproj(x, mix_norm, w_in, l, _token_tile(T, 512))
    proj = proj.reshape(B, S, IN_PAD)

    o_m = _moba(_alibi_slopes(MOBA_HEADS), proj)

    def windows(slab):
        return slab.reshape(NSA_KV_HEADS, B, S // CMP_STRIDE, CMP_STRIDE * HEAD_DIM)

    def placed(w2):
        return jnp.stack([jnp.pad(w2, ((0, 0), (g * HEAD_DIM, LANES - (g + 1) * HEAD_DIM)))
                          for g in range(NSA_KV_HEADS)])

    kc, vc = _compress(windows(ck), windows(cv), pos_k, pos_v, k_w1, v_w1, placed(k_w2), placed(v_w2), l)
    o_n = _nsa(_alibi_slopes(NSA_HEADS), proj, kc, vc)

    return _outproj(o_m.reshape(T, -1), o_n.reshape(T, -1), x, moba_norm, nsa_norm, w_out, l,
                    _token_tile(T, 512))


def _swiglu_step(x, l, norm, w_gate, w_up, w_down, final_g, final=False):
    T = x.shape[0]
    return _ffn(x, norm, w_gate, w_up, w_down, l, _token_tile(T, 1024), 256, final_g, final)


def _rows(p):
    return p.reshape(p.shape[0], 1, -1)


@jax.jit
def kernel(x, ffa_norm, ffa_w_gate, ffa_w_up, ffa_w_down, mix_norm, w_in, cmp_pos_k, cmp_k_w1, cmp_k_w2,
           cmp_pos_v, cmp_v_w1, cmp_v_w2, moba_out_norm, nsa_out_norm, w_out, ffb_norm, ffb_w_gate,
           ffb_w_up, ffb_w_down, final_norm):
    B, S, D = x.shape
    assert S % MOBA_BLOCK == 0 and S >= WINDOW and w_in.shape[-1] == IN_WIDTH
    h = x.reshape(B * S, D)
    ffa_norm, ffb_norm, mix_norm = _rows(ffa_norm), _rows(ffb_norm), _rows(mix_norm)
    moba_out_norm, nsa_out_norm = _rows(moba_out_norm), _rows(nsa_out_norm)
    cmp_pos_k, cmp_pos_v = _rows(cmp_pos_k), _rows(cmp_pos_v)
    depth = ffa_norm.shape[0]
    final_g = final_norm[None]
    for l in range(depth):
        h = _swiglu_step(h, l, ffa_norm, ffa_w_gate, ffa_w_up, ffa_w_down, final_g)
        h = _mixer(h, B, S, l, mix_norm, w_in, cmp_pos_k, cmp_k_w1, cmp_k_w2[l],
                   cmp_pos_v, cmp_v_w1, cmp_v_w2[l], moba_out_norm, nsa_out_norm, w_out)
        h = _swiglu_step(h, l, ffb_norm, ffb_w_gate, ffb_w_up, ffb_w_down, final_g, final=l == depth - 1)
    return h.reshape(B, S, D)
```

```python
import functools

import numpy as np
import jax
import jax.numpy as jnp
from jax import lax
from jax.experimental import pallas as pl
from jax.experimental.pallas import tpu as pltpu

HEAD_DIM = 64
MOBA_HEADS = 8
NSA_HEADS = 8
NSA_KV_HEADS = 2
NSA_GROUP = NSA_HEADS // NSA_KV_HEADS
NSA_BRANCHES = 3
MOBA_BLOCK = 256
MOBA_TOPK = 3
CMP_LEN = 32
CMP_STRIDE = 16
SLC_BLOCK = 64
SLC_TOPK = 16
WINDOW = 512
NEG = -1e30
SLC_FORCE = 1e4
EPS = 1e-6
SCALE = HEAD_DIM ** -0.5

LANES = 128
SUBLANES = 8
HEADS_PER_TILE = LANES // HEAD_DIM
COL_QM, COL_KM, COL_VM, COL_QN = 0, 4, 8, 12
COL_KC, COL_VC, COL_KS, COL_VS, COL_KW, COL_VW, COL_G = 16, 17, 18, 19, 20, 21, 22
IN_TILES = 24
IN_PAD = IN_TILES * LANES
IN_WIDTH = 2840
NSA_QTILE = 256
MOBA_TILE_BLOCKS = 2
FFN_UNROLL = 5
FFN_TOKENS, FFN_CHUNK = 1024, 256
PROJ_TOKENS = 512
OUT_TOKENS = 1024
SLC_CHUNK = 4 * SLC_BLOCK

MXU_DTYPE = jnp.bfloat16
VMEM_LIMIT = 48 * 1024 * 1024
F32 = jnp.float32
HI = lax.Precision.HIGHEST


def _dot(a, b):
    return jnp.dot(a.astype(MXU_DTYPE), b.astype(MXU_DTYPE), preferred_element_type=F32)


def _dot_nt(a, b, precision=None):
    return lax.dot_general(a, b, (((1,), (1,)), ((), ())), precision=precision,
                           preferred_element_type=F32)


def _dot_hi(a, b):
    return jnp.dot(a, b, precision=HI, preferred_element_type=F32)


def _iota(shape, dim):
    return lax.broadcasted_iota(jnp.int32, shape, dim)


def _div_pow2(x, n):
    assert n & (n - 1) == 0
    return x >> (n.bit_length() - 1)


def _round_up(n, m):
    return -(-n // m) * m


def _params(*sem):
    return pltpu.CompilerParams(dimension_semantics=sem, vmem_limit_bytes=VMEM_LIMIT)


def _rms(x, g):
    return x * lax.rsqrt(jnp.mean(x * x, axis=-1, keepdims=True) + EPS) * g


def _ffn_kernel(x_ref, g_ref, wg_ref, wu_ref, wd_ref, gf_ref, o_ref, h_ref, wgb, wub, wdb, *, nj, final):
    step = pl.program_id(0)

    def start():
        h_ref[...] = _rms(x_ref[...], g_ref[...]).astype(h_ref.dtype)

    def chunk(wg, wu, wd):
        h = h_ref[...]
        a = _dot(h, wg)
        b = _dot(h, wu)
        return _dot(a / (1.0 + jnp.exp(-a)) * b, wd)

    def finish():
        y = x_ref[...] + 0.5 * o_ref[...]
        o_ref[...] = _rms(y, gf_ref[...]) if final else y

    @pl.when(step < nj)
    def _():
        wgb[step] = wg_ref[...].astype(wgb.dtype)
        wub[step] = wu_ref[...].astype(wub.dtype)
        wdb[step] = wd_ref[...].astype(wdb.dtype)

        @pl.when(step == 0)
        def _():
            start()
            o_ref[...] = chunk(wgb[0], wub[0], wdb[0])

        @pl.when(step > 0)
        def _():
            o_ref[...] += chunk(wgb[step], wub[step], wdb[step])

        @pl.when(step == nj - 1)
        def _():
            finish()

    @pl.when(step >= nj)
    def _():
        start()
        o_ref[...] = chunk(wgb[0], wub[0], wdb[0])

        def body(j, carry):
            o_ref[...] += chunk(wgb[j], wub[j], wdb[j])
            return carry

        lax.fori_loop(1, nj, body, 0, unroll=FFN_UNROLL)
        finish()


def _ffn(x, g, wg, wu, wd, l, tm, tf, final_g, final):
    T, D = x.shape
    F = wg.shape[2]
    nj = F // tf
    tile = lambda s: (jnp.maximum(s - (nj - 1), 0), 0)
    col = lambda s: jnp.minimum(s, nj - 1)
    return pl.pallas_call(
        functools.partial(_ffn_kernel, nj=nj, final=final),
        grid=(nj + T // tm - 1,),
        in_specs=[
            pl.BlockSpec((tm, D), tile),
            pl.BlockSpec((None, 1, D), lambda s: (l, 0, 0)),
            pl.BlockSpec((None, D, tf), lambda s: (l, 0, col(s))),
            pl.BlockSpec((None, D, tf), lambda s: (l, 0, col(s))),
            pl.BlockSpec((None, tf, D), lambda s: (l, col(s), 0)),
            pl.BlockSpec((1, D), lambda s: (0, 0)),
        ],
        out_specs=pl.BlockSpec((tm, D), tile),
        out_shape=jax.ShapeDtypeStruct((T, D), F32),
        scratch_shapes=[pltpu.VMEM((tm, D), MXU_DTYPE), pltpu.VMEM((nj, D, tf), MXU_DTYPE),
                        pltpu.VMEM((nj, D, tf), MXU_DTYPE), pltpu.VMEM((nj, tf, D), MXU_DTYPE)],
        compiler_params=_params("arbitrary"),
        name="ffn",
    )(x, g, wg, wu, wd, final_g)


def _inproj_kernel(x_ref, g_ref, w_ref, o_ref, ck_ref, cv_ref, wb_ref, kc_tile, vc_tile):
    @pl.when(pl.program_id(0) == 0)
    def _():
        wb_ref[...] = jnp.zeros_like(wb_ref)
        wb_ref[:, 0:w_ref.shape[1]] = w_ref[...].astype(wb_ref.dtype)

    y = _dot(_rms(x_ref[...], g_ref[...]), wb_ref[...])
    o_ref[...] = y
    kc_tile[...] = y[:, COL_KC * LANES:(COL_KC + 1) * LANES]
    vc_tile[...] = y[:, COL_VC * LANES:(COL_VC + 1) * LANES]
    rows = o_ref.shape[0] // CMP_STRIDE
    for t in range(CMP_STRIDE):
        for ref, tile in ((ck_ref, kc_tile), (cv_ref, vc_tile)):
            tok = tile[pl.ds(t, rows, stride=CMP_STRIDE), :]
            for grp in range(NSA_KV_HEADS):
                ref[grp, :, t * HEAD_DIM:(t + 1) * HEAD_DIM] = tok[:, grp * HEAD_DIM:(grp + 1) * HEAD_DIM]


def _inproj(x, g, w, l, tm):
    T, D = x.shape
    W = w.shape[2]
    N = _round_up(W, LANES * 4)
    slab = pl.BlockSpec((NSA_KV_HEADS, tm // CMP_STRIDE, CMP_STRIDE * HEAD_DIM), lambda i: (0, i, 0))
    slab_shape = jax.ShapeDtypeStruct((NSA_KV_HEADS, T // CMP_STRIDE, CMP_STRIDE * HEAD_DIM), F32)
    return pl.pallas_call(
        _inproj_kernel,
        grid=(T // tm,),
        in_specs=[
            pl.BlockSpec((tm, D), lambda i: (i, 0)),
            pl.BlockSpec((None, 1, D), lambda i: (l, 0, 0)),
            pl.BlockSpec((None, D, W), lambda i: (l, 0, 0), pipeline_mode=pl.Buffered(1)),
        ],
        out_specs=(pl.BlockSpec((tm, N), lambda i: (i, 0)), slab, slab),
        out_shape=(jax.ShapeDtypeStruct((T, N), F32), slab_shape, slab_shape),
        scratch_shapes=[pltpu.VMEM((D, N), MXU_DTYPE), pltpu.VMEM((tm, LANES), F32), pltpu.VMEM((tm, LANES), F32)],
        compiler_params=_params("arbitrary"),
        name="inproj",
    )(x, g, w)


POS_SPLIT = 256
EXT_PAD_LANE = 4
EXT_MASK_LANE = 2 * SUBLANES


def _key_ext(pos, lane, mask_block=0, pad=None):
    ext = jnp.where(lane == 0, _div_pow2(pos, POS_SPLIT),
                    jnp.where(lane == 1, pos & (POS_SPLIT - 1),
                              jnp.where((lane == 2) | (lane == 3), 1, 0)))
    if mask_block:
        ext = jnp.where(lane - EXT_MASK_LANE == _div_pow2(pos, mask_block), 1, ext)
    if pad is not None:
        ext = jnp.where(pad, jnp.where(lane == EXT_PAD_LANE, 1, 0), ext)
    return ext.astype(F32)


def _query_ext(slope, t):
    r = _iota((SUBLANES, t.shape[1]), 0)
    big = slope * float(POS_SPLIT)
    hi = _div_pow2(t, POS_SPLIT).astype(F32)
    lo = (t & (POS_SPLIT - 1)).astype(F32)
    return jnp.where(r == 0, big, jnp.where(r == 1, slope, jnp.where(r == 2, -(big * hi),
                     jnp.where(r == 3, -(slope * lo), jnp.where(r == EXT_PAD_LANE, NEG, 0.0)))))


def _softmax_cols(s):
    p = jnp.exp(s - jnp.max(s, axis=0, keepdims=True))
    return p, 1.0 / jnp.sum(p, axis=0, keepdims=True)


def _rank_before(score, n, idx):
    rank = jnp.zeros(score.shape, jnp.int32)
    for m in range(n):
        row = score[m:m + 1, :]
        better = (row > score) | ((row == score) & (m < idx))
        rank = rank + better.astype(jnp.int32)
    return rank


V_ROWS = HEAD_DIM + 2 * SUBLANES


def _with_ones(vt):
    pad = (_iota((V_ROWS - HEAD_DIM, vt.shape[1]), 0) == 0).astype(vt.dtype)
    return jnp.concatenate([vt, pad], axis=0)


def _moba_kernel(slopes_ref, q_ref, k_ref, v_ref, o_ref,
                 kmean_ref, kb_ref, vt_ref, row_ref, *, nb, kk, tq):
    hp = pl.program_id(1)
    ti = pl.program_id(2)
    blk = MOBA_BLOCK
    per = tq // blk
    cols = HEADS_PER_TILE * tq

    @pl.when(ti == 0)
    def _():
        kmean_ref[...] = jnp.zeros_like(kmean_ref)
        for n in range(nb):
            rows = slice(n * blk, (n + 1) * blk)
            kmean_ref[n:n + 1, :] = jnp.mean(k_ref[0, rows, :], axis=0, keepdims=True)
            vt = v_ref[0, rows, :].T.astype(vt_ref.dtype)
            for h in range(HEADS_PER_TILE):
                vt_ref[h, :, rows] = _with_ones(vt[h * HEAD_DIM:(h + 1) * HEAD_DIM])
        kb_ref[...] = k_ref[0].astype(kb_ref.dtype)

    q0 = ti * tq
    lo = _iota((tq, LANES), 1) < HEAD_DIM
    q = q_ref[0]
    qs = jnp.concatenate([jnp.where(lo, q, 0.0), jnp.where(lo, 0.0, q)], axis=0)
    col1 = _iota((1, cols), 1)
    slope = jnp.where(col1 < tq, slopes_ref[HEADS_PER_TILE * hp], slopes_ref[HEADS_PER_TILE * hp + 1])
    i1 = col1 & (tq - 1)
    own = ti * per + _div_pow2(i1, blk)

    gate = _dot_nt(kmean_ref[...], qs, precision=HI)
    n_idx = _iota(gate.shape, 0)
    cand = n_idx < own
    gm = jnp.where(cand, gate, NEG)
    keep = (cand & (_rank_before(gm, nb - 1, n_idx) < kk)) | (n_idx == own)
    row_ref[...] = slope * (n_idx * blk - q0).astype(F32) + jnp.where(keep, 0.0, NEG)

    qsb = (qs * SCALE).astype(MXU_DTYPE)
    rel = _iota((blk, cols), 0) - i1
    bias = slope * rel.astype(F32)

    def attend(t):
        first = per * t
        nkeys = (first + per) * blk
        s = _dot_nt(kb_ref[0:nkeys, :], qsb)
        pieces = []
        for n in range(first + per):
            piece = s[n * blk:(n + 1) * blk] + bias + row_ref[n:n + 1, :]
            if n >= first:
                piece = jnp.where(rel + (n - first) * blk <= 0, piece, NEG)
            pieces.append(piece)
        s = jnp.concatenate(pieces, axis=0)
        pb = jnp.exp((s - jnp.max(s, axis=0, keepdims=True)).astype(MXU_DTYPE))
        halves = []
        for h in range(HEADS_PER_TILE):
            acc = jnp.dot(vt_ref[h, :, 0:nkeys], pb[:, h * tq:(h + 1) * tq], preferred_element_type=F32)
            halves.append(acc[0:HEAD_DIM] * (1.0 / acc[HEAD_DIM:HEAD_DIM + 1]))
        o_ref[0] = jnp.concatenate(halves, axis=0).T

    for t in range(nb // per):
        @pl.when(ti == t)
        def _(t=t):
            attend(t)


def _moba(slopes, proj):
    B, S, _ = proj.shape
    tq = MOBA_TILE_BLOCKS * MOBA_BLOCK
    nb = S // MOBA_BLOCK
    assert S % tq == 0
    kk = max(1, min(MOBA_TOPK, nb - 1))
    n_tiles = MOBA_HEADS // HEADS_PER_TILE
    kern = functools.partial(_moba_kernel, nb=nb, kk=kk, tq=tq)
    return pl.pallas_call(
        kern,
        grid=(B, n_tiles, S // tq),
        in_specs=[
            pl.BlockSpec(memory_space=pltpu.SMEM),
            pl.BlockSpec((1, tq, LANES), lambda b, h, i: (b, i, COL_QM + h)),
            pl.BlockSpec((1, S, LANES), lambda b, h, i: (b, 0, COL_KM + h)),
            pl.BlockSpec((1, S, LANES), lambda b, h, i: (b, 0, COL_VM + h)),
        ],
        out_specs=pl.BlockSpec((1, tq, LANES), lambda b, h, i: (b, i, h)),
        out_shape=jax.ShapeDtypeStruct((B, S, MOBA_HEADS * HEAD_DIM), F32),
        scratch_shapes=[
            pltpu.VMEM((_round_up(nb, SUBLANES), LANES), F32),
            pltpu.VMEM((S, LANES), MXU_DTYPE),
            pltpu.VMEM((HEADS_PER_TILE, V_ROWS, S), MXU_DTYPE),
            pltpu.VMEM((_round_up(nb, SUBLANES), HEADS_PER_TILE * tq), F32),
        ],
        compiler_params=_params("parallel", "parallel", "arbitrary"),
        name="moba",
    )(slopes, proj, proj, proj)


def _gelu_tanh(x):
    return x * (0.5 * (1.0 + jnp.tanh(0.7978845608028654 * (x + 0.044715 * (x * x * x)))))


def _compress_kernel(xk_ref, xv_ref, pk_ref, pv_ref, w1k_ref, w1v_ref, w2k_ref, w2v_ref, kc_ref, vc_ref):
    g = pl.program_id(0)
    half = (CMP_LEN // 2) * HEAD_DIM
    for x_ref, p_ref, w1_ref, w2_ref, o_ref in ((xk_ref, pk_ref, w1k_ref, w2k_ref, kc_ref),
                                               (xv_ref, pv_ref, w1v_ref, w2v_ref, vc_ref)):
        B, nr, W = x_ref.shape
        x = x_ref[...].reshape(B * nr, W)
        top = _dot_hi(x + p_ref[:, 0:half], w1_ref[0:half, :])
        bot = _dot_hi(x + p_ref[:, half:2 * half], w1_ref[half:2 * half, :])
        hid = top + pltpu.roll(bot, B * nr - 1, 0)
        out = _dot_hi(_gelu_tanh(hid), w2_ref[...]).reshape(B, nr, LANES)

        @pl.when(g == 0)
        def _(o_ref=o_ref, out=out):
            o_ref[...] = out

        @pl.when(g > 0)
        def _(o_ref=o_ref, out=out):
            o_ref[...] += out


def _compress(xk, xv, pk, pv, w1k, w1v, w2k, w2v, l):
    G, B, nr, W = xk.shape
    H = w1k.shape[2]
    layer = lambda *shape: pl.BlockSpec((None,) + shape, lambda g: (l,) + (0,) * len(shape))
    xspec = pl.BlockSpec((None, B, nr, W), lambda g: (g, 0, 0, 0))
    w2spec = pl.BlockSpec((None, H, LANES), lambda g: (g, 0, 0))
    ospec = pl.BlockSpec((B, nr, LANES), lambda g: (0, 0, 0))
    oshape = jax.ShapeDtypeStruct((B, nr, LANES), F32)
    return pl.pallas_call(
        _compress_kernel,
        grid=(G,),
        in_specs=[xspec, xspec, layer(1, 2 * W), layer(1, 2 * W), layer(2 * W, H), layer(2 * W, H),
                  w2spec, w2spec],
        out_specs=(ospec, ospec),
        out_shape=(oshape, oshape),
        compiler_params=_params("arbitrary"),
        name="nsa_compress",
    )(xk, xv, pk, pv, w1k, w1v, w2k, w2v)


def _nsa_kernel(slopes_ref, qa_ref, qb_ref, g_ref, kc_ref, vc_ref, ks_ref, vs_ref, kw_ref, vw_ref,
                o_ref, kcx, vct, ksx, vst, kwx, vwt, qx_ref, gt_ref, ocmp_ref, owin_ref,
                *, S, tq, n_cmp, n_slc, k_top):
    g = pl.program_id(1)
    qi = pl.program_id(2)
    q0 = qi * tq
    cols = NSA_GROUP * tq
    sblk = SLC_CHUNK
    nr = kcx.shape[0]
    nj = _round_up(n_slc, 2 * SUBLANES)
    n_back = WINDOW // tq
    ext0 = HEAD_DIM

    @pl.when(qi == 0)
    def _():
        def both_halves(x):
            mine = (_iota(x.shape, 1) >= HEAD_DIM) == (g == 1)
            return jnp.where(mine, x, pltpu.roll(x, HEAD_DIM, 1))

        def with_ext(x, pos, mask_block=0):
            lane = _iota(x.shape, 1)
            return jnp.where(lane < HEAD_DIM, both_halves(x), _key_ext(pos, lane - ext0, mask_block)).astype(MXU_DTYPE)

        def transposed(x):
            return both_halves(x).T[0:HEAD_DIM].astype(MXU_DTYPE)

        qx_ref[...] = jnp.zeros_like(qx_ref)
        kcx[...] = with_ext(kc_ref[0], _iota((nr, LANES), 0) * CMP_STRIDE + (CMP_LEN - 1))
        vct[...] = transposed(vc_ref[0])
        lane = _iota((WINDOW, LANES), 1)
        kwx[0:WINDOW, :] = _key_ext(lane * 0, lane - ext0, pad=lane >= 0).astype(MXU_DTYPE)
        for c in range(n_back):
            vwt[c] = jnp.zeros(vwt.shape[1:], vwt.dtype)
        for c in range(S // sblk):
            rows = slice(c * sblk, (c + 1) * sblk)
            pos = c * sblk + _iota((sblk, LANES), 0)
            ksx[rows, :] = with_ext(ks_ref[0, rows, :], pos, SLC_BLOCK)
            kwx[WINDOW + c * sblk:WINDOW + (c + 1) * sblk, :] = with_ext(kw_ref[0, rows, :], pos)
            vst[:, rows] = transposed(vs_ref[0, rows, :])
        for c in range(S // tq):
            vwt[n_back + c] = transposed(vw_ref[0, c * tq:(c + 1) * tq, :])

    qat = qa_ref[0].T
    qbt = qb_ref[0].T
    qt = jnp.concatenate([qat[0:HEAD_DIM], qat[HEAD_DIM:LANES], qbt[0:HEAD_DIM], qbt[HEAD_DIM:LANES]], axis=1)
    col1 = _iota((1, cols), 1)
    h0 = NSA_GROUP * g
    slope = jnp.where(col1 < tq, slopes_ref[h0],
                      jnp.where(col1 < 2 * tq, slopes_ref[h0 + 1],
                                jnp.where(col1 < 3 * tq, slopes_ref[h0 + 2], slopes_ref[h0 + 3])))
    i1 = col1 & (tq - 1)
    t1 = q0 + i1
    qx_ref[0:HEAD_DIM, :] = (qt * SCALE).astype(qx_ref.dtype)
    qx_ref[ext0:ext0 + EXT_MASK_LANE, :] = jnp.concatenate(
        [_query_ext(slope, t1), jnp.zeros((EXT_MASK_LANE - SUBLANES, cols), F32)], axis=0).astype(qx_ref.dtype)
    qx = qx_ref[...]

    relw = _iota((tq, cols), 0) - i1
    s = jnp.dot(kwx[pl.ds(pl.multiple_of(q0, tq), WINDOW + tq), :], qx, preferred_element_type=F32)
    s = jnp.concatenate([jnp.where(relw > 0, s[0:tq], NEG), s[tq:WINDOW],
                         jnp.where(relw <= 0, s[WINDOW:WINDOW + tq], NEG)], axis=0)
    p, inv = _softmax_cols(s)
    pb = p.astype(MXU_DTYPE)
    acc = _dot(vwt[qi], pb[0:tq])
    for e in range(1, n_back + 1):
        acc = acc + _dot(vwt[qi + e], pb[e * tq:(e + 1) * tq])
    owin_ref[...] = acc * inv

    nidx = _iota((nr, cols), 0)
    maskc = (nidx * CMP_STRIDE + (CMP_LEN - 1) <= t1) & (nidx < n_cmp)
    s = jnp.where(maskc, jnp.dot(kcx[...], qx, preferred_element_type=F32), NEG)
    p = jnp.where(maskc, jnp.exp(s - jnp.max(s, axis=0, keepdims=True)), 0.0)
    p = p / jnp.maximum(jnp.sum(p, axis=0, keepdims=True), 1e-30)
    ocmp_ref[...] = _dot(vct[...], p)

    psum = p[:, 0:tq] + p[:, tq:2 * tq] + p[:, 2 * tq:3 * tq] + p[:, 3 * tq:4 * tq]
    jj = _iota((nj, nr), 0) * SLC_BLOCK
    nn = _iota((nj, nr), 1) * CMP_STRIDE
    overlap = ((nn < jj + SLC_BLOCK) & (nn + CMP_LEN > jj)
               & (nn < n_cmp * CMP_STRIDE) & (jj < n_slc * SLC_BLOCK)).astype(F32)
    imp = _dot_hi(overlap, psum)
    jl = _iota((nj, tq), 0)
    tb = _div_pow2(q0 + _iota((nj, tq), 1), SLC_BLOCK)
    cand = jl <= tb
    forced = (jl == 0) | (jl == tb) | (jl == tb - 1)
    score = jnp.where(cand, jnp.where(forced, SLC_FORCE, imp), NEG)
    keep = cand & (_rank_before(score, n_slc, jl) < k_top)
    mask0 = ext0 + EXT_MASK_LANE
    qx_ref[mask0:mask0 + nj, :] = jnp.concatenate([jnp.where(keep, 0.0, NEG)] * NSA_GROUP, axis=1).astype(qx_ref.dtype)

    z = g_ref[0].T
    gt_ref[...] = 1.0 / (1.0 + jnp.exp(-z))

    def combine(oslc):
        ocmp = ocmp_ref[...]
        owin = owin_ref[...]
        heads = []
        for r in range(NSA_GROUP):
            sl = slice(r * tq, (r + 1) * tq)
            c = NSA_BRANCHES * (NSA_GROUP * g + r)
            heads.append(gt_ref[pl.ds(c, 1), :] * ocmp[:, sl] + gt_ref[pl.ds(c + 1, 1), :] * oslc[:, sl]
                         + gt_ref[pl.ds(c + 2, 1), :] * owin[:, sl])
        o_ref[0, :, 0:LANES] = jnp.concatenate(heads[0:2], axis=0).T
        o_ref[0, :, LANES:2 * LANES] = jnp.concatenate(heads[2:4], axis=0).T

    def slc_attend(c):
        nkeys = (c + 1) * sblk
        s = jnp.dot(ksx[0:nkeys, :], qx_ref[...], preferred_element_type=F32)
        causal = _iota((sblk, cols), 0) + (c * sblk - q0) <= i1
        diag = jnp.where(causal, s[c * sblk:nkeys], NEG)
        s = jnp.concatenate([s[0:c * sblk], diag], axis=0) if c else diag
        p, inv = _softmax_cols(s)
        combine(_dot(vst[:, 0:nkeys], p) * inv)

    dc = _div_pow2(q0, sblk)
    for c in range(S // sblk):
        @pl.when(dc == c)
        def _(c=c):
            slc_attend(c)


def _nsa(slopes, proj, kc, vc):
    B, S, _ = proj.shape
    tq = NSA_QTILE
    n_cmp = (S - CMP_LEN) // CMP_STRIDE + 1
    n_slc = S // SLC_BLOCK
    k_top = min(SLC_TOPK, n_slc)
    nr = kc.shape[1]
    cols = NSA_GROUP * tq
    assert HEAD_DIM + EXT_MASK_LANE + _round_up(n_slc, 2 * SUBLANES) <= LANES and S // POS_SPLIT <= POS_SPLIT
    assert SLC_CHUNK % tq == 0 and WINDOW % tq == 0 and S % SLC_CHUNK == 0
    kern = functools.partial(_nsa_kernel, S=S, tq=tq, n_cmp=n_cmp, n_slc=n_slc, k_top=k_top)
    qtiles = NSA_GROUP // HEADS_PER_TILE

    def seq(col):
        return pl.BlockSpec((1, S, LANES), lambda b, g, i: (b, 0, col))

    small = pl.BlockSpec((1, nr, LANES), lambda b, g, i: (b, 0, 0))
    return pl.pallas_call(
        kern,
        grid=(B, NSA_KV_HEADS, S // tq),
        in_specs=[
            pl.BlockSpec(memory_space=pltpu.SMEM),
            pl.BlockSpec((1, tq, LANES), lambda b, g, i: (b, i, COL_QN + qtiles * g)),
            pl.BlockSpec((1, tq, LANES), lambda b, g, i: (b, i, COL_QN + qtiles * g + 1)),
            pl.BlockSpec((1, tq, LANES), lambda b, g, i: (b, i, COL_G)),
            small, small, seq(COL_KS), seq(COL_VS), seq(COL_KW), seq(COL_VW),
        ],
        out_specs=pl.BlockSpec((1, tq, qtiles * LANES), lambda b, g, i: (b, i, g)),
        out_shape=jax.ShapeDtypeStruct((B, S, NSA_HEADS * HEAD_DIM), F32),
        scratch_shapes=[
            pltpu.VMEM((nr, LANES), MXU_DTYPE), pltpu.VMEM((HEAD_DIM, nr), MXU_DTYPE),
            pltpu.VMEM((S, LANES), MXU_DTYPE), pltpu.VMEM((HEAD_DIM, S), MXU_DTYPE),
            pltpu.VMEM((WINDOW + S, LANES), MXU_DTYPE),
            pltpu.VMEM((WINDOW // tq + S // tq, HEAD_DIM, tq), MXU_DTYPE),
            pltpu.VMEM((LANES, cols), MXU_DTYPE),
            pltpu.VMEM((LANES, tq), F32),
            pltpu.VMEM((HEAD_DIM, cols), F32), pltpu.VMEM((HEAD_DIM, cols), F32),
        ],
        compiler_params=_params("parallel", "parallel", "arbitrary"),
        name="nsa",
    )(slopes, proj, proj, proj, kc, vc, proj, proj, proj, proj)


def _outproj_kernel(om_ref, on_ref, x_ref, gm_ref, gn_ref, wm_ref, wn_ref, o_ref):
    y = _dot(_rms(om_ref[...], gm_ref[...]), wm_ref[...]) + _dot(_rms(on_ref[...], gn_ref[...]), wn_ref[...])
    o_ref[...] = x_ref[...] + y


def _outproj(om, on, x, gm, gn, w, l, tm):
    T, D = x.shape
    Wm, Wn = om.shape[1], on.shape[1]
    assert Wm == Wn and w.shape[1] == Wm + Wn
    return pl.pallas_call(
        _outproj_kernel,
        grid=(T // tm,),
        in_specs=[
            pl.BlockSpec((tm, Wm), lambda i: (i, 0)),
            pl.BlockSpec((tm, Wn), lambda i: (i, 0)),
            pl.BlockSpec((tm, D), lambda i: (i, 0)),
            pl.BlockSpec((None, 1, Wm), lambda i: (l, 0, 0)),
            pl.BlockSpec((None, 1, Wn), lambda i: (l, 0, 0)),
            pl.BlockSpec((None, Wm, D), lambda i: (l, 0, 0)),
            pl.BlockSpec((None, Wn, D), lambda i: (l, 1, 0)),
        ],
        out_specs=pl.BlockSpec((tm, D), lambda i: (i, 0)),
        out_shape=jax.ShapeDtypeStruct((T, D), F32),
        compiler_params=_params("parallel"),
        name="outproj",
    )(om, on, x, gm, gn, w, w)


def _alibi_slopes(n):
    slopes = 2.0 ** (-8.0 * np.arange(1, n + 1) / n)
    assert np.all(np.log2(slopes) == np.round(np.log2(slopes)))
    return jnp.asarray(slopes, dtype=F32)


def _token_tile(T, want):
    return want if T % want == 0 else T


def _mixer(x, B, S, l, mix_norm, w_in, pos_k, k_w1, k_w2, pos_v, v_w1, v_w2, moba_norm, nsa_norm, w_out):
    T, D = x.shape
    proj, ck, cv = _inproj(x, mix_norm, w_in, l, _token_tile(T, PROJ_TOKENS))
    proj = proj.reshape(B, S, IN_PAD)

    o_m = _moba(_alibi_slopes(MOBA_HEADS), proj)

    def windows(slab):
        return slab.reshape(NSA_KV_HEADS, B, S // CMP_STRIDE, CMP_STRIDE * HEAD_DIM)

    def placed(w2):
        return jnp.stack([jnp.pad(w2, ((0, 0), (g * HEAD_DIM, LANES - (g + 1) * HEAD_DIM)))
                          for g in range(NSA_KV_HEADS)])

    kc, vc = _compress(windows(ck), windows(cv), pos_k, pos_v, k_w1, v_w1, placed(k_w2), placed(v_w2), l)
    o_n = _nsa(_alibi_slopes(NSA_HEADS), proj, kc, vc)

    return _outproj(o_m.reshape(T, -1), o_n.reshape(T, -1), x, moba_norm, nsa_norm, w_out, l,
                    _token_tile(T, OUT_TOKENS))


def _swiglu_step(x, l, norm, w_gate, w_up, w_down, final_g, final=False):
    T = x.shape[0]
    return _ffn(x, norm, w_gate, w_up, w_down, l, _token_tile(T, FFN_TOKENS), FFN_CHUNK, final_g, final)


def _rows(p):
    return p.reshape(p.shape[0], 1, -1)


@jax.jit
def kernel(x, ffa_norm, ffa_w_gate, ffa_w_up, ffa_w_down, mix_norm, w_in, cmp_pos_k, cmp_k_w1, cmp_k_w2,
           cmp_pos_v, cmp_v_w1, cmp_v_w2, moba_out_norm, nsa_out_norm, w_out, ffb_norm, ffb_w_gate,
           ffb_w_up, ffb_w_down, final_norm):
    B, S, D = x.shape
    assert S % MOBA_BLOCK == 0 and S >= WINDOW and w_in.shape[-1] == IN_WIDTH
    h = x.reshape(B * S, D)
    ffa_norm, ffb_norm, mix_norm = _rows(ffa_norm), _rows(ffb_norm), _rows(mix_norm)
    moba_out_norm, nsa_out_norm = _rows(moba_out_norm), _rows(nsa_out_norm)
    cmp_pos_k, cmp_pos_v = _rows(cmp_pos_k), _rows(cmp_pos_v)
    depth = ffa_norm.shape[0]
    final_g = final_norm[None]
    for l in range(depth):
        h = _swiglu_step(h, l, ffa_norm, ffa_w_gate, ffa_w_up, ffa_w_down, final_g)
        h = _mixer(h, B, S, l, mix_norm, w_in, cmp_pos_k, cmp_k_w1, cmp_k_w2[l],
                   cmp_pos_v, cmp_v_w1, cmp_v_w2[l], moba_out_norm, nsa_out_norm, w_out)
        h = _swiglu_step(h, l, ffb_norm, ffb_w_gate, ffb_w_up, ffb_w_down, final_g, final=l == depth - 1)
    return h.reshape(B, S, D)
```

```python
import functools

import numpy as np
import jax
import jax.numpy as jnp
from jax import lax
from jax.experimental import pallas as pl
from jax.experimental.pallas import tpu as pltpu

HEAD_DIM = 64
MOBA_HEADS = 8
NSA_HEADS = 8
NSA_KV_HEADS = 2
NSA_GROUP = NSA_HEADS // NSA_KV_HEADS
NSA_BRANCHES = 3
MOBA_BLOCK = 256
MOBA_TOPK = 3
CMP_LEN = 32
CMP_STRIDE = 16
SLC_BLOCK = 64
SLC_TOPK = 16
WINDOW = 512
NEG = -1e30
SLC_FORCE = 1e4
EPS = 1e-6
SCALE = HEAD_DIM ** -0.5

LANES = 128
SUBLANES = 8
HEADS_PER_TILE = LANES // HEAD_DIM
COL_QM, COL_KM, COL_VM, COL_QN = 0, 4, 8, 12
COL_KC, COL_VC, COL_KS, COL_VS, COL_KW, COL_VW, COL_G = 16, 17, 18, 19, 20, 21, 22
IN_TILES = 24
IN_PAD = IN_TILES * LANES
IN_WIDTH = 2840
NSA_QTILE = 256
MOBA_TILE_BLOCKS = 2
FFN_UNROLL = 5
FFN_TOKENS, FFN_CHUNK = 1024, 256
PROJ_TOKENS = 512
OUT_TOKENS = 1024
SLC_CHUNK = 4 * SLC_BLOCK

MXU_DTYPE = jnp.bfloat16
VMEM_LIMIT = 48 * 1024 * 1024
F32 = jnp.float32
HI = lax.Precision.HIGHEST


def _dot(a, b):
    return jnp.dot(a.astype(MXU_DTYPE), b.astype(MXU_DTYPE), preferred_element_type=F32)


def _dot_nt(a, b, precision=None):
    return lax.dot_general(a, b, (((1,), (1,)), ((), ())), precision=precision,
                           preferred_element_type=F32)


def _dot_hi(a, b):
    return jnp.dot(a, b, precision=HI, preferred_element_type=F32)


def _iota(shape, dim):
    return lax.broadcasted_iota(jnp.int32, shape, dim)


def _div_pow2(x, n):
    assert n & (n - 1) == 0
    return x >> (n.bit_length() - 1)


def _round_up(n, m):
    return -(-n // m) * m


def _params(*sem):
    return pltpu.CompilerParams(dimension_semantics=sem, vmem_limit_bytes=VMEM_LIMIT)


def _rms(x, g):
    return x * lax.rsqrt(jnp.mean(x * x, axis=-1, keepdims=True) + EPS) * g


def _ffn_kernel(x_ref, g_ref, wg_ref, wu_ref, wd_ref, gf_ref, o_ref, h_ref, wgb, wub, wdb, *, nj, final):
    step = pl.program_id(0)

    def start():
        h_ref[...] = _rms(x_ref[...], g_ref[...]).astype(h_ref.dtype)

    def chunk(wg, wu, wd):
        h = h_ref[...]
        a = _dot(h, wg)
        b = _dot(h, wu)
        return _dot(a / (1.0 + jnp.exp(-a)) * b, wd)

    def finish():
        y = x_ref[...] + 0.5 * o_ref[...]
        o_ref[...] = _rms(y, gf_ref[...]) if final else y

    @pl.when(step < nj)
    def _():
        wgb[step] = wg_ref[...].astype(wgb.dtype)
        wub[step] = wu_ref[...].astype(wub.dtype)
        wdb[step] = wd_ref[...].astype(wdb.dtype)

        @pl.when(step == 0)
        def _():
            start()
            o_ref[...] = chunk(wgb[0], wub[0], wdb[0])

        @pl.when(step > 0)
        def _():
            o_ref[...] += chunk(wgb[step], wub[step], wdb[step])

        @pl.when(step == nj - 1)
        def _():
            finish()

    @pl.when(step >= nj)
    def _():
        start()
        o_ref[...] = chunk(wgb[0], wub[0], wdb[0])

        def body(j, carry):
            o_ref[...] += chunk(wgb[j], wub[j], wdb[j])
            return carry

        lax.fori_loop(1, nj, body, 0, unroll=FFN_UNROLL)
        finish()


def _ffn(x, g, wg, wu, wd, l, tm, tf, final_g, final):
    T, D = x.shape
    F = wg.shape[2]
    nj = F // tf
    tile = lambda s: (jnp.maximum(s - (nj - 1), 0), 0)
    col = lambda s: jnp.minimum(s, nj - 1)
    return pl.pallas_call(
        functools.partial(_ffn_kernel, nj=nj, final=final),
        grid=(nj + T // tm - 1,),
        in_specs=[
            pl.BlockSpec((tm, D), tile),
            pl.BlockSpec((None, 1, D), lambda s: (l, 0, 0)),
            pl.BlockSpec((None, D, tf), lambda s: (l, 0, col(s))),
            pl.BlockSpec((None, D, tf), lambda s: (l, 0, col(s))),
            pl.BlockSpec((None, tf, D), lambda s: (l, col(s), 0)),
            pl.BlockSpec((1, D), lambda s: (0, 0)),
        ],
        out_specs=pl.BlockSpec((tm, D), tile),
        out_shape=jax.ShapeDtypeStruct((T, D), F32),
        scratch_shapes=[pltpu.VMEM((tm, D), MXU_DTYPE), pltpu.VMEM((nj, D, tf), MXU_DTYPE),
                        pltpu.VMEM((nj, D, tf), MXU_DTYPE), pltpu.VMEM((nj, tf, D), MXU_DTYPE)],
        compiler_params=_params("arbitrary"),
        name="ffn",
    )(x, g, wg, wu, wd, final_g)


def _inproj_kernel(x_ref, g_ref, w_ref, o_ref, ck_ref, cv_ref, wb_ref, kc_tile, vc_tile):
    @pl.when(pl.program_id(0) == 0)
    def _():
        wb_ref[...] = jnp.zeros_like(wb_ref)
        wb_ref[:, 0:w_ref.shape[1]] = w_ref[...].astype(wb_ref.dtype)

    y = _dot(_rms(x_ref[...], g_ref[...]), wb_ref[...])
    o_ref[...] = y
    kc_tile[...] = y[:, COL_KC * LANES:(COL_KC + 1) * LANES]
    vc_tile[...] = y[:, COL_VC * LANES:(COL_VC + 1) * LANES]
    rows = o_ref.shape[0] // CMP_STRIDE
    for t in range(CMP_STRIDE):
        for ref, tile in ((ck_ref, kc_tile), (cv_ref, vc_tile)):
            tok = tile[pl.ds(t, rows, stride=CMP_STRIDE), :]
            for grp in range(NSA_KV_HEADS):
                ref[grp, :, t * HEAD_DIM:(t + 1) * HEAD_DIM] = tok[:, grp * HEAD_DIM:(grp + 1) * HEAD_DIM]


def _inproj(x, g, w, l, tm):
    T, D = x.shape
    W = w.shape[2]
    N = _round_up(W, LANES * 4)
    slab = pl.BlockSpec((NSA_KV_HEADS, tm // CMP_STRIDE, CMP_STRIDE * HEAD_DIM), lambda i: (0, i, 0))
    slab_shape = jax.ShapeDtypeStruct((NSA_KV_HEADS, T // CMP_STRIDE, CMP_STRIDE * HEAD_DIM), F32)
    return pl.pallas_call(
        _inproj_kernel,
        grid=(T // tm,),
        in_specs=[
            pl.BlockSpec((tm, D), lambda i: (i, 0)),
            pl.BlockSpec((None, 1, D), lambda i: (l, 0, 0)),
            pl.BlockSpec((None, D, W), lambda i: (l, 0, 0), pipeline_mode=pl.Buffered(1)),
        ],
        out_specs=(pl.BlockSpec((tm, N), lambda i: (i, 0)), slab, slab),
        out_shape=(jax.ShapeDtypeStruct((T, N), F32), slab_shape, slab_shape),
        scratch_shapes=[pltpu.VMEM((D, N), MXU_DTYPE), pltpu.VMEM((tm, LANES), F32), pltpu.VMEM((tm, LANES), F32)],
        compiler_params=_params("arbitrary"),
        name="inproj",
    )(x, g, w)


POS_SPLIT = 256
EXT_PAD_LANE = 4
EXT_MASK_LANE = 2 * SUBLANES


def _key_ext(pos, lane, mask_block=0, pad=None):
    ext = jnp.where(lane == 0, _div_pow2(pos, POS_SPLIT),
                    jnp.where(lane == 1, pos & (POS_SPLIT - 1),
                              jnp.where((lane == 2) | (lane == 3), 1, 0)))
    if mask_block:
        ext = jnp.where(lane - EXT_MASK_LANE == _div_pow2(pos, mask_block), 1, ext)
    if pad is not None:
        ext = jnp.where(pad, jnp.where(lane == EXT_PAD_LANE, 1, 0), ext)
    return ext.astype(F32)


def _query_ext(slope, t):
    r = _iota((SUBLANES, t.shape[1]), 0)
    big = slope * float(POS_SPLIT)
    hi = _div_pow2(t, POS_SPLIT).astype(F32)
    lo = (t & (POS_SPLIT - 1)).astype(F32)
    return jnp.where(r == 0, big, jnp.where(r == 1, slope, jnp.where(r == 2, -(big * hi),
                     jnp.where(r == 3, -(slope * lo), jnp.where(r == EXT_PAD_LANE, NEG, 0.0)))))


def _softmax_cols(s):
    p = jnp.exp(s - jnp.max(s, axis=0, keepdims=True))
    return p, 1.0 / jnp.sum(p, axis=0, keepdims=True)


def _rank_before(score, n, idx):
    rank = jnp.zeros(score.shape, jnp.int32)
    for m in range(n):
        row = score[m:m + 1, :]
        better = (row > score) | ((row == score) & (m < idx))
        rank = rank + better.astype(jnp.int32)
    return rank


V_ROWS = HEAD_DIM + 2 * SUBLANES


def _with_ones(vt):
    pad = (_iota((V_ROWS - HEAD_DIM, vt.shape[1]), 0) == 0).astype(vt.dtype)
    return jnp.concatenate([vt, pad], axis=0)


def _moba_kernel(slopes_ref, q_ref, k_ref, v_ref, o_ref,
                 kmean_ref, kb_ref, vt_ref, row_ref, *, nb, kk, tq):
    hp = pl.program_id(1)
    ti = pl.program_id(2)
    blk = MOBA_BLOCK
    per = tq // blk
    cols = HEADS_PER_TILE * tq

    @pl.when(ti == 0)
    def _():
        kmean_ref[...] = jnp.zeros_like(kmean_ref)
        for n in range(nb):
            rows = slice(n * blk, (n + 1) * blk)
            kmean_ref[n:n + 1, :] = jnp.mean(k_ref[0, rows, :], axis=0, keepdims=True)
            vt = v_ref[0, rows, :].T.astype(vt_ref.dtype)
            for h in range(HEADS_PER_TILE):
                vt_ref[h, :, rows] = _with_ones(vt[h * HEAD_DIM:(h + 1) * HEAD_DIM])
        kb_ref[...] = k_ref[0].astype(kb_ref.dtype)

    q0 = ti * tq
    lo = _iota((tq, LANES), 1) < HEAD_DIM
    q = q_ref[0]
    qs = jnp.concatenate([jnp.where(lo, q, 0.0), jnp.where(lo, 0.0, q)], axis=0)
    col1 = _iota((1, cols), 1)
    slope = jnp.where(col1 < tq, slopes_ref[HEADS_PER_TILE * hp], slopes_ref[HEADS_PER_TILE * hp + 1])
    i1 = col1 & (tq - 1)
    own = ti * per + _div_pow2(i1, blk)

    gate = _dot_nt(kmean_ref[...], qs, precision=HI)
    n_idx = _iota(gate.shape, 0)
    cand = n_idx < own
    gm = jnp.where(cand, gate, NEG)
    keep = (cand & (_rank_before(gm, nb - 1, n_idx) < kk)) | (n_idx == own)
    row_ref[...] = slope * (n_idx * blk - q0).astype(F32) + jnp.where(keep, 0.0, NEG)

    qsb = (qs * SCALE).astype(MXU_DTYPE)
    rel = _iota((blk, cols), 0) - i1
    bias = slope * rel.astype(F32)

    def attend(t):
        first = per * t
        nkeys = (first + per) * blk
        s = _dot_nt(kb_ref[0:nkeys, :], qsb)
        pieces = []
        for n in range(first + per):
            piece = s[n * blk:(n + 1) * blk] + bias + row_ref[n:n + 1, :]
            if n >= first:
                piece = jnp.where(rel + (n - first) * blk <= 0, piece, NEG)
            pieces.append(piece)
        s = jnp.concatenate(pieces, axis=0)
        pb = jnp.exp((s - jnp.max(s, axis=0, keepdims=True)).astype(MXU_DTYPE))
        halves = []
        for h in range(HEADS_PER_TILE):
            acc = jnp.dot(vt_ref[h, :, 0:nkeys], pb[:, h * tq:(h + 1) * tq], preferred_element_type=F32)
            halves.append(acc[0:HEAD_DIM] * (1.0 / acc[HEAD_DIM:HEAD_DIM + 1]))
        o_ref[0] = jnp.concatenate(halves, axis=0).T

    for t in range(nb // per):
        @pl.when(ti == t)
        def _(t=t):
            attend(t)


def _moba(slopes, proj):
    B, S, _ = proj.shape
    tq = MOBA_TILE_BLOCKS * MOBA_BLOCK
    nb = S // MOBA_BLOCK
    assert S % tq == 0
    kk = max(1, min(MOBA_TOPK, nb - 1))
    n_tiles = MOBA_HEADS // HEADS_PER_TILE
    kern = functools.partial(_moba_kernel, nb=nb, kk=kk, tq=tq)
    return pl.pallas_call(
        kern,
        grid=(B, n_tiles, S // tq),
        in_specs=[
            pl.BlockSpec(memory_space=pltpu.SMEM),
            pl.BlockSpec((1, tq, LANES), lambda b, h, i: (b, i, COL_QM + h)),
            pl.BlockSpec((1, S, LANES), lambda b, h, i: (b, 0, COL_KM + h)),
            pl.BlockSpec((1, S, LANES), lambda b, h, i: (b, 0, COL_VM + h)),
        ],
        out_specs=pl.BlockSpec((1, tq, LANES), lambda b, h, i: (b, i, h)),
        out_shape=jax.ShapeDtypeStruct((B, S, MOBA_HEADS * HEAD_DIM), F32),
        scratch_shapes=[
            pltpu.VMEM((_round_up(nb, SUBLANES), LANES), F32),
            pltpu.VMEM((S, LANES), MXU_DTYPE),
            pltpu.VMEM((HEADS_PER_TILE, V_ROWS, S), MXU_DTYPE),
            pltpu.VMEM((_round_up(nb, SUBLANES), HEADS_PER_TILE * tq), F32),
        ],
        compiler_params=_params("parallel", "parallel", "arbitrary"),
        name="moba",
    )(slopes, proj, proj, proj)


def _gelu_tanh(x):
    return x * (0.5 * (1.0 + jnp.tanh(0.7978845608028654 * (x + 0.044715 * (x * x * x)))))


def _compress_kernel(xk_ref, xv_ref, pk_ref, pv_ref, w1k_ref, w1v_ref, w2k_ref, w2v_ref, kc_ref, vc_ref):
    g = pl.program_id(0)
    half = (CMP_LEN // 2) * HEAD_DIM
    for x_ref, p_ref, w1_ref, w2_ref, o_ref in ((xk_ref, pk_ref, w1k_ref, w2k_ref, kc_ref),
                                               (xv_ref, pv_ref, w1v_ref, w2v_ref, vc_ref)):
        B, nr, W = x_ref.shape
        x = x_ref[...].reshape(B * nr, W)
        top = _dot_hi(x + p_ref[:, 0:half], w1_ref[0:half, :])
        bot = _dot_hi(x + p_ref[:, half:2 * half], w1_ref[half:2 * half, :])
        hid = top + pltpu.roll(bot, B * nr - 1, 0)
        out = _dot_hi(_gelu_tanh(hid), w2_ref[...]).reshape(B, nr, LANES)

        @pl.when(g == 0)
        def _(o_ref=o_ref, out=out):
            o_ref[...] = out

        @pl.when(g > 0)
        def _(o_ref=o_ref, out=out):
            o_ref[...] += out


def _compress(xk, xv, pk, pv, w1k, w1v, w2k, w2v, l):
    G, B, nr, W = xk.shape
    H = w1k.shape[2]
    layer = lambda *shape: pl.BlockSpec((None,) + shape, lambda g: (l,) + (0,) * len(shape))
    xspec = pl.BlockSpec((None, B, nr, W), lambda g: (g, 0, 0, 0))
    w2spec = pl.BlockSpec((None, H, LANES), lambda g: (g, 0, 0))
    ospec = pl.BlockSpec((B, nr, LANES), lambda g: (0, 0, 0))
    oshape = jax.ShapeDtypeStruct((B, nr, LANES), F32)
    return pl.pallas_call(
        _compress_kernel,
        grid=(G,),
        in_specs=[xspec, xspec, layer(1, 2 * W), layer(1, 2 * W), layer(2 * W, H), layer(2 * W, H),
                  w2spec, w2spec],
        out_specs=(ospec, ospec),
        out_shape=(oshape, oshape),
        compiler_params=_params("arbitrary"),
        name="nsa_compress",
    )(xk, xv, pk, pv, w1k, w1v, w2k, w2v)


def _nsa_kernel(slopes_ref, qa_ref, qb_ref, g_ref, kc_ref, vc_ref, ks_ref, vs_ref, kw_ref, vw_ref,
                o_ref, kcx, vct, ksx, vst, kwx, vwt, qx_ref, gt_ref, ocmp_ref, owin_ref,
                *, S, tq, n_cmp, n_slc, k_top):
    g = pl.program_id(1)
    qi = pl.program_id(2)
    q0 = qi * tq
    cols = NSA_GROUP * tq
    sblk = SLC_CHUNK
    nr = kcx.shape[0]
    nj = _round_up(n_slc, 2 * SUBLANES)
    n_back = WINDOW // tq
    ext0 = HEAD_DIM

    @pl.when(qi == 0)
    def _():
        def both_halves(x):
            mine = (_iota(x.shape, 1) >= HEAD_DIM) == (g == 1)
            return jnp.where(mine, x, pltpu.roll(x, HEAD_DIM, 1))

        def with_ext(x, pos, mask_block=0):
            lane = _iota(x.shape, 1)
            return jnp.where(lane < HEAD_DIM, both_halves(x), _key_ext(pos, lane - ext0, mask_block)).astype(MXU_DTYPE)

        def transposed(x):
            return both_halves(x).T[0:HEAD_DIM].astype(MXU_DTYPE)

        qx_ref[...] = jnp.zeros_like(qx_ref)
        kcx[...] = with_ext(kc_ref[0], _iota((nr, LANES), 0) * CMP_STRIDE + (CMP_LEN - 1))
        vct[...] = transposed(vc_ref[0])
        lane = _iota((WINDOW, LANES), 1)
        kwx[0:WINDOW, :] = _key_ext(lane * 0, lane - ext0, pad=lane >= 0).astype(MXU_DTYPE)
        for c in range(n_back):
            vwt[c] = jnp.zeros(vwt.shape[1:], vwt.dtype)
        for c in range(S // sblk):
            rows = slice(c * sblk, (c + 1) * sblk)
            pos = c * sblk + _iota((sblk, LANES), 0)
            ksx[rows, :] = with_ext(ks_ref[0, rows, :], pos, SLC_BLOCK)
            kwx[WINDOW + c * sblk:WINDOW + (c + 1) * sblk, :] = with_ext(kw_ref[0, rows, :], pos)
            vst[:, rows] = transposed(vs_ref[0, rows, :])
        for c in range(S // tq):
            vwt[n_back + c] = transposed(vw_ref[0, c * tq:(c + 1) * tq, :])

    qat = qa_ref[0].T
    qbt = qb_ref[0].T
    qt = jnp.concatenate([qat[0:HEAD_DIM], qat[HEAD_DIM:LANES], qbt[0:HEAD_DIM], qbt[HEAD_DIM:LANES]], axis=1)
    col1 = _iota((1, cols), 1)
    h0 = NSA_GROUP * g
    slope = jnp.where(col1 < tq, slopes_ref[h0],
                      jnp.where(col1 < 2 * tq, slopes_ref[h0 + 1],
                                jnp.where(col1 < 3 * tq, slopes_ref[h0 + 2], slopes_ref[h0 + 3])))
    i1 = col1 & (tq - 1)
    t1 = q0 + i1
    qx_ref[0:HEAD_DIM, :] = (qt * SCALE).astype(qx_ref.dtype)
    qx_ref[ext0:ext0 + EXT_MASK_LANE, :] = jnp.concatenate(
        [_query_ext(slope, t1), jnp.zeros((EXT_MASK_LANE - SUBLANES, cols), F32)], axis=0).astype(qx_ref.dtype)
    qx = qx_ref[...]

    relw = _iota((tq, cols), 0) - i1
    s = jnp.dot(kwx[pl.ds(pl.multiple_of(q0, tq), WINDOW + tq), :], qx, preferred_element_type=F32)
    s = jnp.concatenate([jnp.where(relw > 0, s[0:tq], NEG), s[tq:WINDOW],
                         jnp.where(relw <= 0, s[WINDOW:WINDOW + tq], NEG)], axis=0)
    p, inv = _softmax_cols(s)
    pb = p.astype(MXU_DTYPE)
    acc = _dot(vwt[qi], pb[0:tq])
    for e in range(1, n_back + 1):
        acc = acc + _dot(vwt[qi + e], pb[e * tq:(e + 1) * tq])
    owin_ref[...] = acc * inv

    nidx = _iota((nr, cols), 0)
    maskc = (nidx * CMP_STRIDE + (CMP_LEN - 1) <= t1) & (nidx < n_cmp)
    s = jnp.where(maskc, jnp.dot(kcx[...], qx, preferred_element_type=F32), NEG)
    p = jnp.where(maskc, jnp.exp(s - jnp.max(s, axis=0, keepdims=True)), 0.0)
    p = p / jnp.maximum(jnp.sum(p, axis=0, keepdims=True), 1e-30)
    ocmp_ref[...] = _dot(vct[...], p)

    psum = p[:, 0:tq] + p[:, tq:2 * tq] + p[:, 2 * tq:3 * tq] + p[:, 3 * tq:4 * tq]
    jj = _iota((nj, nr), 0) * SLC_BLOCK
    nn = _iota((nj, nr), 1) * CMP_STRIDE
    overlap = ((nn < jj + SLC_BLOCK) & (nn + CMP_LEN > jj)
               & (nn < n_cmp * CMP_STRIDE) & (jj < n_slc * SLC_BLOCK)).astype(F32)
    imp = _dot_hi(overlap, psum)
    jl = _iota((nj, tq), 0)
    tb = _div_pow2(q0 + _iota((nj, tq), 1), SLC_BLOCK)
    cand = jl <= tb
    forced = (jl == 0) | (jl == tb) | (jl == tb - 1)
    score = jnp.where(cand, jnp.where(forced, SLC_FORCE, imp), NEG)
    mask0 = ext0 + EXT_MASK_LANE

    def set_mask(rows_ranked):
        keep = cand if rows_ranked <= k_top else cand & (_rank_before(score, rows_ranked, jl) < k_top)
        qx_ref[mask0:mask0 + nj, :] = jnp.concatenate([jnp.where(keep, 0.0, NEG)] * NSA_GROUP, axis=1).astype(qx_ref.dtype)

    z = g_ref[0].T
    gt_ref[...] = 1.0 / (1.0 + jnp.exp(-z))

    def combine(oslc):
        ocmp = ocmp_ref[...]
        owin = owin_ref[...]
        heads = []
        for r in range(NSA_GROUP):
            sl = slice(r * tq, (r + 1) * tq)
            c = NSA_BRANCHES * (NSA_GROUP * g + r)
            heads.append(gt_ref[pl.ds(c, 1), :] * ocmp[:, sl] + gt_ref[pl.ds(c + 1, 1), :] * oslc[:, sl]
                         + gt_ref[pl.ds(c + 2, 1), :] * owin[:, sl])
        o_ref[0, :, 0:LANES] = jnp.concatenate(heads[0:2], axis=0).T
        o_ref[0, :, LANES:2 * LANES] = jnp.concatenate(heads[2:4], axis=0).T

    def slc_attend(c):
        nkeys = (c + 1) * sblk
        set_mask(min(nkeys // SLC_BLOCK, n_slc))
        s = jnp.dot(ksx[0:nkeys, :], qx_ref[...], preferred_element_type=F32)
        causal = _iota((sblk, cols), 0) + (c * sblk - q0) <= i1
        diag = jnp.where(causal, s[c * sblk:nkeys], NEG)
        s = jnp.concatenate([s[0:c * sblk], diag], axis=0) if c else diag
        p, inv = _softmax_cols(s)
        combine(_dot(vst[:, 0:nkeys], p) * inv)

    dc = _div_pow2(q0, sblk)
    for c in range(S // sblk):
        @pl.when(dc == c)
        def _(c=c):
            slc_attend(c)


def _nsa(slopes, proj, kc, vc):
    B, S, _ = proj.shape
    tq = NSA_QTILE
    n_cmp = (S - CMP_LEN) // CMP_STRIDE + 1
    n_slc = S // SLC_BLOCK
    k_top = min(SLC_TOPK, n_slc)
    nr = kc.shape[1]
    cols = NSA_GROUP * tq
    assert HEAD_DIM + EXT_MASK_LANE + _round_up(n_slc, 2 * SUBLANES) <= LANES and S // POS_SPLIT <= POS_SPLIT
    assert SLC_CHUNK % tq == 0 and WINDOW % tq == 0 and S % SLC_CHUNK == 0
    kern = functools.partial(_nsa_kernel, S=S, tq=tq, n_cmp=n_cmp, n_slc=n_slc, k_top=k_top)
    qtiles = NSA_GROUP // HEADS_PER_TILE

    def seq(col):
        return pl.BlockSpec((1, S, LANES), lambda b, g, i: (b, 0, col))

    small = pl.BlockSpec((1, nr, LANES), lambda b, g, i: (b, 0, 0))
    return pl.pallas_call(
        kern,
        grid=(B, NSA_KV_HEADS, S // tq),
        in_specs=[
            pl.BlockSpec(memory_space=pltpu.SMEM),
            pl.BlockSpec((1, tq, LANES), lambda b, g, i: (b, i, COL_QN + qtiles * g)),
            pl.BlockSpec((1, tq, LANES), lambda b, g, i: (b, i, COL_QN + qtiles * g + 1)),
            pl.BlockSpec((1, tq, LANES), lambda b, g, i: (b, i, COL_G)),
            small, small, seq(COL_KS), seq(COL_VS), seq(COL_KW), seq(COL_VW),
        ],
        out_specs=pl.BlockSpec((1, tq, qtiles * LANES), lambda b, g, i: (b, i, g)),
        out_shape=jax.ShapeDtypeStruct((B, S, NSA_HEADS * HEAD_DIM), F32),
        scratch_shapes=[
            pltpu.VMEM((nr, LANES), MXU_DTYPE), pltpu.VMEM((HEAD_DIM, nr), MXU_DTYPE),
            pltpu.VMEM((S, LANES), MXU_DTYPE), pltpu.VMEM((HEAD_DIM, S), MXU_DTYPE),
            pltpu.VMEM((WINDOW + S, LANES), MXU_DTYPE),
            pltpu.VMEM((WINDOW // tq + S // tq, HEAD_DIM, tq), MXU_DTYPE),
            pltpu.VMEM((LANES, cols), MXU_DTYPE),
            pltpu.VMEM((LANES, tq), F32),
            pltpu.VMEM((HEAD_DIM, cols), F32), pltpu.VMEM((HEAD_DIM, cols), F32),
        ],
        compiler_params=_params("parallel", "parallel", "arbitrary"),
        name="nsa",
    )(slopes, proj, proj, proj, kc, vc, proj, proj, proj, proj)


def _outproj_kernel(om_ref, on_ref, x_ref, gm_ref, gn_ref, wm_ref, wn_ref, o_ref):
    y = _dot(_rms(om_ref[...], gm_ref[...]), wm_ref[...]) + _dot(_rms(on_ref[...], gn_ref[...]), wn_ref[...])
    o_ref[...] = x_ref[...] + y


def _outproj(om, on, x, gm, gn, w, l, tm):
    T, D = x.shape
    Wm, Wn = om.shape[1], on.shape[1]
    assert Wm == Wn and w.shape[1] == Wm + Wn
    return pl.pallas_call(
        _outproj_kernel,
        grid=(T // tm,),
        in_specs=[
            pl.BlockSpec((tm, Wm), lambda i: (i, 0)),
            pl.BlockSpec((tm, Wn), lambda i: (i, 0)),
            pl.BlockSpec((tm, D), lambda i: (i, 0)),
            pl.BlockSpec((None, 1, Wm), lambda i: (l, 0, 0)),
            pl.BlockSpec((None, 1, Wn), lambda i: (l, 0, 0)),
            pl.BlockSpec((None, Wm, D), lambda i: (l, 0, 0)),
            pl.BlockSpec((None, Wn, D), lambda i: (l, 1, 0)),
        ],
        out_specs=pl.BlockSpec((tm, D), lambda i: (i, 0)),
        out_shape=jax.ShapeDtypeStruct((T, D), F32),
        compiler_params=_params("parallel"),
        name="outproj",
    )(om, on, x, gm, gn, w, w)


def _alibi_slopes(n):
    slopes = 2.0 ** (-8.0 * np.arange(1, n + 1) / n)
    assert np.all(np.log2(slopes) == np.round(np.log2(slopes)))
    return jnp.asarray(slopes, dtype=F32)


def _token_tile(T, want):
    return want if T % want == 0 else T


def _mixer(x, B, S, l, mix_norm, w_in, pos_k, k_w1, k_w2, pos_v, v_w1, v_w2, moba_norm, nsa_norm, w_out):
    T, D = x.shape
    proj, ck, cv = _inproj(x, mix_norm, w_in, l, _token_tile(T, PROJ_TOKENS))
    proj = proj.reshape(B, S, IN_PAD)

    o_m = _moba(_alibi_slopes(MOBA_HEADS), proj)

    def windows(slab):
        return slab.reshape(NSA_KV_HEADS, B, S // CMP_STRIDE, CMP_STRIDE * HEAD_DIM)

    def placed(w2):
        return jnp.stack([jnp.pad(w2, ((0, 0), (g * HEAD_DIM, LANES - (g + 1) * HEAD_DIM)))
                          for g in range(NSA_KV_HEADS)])

    kc, vc = _compress(windows(ck), windows(cv), pos_k, pos_v, k_w1, v_w1, placed(k_w2), placed(v_w2), l)
    o_n = _nsa(_alibi_slopes(NSA_HEADS), proj, kc, vc)

    return _outproj(o_m.reshape(T, -1), o_n.reshape(T, -1), x, moba_norm, nsa_norm, w_out, l,
                    _token_tile(T, OUT_TOKENS))


def _swiglu_step(x, l, norm, w_gate, w_up, w_down, final_g, final=False):
    T = x.shape[0]
    return _ffn(x, norm, w_gate, w_up, w_down, l, _token_tile(T, FFN_TOKENS), FFN_CHUNK, final_g, final)


def _rows(p):
    return p.reshape(p.shape[0], 1, -1)


@jax.jit
def kernel(x, ffa_norm, ffa_w_gate, ffa_w_up, ffa_w_down, mix_norm, w_in, cmp_pos_k, cmp_k_w1, cmp_k_w2,
           cmp_pos_v, cmp_v_w1, cmp_v_w2, moba_out_norm, nsa_out_norm, w_out, ffb_norm, ffb_w_gate,
           ffb_w_up, ffb_w_down, final_norm):
    B, S, D = x.shape
    assert S % MOBA_BLOCK == 0 and S >= WINDOW and w_in.shape[-1] == IN_WIDTH
    h = x.reshape(B * S, D)
    ffa_norm, ffb_norm, mix_norm = _rows(ffa_norm), _rows(ffb_norm), _rows(mix_norm)
    moba_out_norm, nsa_out_norm = _rows(moba_out_norm), _rows(nsa_out_norm)
    cmp_pos_k, cmp_pos_v = _rows(cmp_pos_k), _rows(cmp_pos_v)
    depth = ffa_norm.shape[0]
    final_g = final_norm[None]
    for l in range(depth):
        h = _swiglu_step(h, l, ffa_norm, ffa_w_gate, ffa_w_up, ffa_w_down, final_g)
        h = _mixer(h, B, S, l, mix_norm, w_in, cmp_pos_k, cmp_k_w1, cmp_k_w2[l],
                   cmp_pos_v, cmp_v_w1, cmp_v_w2[l], moba_out_norm, nsa_out_norm, w_out)
        h = _swiglu_step(h, l, ffb_norm, ffb_w_gate, ffb_w_up, ffb_w_down, final_g, final=l == depth - 1)
    return h.reshape(B, S, D)
```

```python
import functools

import numpy as np
import jax
import jax.numpy as jnp
from jax import lax
from jax.experimental import pallas as pl
from jax.experimental.pallas import tpu as pltpu

HEAD_DIM = 64
MOBA_HEADS = 8
NSA_HEADS = 8
NSA_KV_HEADS = 2
NSA_GROUP = NSA_HEADS // NSA_KV_HEADS
NSA_BRANCHES = 3
MOBA_BLOCK = 256
MOBA_TOPK = 3
CMP_LEN = 32
CMP_STRIDE = 16
SLC_BLOCK = 64
SLC_TOPK = 16
WINDOW = 512
NEG = -1e30
SLC_FORCE = 1e4
EPS = 1e-6
SCALE = HEAD_DIM ** -0.5

LANES = 128
SUBLANES = 8
HEADS_PER_TILE = LANES // HEAD_DIM
COL_QM, COL_KM, COL_VM, COL_QN = 0, 4, 8, 12
COL_KC, COL_VC, COL_KS, COL_VS, COL_KW, COL_VW, COL_G = 16, 17, 18, 19, 20, 21, 22
IN_TILES = 24
IN_PAD = IN_TILES * LANES
IN_WIDTH = 2840
NSA_QTILE = 256
MOBA_TILE_BLOCKS = 2
FFN_UNROLL = 5
FFN_TOKENS, FFN_CHUNK = 1024, 256
PROJ_TOKENS = 512
OUT_TOKENS = 1024
SLC_CHUNK = 4 * SLC_BLOCK

MXU_DTYPE = jnp.bfloat16
VMEM_LIMIT = 48 * 1024 * 1024
F32 = jnp.float32
HI = lax.Precision.HIGHEST


def _dot(a, b):
    return jnp.dot(a.astype(MXU_DTYPE), b.astype(MXU_DTYPE), preferred_element_type=F32)


def _dot_nt(a, b, precision=None):
    return lax.dot_general(a, b, (((1,), (1,)), ((), ())), precision=precision,
                           preferred_element_type=F32)


def _dot_hi(a, b):
    return jnp.dot(a, b, precision=HI, preferred_element_type=F32)


def _iota(shape, dim):
    return lax.broadcasted_iota(jnp.int32, shape, dim)


def _div_pow2(x, n):
    assert n & (n - 1) == 0
    return x >> (n.bit_length() - 1)


def _round_up(n, m):
    return -(-n // m) * m


def _params(*sem):
    return pltpu.CompilerParams(dimension_semantics=sem, vmem_limit_bytes=VMEM_LIMIT)


def _rms(x, g):
    return x * lax.rsqrt(jnp.mean(x * x, axis=-1, keepdims=True) + EPS) * g


def _ffn_kernel(x_ref, g_ref, wg_ref, wu_ref, wd_ref, gf_ref, o_ref, h_ref, wgb, wub, wdb, *, nj, final):
    step = pl.program_id(0)

    def start():
        h_ref[...] = _rms(x_ref[...], g_ref[...]).astype(h_ref.dtype)

    def chunk(wg, wu, wd):
        h = h_ref[...]
        a = _dot(h, wg)
        b = _dot(h, wu)
        return _dot(a / (1.0 + jnp.exp(-a)) * b, wd)

    def finish():
        y = x_ref[...] + 0.5 * o_ref[...]
        o_ref[...] = _rms(y, gf_ref[...]) if final else y

    @pl.when(step < nj)
    def _():
        wgb[step] = wg_ref[...].astype(wgb.dtype)
        wub[step] = wu_ref[...].astype(wub.dtype)
        wdb[step] = wd_ref[...].astype(wdb.dtype)

        @pl.when(step == 0)
        def _():
            start()
            o_ref[...] = chunk(wgb[0], wub[0], wdb[0])

        @pl.when(step > 0)
        def _():
            o_ref[...] += chunk(wgb[step], wub[step], wdb[step])

        @pl.when(step == nj - 1)
        def _():
            finish()

    @pl.when(step >= nj)
    def _():
        start()
        o_ref[...] = chunk(wgb[0], wub[0], wdb[0])

        def body(j, carry):
            o_ref[...] += chunk(wgb[j], wub[j], wdb[j])
            return carry

        lax.fori_loop(1, nj, body, 0, unroll=FFN_UNROLL)
        finish()


def _ffn(x, g, wg, wu, wd, l, tm, tf, final_g, final):
    T, D = x.shape
    F = wg.shape[2]
    nj = F // tf
    tile = lambda s: (jnp.maximum(s - (nj - 1), 0), 0)
    col = lambda s: jnp.minimum(s, nj - 1)
    return pl.pallas_call(
        functools.partial(_ffn_kernel, nj=nj, final=final),
        grid=(nj + T // tm - 1,),
        in_specs=[
            pl.BlockSpec((tm, D), tile),
            pl.BlockSpec((None, 1, D), lambda s: (l, 0, 0)),
            pl.BlockSpec((None, D, tf), lambda s: (l, 0, col(s))),
            pl.BlockSpec((None, D, tf), lambda s: (l, 0, col(s))),
            pl.BlockSpec((None, tf, D), lambda s: (l, col(s), 0)),
            pl.BlockSpec((1, D), lambda s: (0, 0)),
        ],
        out_specs=pl.BlockSpec((tm, D), tile),
        out_shape=jax.ShapeDtypeStruct((T, D), F32),
        scratch_shapes=[pltpu.VMEM((tm, D), MXU_DTYPE), pltpu.VMEM((nj, D, tf), MXU_DTYPE),
                        pltpu.VMEM((nj, D, tf), MXU_DTYPE), pltpu.VMEM((nj, tf, D), MXU_DTYPE)],
        compiler_params=_params("arbitrary"),
        name="ffn",
    )(x, g, wg, wu, wd, final_g)


def _inproj_kernel(x_ref, g_ref, w_ref, o_ref, ck_ref, cv_ref, wb_ref, kc_tile, vc_tile):
    @pl.when(pl.program_id(0) == 0)
    def _():
        wb_ref[...] = jnp.zeros_like(wb_ref)
        wb_ref[:, 0:w_ref.shape[1]] = w_ref[...].astype(wb_ref.dtype)

    y = _dot(_rms(x_ref[...], g_ref[...]), wb_ref[...])
    o_ref[...] = y
    kc_tile[...] = y[:, COL_KC * LANES:(COL_KC + 1) * LANES]
    vc_tile[...] = y[:, COL_VC * LANES:(COL_VC + 1) * LANES]
    rows = o_ref.shape[0] // CMP_STRIDE
    for t in range(CMP_STRIDE):
        for ref, tile in ((ck_ref, kc_tile), (cv_ref, vc_tile)):
            tok = tile[pl.ds(t, rows, stride=CMP_STRIDE), :]
            for grp in range(NSA_KV_HEADS):
                ref[grp, :, t * HEAD_DIM:(t + 1) * HEAD_DIM] = tok[:, grp * HEAD_DIM:(grp + 1) * HEAD_DIM]


def _inproj(x, g, w, l, tm):
    T, D = x.shape
    W = w.shape[2]
    N = _round_up(W, LANES * 4)
    slab = pl.BlockSpec((NSA_KV_HEADS, tm // CMP_STRIDE, CMP_STRIDE * HEAD_DIM), lambda i: (0, i, 0))
    slab_shape = jax.ShapeDtypeStruct((NSA_KV_HEADS, T // CMP_STRIDE, CMP_STRIDE * HEAD_DIM), F32)
    return pl.pallas_call(
        _inproj_kernel,
        grid=(T // tm,),
        in_specs=[
            pl.BlockSpec((tm, D), lambda i: (i, 0)),
            pl.BlockSpec((None, 1, D), lambda i: (l, 0, 0)),
            pl.BlockSpec((None, D, W), lambda i: (l, 0, 0), pipeline_mode=pl.Buffered(1)),
        ],
        out_specs=(pl.BlockSpec((tm, N), lambda i: (i, 0)), slab, slab),
        out_shape=(jax.ShapeDtypeStruct((T, N), F32), slab_shape, slab_shape),
        scratch_shapes=[pltpu.VMEM((D, N), MXU_DTYPE), pltpu.VMEM((tm, LANES), F32), pltpu.VMEM((tm, LANES), F32)],
        compiler_params=_params("arbitrary"),
        name="inproj",
    )(x, g, w)


POS_SPLIT = 256
EXT_PAD_LANE = 4
EXT_MASK_LANE = 2 * SUBLANES


def _key_ext(pos, lane, mask_block=0, pad=None):
    ext = jnp.where(lane == 0, _div_pow2(pos, POS_SPLIT),
                    jnp.where(lane == 1, pos & (POS_SPLIT - 1),
                              jnp.where((lane == 2) | (lane == 3), 1, 0)))
    if mask_block:
        ext = jnp.where(lane - EXT_MASK_LANE == _div_pow2(pos, mask_block), 1, ext)
    if pad is not None:
        ext = jnp.where(pad, jnp.where(lane == EXT_PAD_LANE, 1, 0), ext)
    return ext.astype(F32)


def _query_ext(slope, t):
    r = _iota((SUBLANES, t.shape[1]), 0)
    big = slope * float(POS_SPLIT)
    hi = _div_pow2(t, POS_SPLIT).astype(F32)
    lo = (t & (POS_SPLIT - 1)).astype(F32)
    return jnp.where(r == 0, big, jnp.where(r == 1, slope, jnp.where(r == 2, -(big * hi),
                     jnp.where(r == 3, -(slope * lo), jnp.where(r == EXT_PAD_LANE, NEG, 0.0)))))


def _softmax_cols(s):
    p = jnp.exp(s - jnp.max(s, axis=0, keepdims=True))
    return p, 1.0 / jnp.sum(p, axis=0, keepdims=True)


def _rank_before(score, n, idx):
    rank = jnp.zeros(score.shape, jnp.int32)
    for m in range(n):
        row = score[m:m + 1, :]
        better = (row > score) | ((row == score) & (m < idx))
        rank = rank + better.astype(jnp.int32)
    return rank


V_ROWS = HEAD_DIM + 2 * SUBLANES


def _with_ones(vt):
    pad = (_iota((V_ROWS - HEAD_DIM, vt.shape[1]), 0) == 0).astype(vt.dtype)
    return jnp.concatenate([vt, pad], axis=0)


def _moba_kernel(slopes_ref, q_ref, k_ref, v_ref, o_ref,
                 kmean_ref, kb_ref, vt_ref, row_ref, *, nb, kk, tq):
    hp = pl.program_id(1)
    ti = pl.program_id(2)
    blk = MOBA_BLOCK
    per = tq // blk
    cols = HEADS_PER_TILE * tq

    @pl.when(ti == 0)
    def _():
        kmean_ref[...] = jnp.zeros_like(kmean_ref)
        for n in range(nb):
            rows = slice(n * blk, (n + 1) * blk)
            kmean_ref[n:n + 1, :] = jnp.mean(k_ref[0, rows, :], axis=0, keepdims=True)
            vt = v_ref[0, rows, :].T.astype(vt_ref.dtype)
            for h in range(HEADS_PER_TILE):
                vt_ref[h, :, rows] = _with_ones(vt[h * HEAD_DIM:(h + 1) * HEAD_DIM])
        kb_ref[...] = k_ref[0].astype(kb_ref.dtype)

    q0 = ti * tq
    lo = _iota((tq, LANES), 1) < HEAD_DIM
    q = q_ref[0]
    qs = jnp.concatenate([jnp.where(lo, q, 0.0), jnp.where(lo, 0.0, q)], axis=0)
    col1 = _iota((1, cols), 1)
    slope = jnp.where(col1 < tq, slopes_ref[HEADS_PER_TILE * hp], slopes_ref[HEADS_PER_TILE * hp + 1])
    i1 = col1 & (tq - 1)
    own = ti * per + _div_pow2(i1, blk)

    gate = _dot_nt(kmean_ref[...], qs, precision=HI)
    n_idx = _iota(gate.shape, 0)
    cand = n_idx < own
    gm = jnp.where(cand, gate, NEG)
    keep = (cand & (_rank_before(gm, nb - 1, n_idx) < kk)) | (n_idx == own)
    row_ref[...] = slope * (n_idx * blk - q0).astype(F32) + jnp.where(keep, 0.0, NEG)

    qsb = (qs * SCALE).astype(MXU_DTYPE)
    rel = _iota((blk, cols), 0) - i1
    bias = slope * rel.astype(F32)

    def attend(t):
        first = per * t
        nkeys = (first + per) * blk
        s = _dot_nt(kb_ref[0:nkeys, :], qsb)
        pieces = []
        for n in range(first + per):
            piece = s[n * blk:(n + 1) * blk] + bias + row_ref[n:n + 1, :]
            if n >= first:
                piece = jnp.where(rel + (n - first) * blk <= 0, piece, NEG)
            pieces.append(piece)
        s = jnp.concatenate(pieces, axis=0)
        pb = jnp.exp((s - jnp.max(s, axis=0, keepdims=True)).astype(MXU_DTYPE))
        halves = []
        for h in range(HEADS_PER_TILE):
            acc = jnp.dot(vt_ref[h, :, 0:nkeys], pb[:, h * tq:(h + 1) * tq], preferred_element_type=F32)
            halves.append(acc[0:HEAD_DIM] * (1.0 / acc[HEAD_DIM:HEAD_DIM + 1]))
        o_ref[0] = jnp.concatenate(halves, axis=0).T.astype(o_ref.dtype)

    for t in range(nb // per):
        @pl.when(ti == t)
        def _(t=t):
            attend(t)


def _moba(slopes, proj):
    B, S, _ = proj.shape
    tq = MOBA_TILE_BLOCKS * MOBA_BLOCK
    nb = S // MOBA_BLOCK
    assert S % tq == 0
    kk = max(1, min(MOBA_TOPK, nb - 1))
    n_tiles = MOBA_HEADS // HEADS_PER_TILE
    kern = functools.partial(_moba_kernel, nb=nb, kk=kk, tq=tq)
    return pl.pallas_call(
        kern,
        grid=(B, n_tiles, S // tq),
        in_specs=[
            pl.BlockSpec(memory_space=pltpu.SMEM),
            pl.BlockSpec((1, tq, LANES), lambda b, h, i: (b, i, COL_QM + h)),
            pl.BlockSpec((1, S, LANES), lambda b, h, i: (b, 0, COL_KM + h)),
            pl.BlockSpec((1, S, LANES), lambda b, h, i: (b, 0, COL_VM + h)),
        ],
        out_specs=pl.BlockSpec((1, tq, LANES), lambda b, h, i: (b, i, h)),
        out_shape=jax.ShapeDtypeStruct((B, S, MOBA_HEADS * HEAD_DIM), MXU_DTYPE),
        scratch_shapes=[
            pltpu.VMEM((_round_up(nb, SUBLANES), LANES), F32),
            pltpu.VMEM((S, LANES), MXU_DTYPE),
            pltpu.VMEM((HEADS_PER_TILE, V_ROWS, S), MXU_DTYPE),
            pltpu.VMEM((_round_up(nb, SUBLANES), HEADS_PER_TILE * tq), F32),
        ],
        compiler_params=_params("parallel", "parallel", "arbitrary"),
        name="moba",
    )(slopes, proj, proj, proj)


def _gelu_tanh(x):
    return x * (0.5 * (1.0 + jnp.tanh(0.7978845608028654 * (x + 0.044715 * (x * x * x)))))


def _compress_kernel(xk_ref, xv_ref, pk_ref, pv_ref, w1k_ref, w1v_ref, w2k_ref, w2v_ref, kc_ref, vc_ref):
    g = pl.program_id(0)
    half = (CMP_LEN // 2) * HEAD_DIM
    for x_ref, p_ref, w1_ref, w2_ref, o_ref in ((xk_ref, pk_ref, w1k_ref, w2k_ref, kc_ref),
                                               (xv_ref, pv_ref, w1v_ref, w2v_ref, vc_ref)):
        B, nr, W = x_ref.shape
        x = x_ref[...].reshape(B * nr, W)
        top = _dot_hi(x + p_ref[:, 0:half], w1_ref[0:half, :])
        bot = _dot_hi(x + p_ref[:, half:2 * half], w1_ref[half:2 * half, :])
        hid = top + pltpu.roll(bot, B * nr - 1, 0)
        out = _dot_hi(_gelu_tanh(hid), w2_ref[...]).reshape(B, nr, LANES)

        @pl.when(g == 0)
        def _(o_ref=o_ref, out=out):
            o_ref[...] = out

        @pl.when(g > 0)
        def _(o_ref=o_ref, out=out):
            o_ref[...] += out


def _compress(xk, xv, pk, pv, w1k, w1v, w2k, w2v, l):
    G, B, nr, W = xk.shape
    H = w1k.shape[2]
    layer = lambda *shape: pl.BlockSpec((None,) + shape, lambda g: (l,) + (0,) * len(shape))
    xspec = pl.BlockSpec((None, B, nr, W), lambda g: (g, 0, 0, 0))
    w2spec = pl.BlockSpec((None, H, LANES), lambda g: (g, 0, 0))
    ospec = pl.BlockSpec((B, nr, LANES), lambda g: (0, 0, 0))
    oshape = jax.ShapeDtypeStruct((B, nr, LANES), F32)
    return pl.pallas_call(
        _compress_kernel,
        grid=(G,),
        in_specs=[xspec, xspec, layer(1, 2 * W), layer(1, 2 * W), layer(2 * W, H), layer(2 * W, H),
                  w2spec, w2spec],
        out_specs=(ospec, ospec),
        out_shape=(oshape, oshape),
        compiler_params=_params("arbitrary"),
        name="nsa_compress",
    )(xk, xv, pk, pv, w1k, w1v, w2k, w2v)


def _nsa_kernel(slopes_ref, qa_ref, qb_ref, g_ref, kc_ref, vc_ref, ks_ref, vs_ref, kw_ref, vw_ref,
                o_ref, kcx, vct, ksx, vst, kwx, vwt, qx_ref, gt_ref, ocmp_ref, owin_ref,
                *, S, tq, n_cmp, n_slc, k_top):
    g = pl.program_id(1)
    qi = pl.program_id(2)
    q0 = qi * tq
    cols = NSA_GROUP * tq
    sblk = SLC_CHUNK
    nr = kcx.shape[0]
    nj = _round_up(n_slc, 2 * SUBLANES)
    n_back = WINDOW // tq
    ext0 = HEAD_DIM

    @pl.when(qi == 0)
    def _():
        def both_halves(x):
            mine = (_iota(x.shape, 1) >= HEAD_DIM) == (g == 1)
            return jnp.where(mine, x, pltpu.roll(x, HEAD_DIM, 1))

        def with_ext(x, pos, mask_block=0):
            lane = _iota(x.shape, 1)
            return jnp.where(lane < HEAD_DIM, both_halves(x), _key_ext(pos, lane - ext0, mask_block)).astype(MXU_DTYPE)

        def transposed(x):
            return both_halves(x).T[0:HEAD_DIM].astype(MXU_DTYPE)

        qx_ref[...] = jnp.zeros_like(qx_ref)
        kcx[...] = with_ext(kc_ref[0], _iota((nr, LANES), 0) * CMP_STRIDE + (CMP_LEN - 1))
        vct[...] = transposed(vc_ref[0])
        lane = _iota((WINDOW, LANES), 1)
        kwx[0:WINDOW, :] = _key_ext(lane * 0, lane - ext0, pad=lane >= 0).astype(MXU_DTYPE)
        for c in range(n_back):
            vwt[c] = jnp.zeros(vwt.shape[1:], vwt.dtype)
        for c in range(S // sblk):
            rows = slice(c * sblk, (c + 1) * sblk)
            pos = c * sblk + _iota((sblk, LANES), 0)
            ksx[rows, :] = with_ext(ks_ref[0, rows, :], pos, SLC_BLOCK)
            kwx[WINDOW + c * sblk:WINDOW + (c + 1) * sblk, :] = with_ext(kw_ref[0, rows, :], pos)
            vst[:, rows] = transposed(vs_ref[0, rows, :])
        for c in range(S // tq):
            vwt[n_back + c] = transposed(vw_ref[0, c * tq:(c + 1) * tq, :])

    qat = qa_ref[0].T
    qbt = qb_ref[0].T
    qt = jnp.concatenate([qat[0:HEAD_DIM], qat[HEAD_DIM:LANES], qbt[0:HEAD_DIM], qbt[HEAD_DIM:LANES]], axis=1)
    col1 = _iota((1, cols), 1)
    h0 = NSA_GROUP * g
    slope = jnp.where(col1 < tq, slopes_ref[h0],
                      jnp.where(col1 < 2 * tq, slopes_ref[h0 + 1],
                                jnp.where(col1 < 3 * tq, slopes_ref[h0 + 2], slopes_ref[h0 + 3])))
    i1 = col1 & (tq - 1)
    t1 = q0 + i1
    qx_ref[0:HEAD_DIM, :] = (qt * SCALE).astype(qx_ref.dtype)
    qx_ref[ext0:ext0 + EXT_MASK_LANE, :] = jnp.concatenate(
        [_query_ext(slope, t1), jnp.zeros((EXT_MASK_LANE - SUBLANES, cols), F32)], axis=0).astype(qx_ref.dtype)
    qx = qx_ref[...]

    relw = _iota((tq, cols), 0) - i1
    s = jnp.dot(kwx[pl.ds(pl.multiple_of(q0, tq), WINDOW + tq), :], qx, preferred_element_type=F32)
    s = jnp.concatenate([jnp.where(relw > 0, s[0:tq], NEG), s[tq:WINDOW],
                         jnp.where(relw <= 0, s[WINDOW:WINDOW + tq], NEG)], axis=0)
    p, inv = _softmax_cols(s)
    pb = p.astype(MXU_DTYPE)
    acc = _dot(vwt[qi], pb[0:tq])
    for e in range(1, n_back + 1):
        acc = acc + _dot(vwt[qi + e], pb[e * tq:(e + 1) * tq])
    owin_ref[...] = acc * inv

    nidx = _iota((nr, cols), 0)
    maskc = (nidx * CMP_STRIDE + (CMP_LEN - 1) <= t1) & (nidx < n_cmp)
    s = jnp.where(maskc, jnp.dot(kcx[...], qx, preferred_element_type=F32), NEG)
    p = jnp.where(maskc, jnp.exp(s - jnp.max(s, axis=0, keepdims=True)), 0.0)
    p = p / jnp.maximum(jnp.sum(p, axis=0, keepdims=True), 1e-30)
    ocmp_ref[...] = _dot(vct[...], p)

    psum = p[:, 0:tq] + p[:, tq:2 * tq] + p[:, 2 * tq:3 * tq] + p[:, 3 * tq:4 * tq]
    jj = _iota((nj, nr), 0) * SLC_BLOCK
    nn = _iota((nj, nr), 1) * CMP_STRIDE
    overlap = ((nn < jj + SLC_BLOCK) & (nn + CMP_LEN > jj)
               & (nn < n_cmp * CMP_STRIDE) & (jj < n_slc * SLC_BLOCK)).astype(F32)
    imp = _dot_hi(overlap, psum)
    jl = _iota((nj, tq), 0)
    tb = _div_pow2(q0 + _iota((nj, tq), 1), SLC_BLOCK)
    cand = jl <= tb
    forced = (jl == 0) | (jl == tb) | (jl == tb - 1)
    score = jnp.where(cand, jnp.where(forced, SLC_FORCE, imp), NEG)
    mask0 = ext0 + EXT_MASK_LANE

    def set_mask(rows_ranked):
        keep = cand if rows_ranked <= k_top else cand & (_rank_before(score, rows_ranked, jl) < k_top)
        qx_ref[mask0:mask0 + nj, :] = jnp.concatenate([jnp.where(keep, 0.0, NEG)] * NSA_GROUP, axis=1).astype(qx_ref.dtype)

    z = g_ref[0].T
    gt_ref[...] = 1.0 / (1.0 + jnp.exp(-z))

    def combine(oslc):
        ocmp = ocmp_ref[...]
        owin = owin_ref[...]
        heads = []
        for r in range(NSA_GROUP):
            sl = slice(r * tq, (r + 1) * tq)
            c = NSA_BRANCHES * (NSA_GROUP * g + r)
            heads.append(gt_ref[pl.ds(c, 1), :] * ocmp[:, sl] + gt_ref[pl.ds(c + 1, 1), :] * oslc[:, sl]
                         + gt_ref[pl.ds(c + 2, 1), :] * owin[:, sl])
        o_ref[0, :, 0:LANES] = jnp.concatenate(heads[0:2], axis=0).T.astype(o_ref.dtype)
        o_ref[0, :, LANES:2 * LANES] = jnp.concatenate(heads[2:4], axis=0).T.astype(o_ref.dtype)

    def slc_attend(c):
        nkeys = (c + 1) * sblk
        set_mask(min(nkeys // SLC_BLOCK, n_slc))
        s = jnp.dot(ksx[0:nkeys, :], qx_ref[...], preferred_element_type=F32)
        causal = _iota((sblk, cols), 0) + (c * sblk - q0) <= i1
        diag = jnp.where(causal, s[c * sblk:nkeys], NEG)
        s = jnp.concatenate([s[0:c * sblk], diag], axis=0) if c else diag
        p, inv = _softmax_cols(s)
        combine(_dot(vst[:, 0:nkeys], p) * inv)

    dc = _div_pow2(q0, sblk)
    for c in range(S // sblk):
        @pl.when(dc == c)
        def _(c=c):
            slc_attend(c)


def _nsa(slopes, proj, kc, vc):
    B, S, _ = proj.shape
    tq = NSA_QTILE
    n_cmp = (S - CMP_LEN) // CMP_STRIDE + 1
    n_slc = S // SLC_BLOCK
    k_top = min(SLC_TOPK, n_slc)
    nr = kc.shape[1]
    cols = NSA_GROUP * tq
    assert HEAD_DIM + EXT_MASK_LANE + _round_up(n_slc, 2 * SUBLANES) <= LANES and S // POS_SPLIT <= POS_SPLIT
    assert SLC_CHUNK % tq == 0 and WINDOW % tq == 0 and S % SLC_CHUNK == 0
    kern = functools.partial(_nsa_kernel, S=S, tq=tq, n_cmp=n_cmp, n_slc=n_slc, k_top=k_top)
    qtiles = NSA_GROUP // HEADS_PER_TILE

    def seq(col):
        return pl.BlockSpec((1, S, LANES), lambda b, g, i: (b, 0, col))

    small = pl.BlockSpec((1, nr, LANES), lambda b, g, i: (b, 0, 0))
    return pl.pallas_call(
        kern,
        grid=(B, NSA_KV_HEADS, S // tq),
        in_specs=[
            pl.BlockSpec(memory_space=pltpu.SMEM),
            pl.BlockSpec((1, tq, LANES), lambda b, g, i: (b, i, COL_QN + qtiles * g)),
            pl.BlockSpec((1, tq, LANES), lambda b, g, i: (b, i, COL_QN + qtiles * g + 1)),
            pl.BlockSpec((1, tq, LANES), lambda b, g, i: (b, i, COL_G)),
            small, small, seq(COL_KS), seq(COL_VS), seq(COL_KW), seq(COL_VW),
        ],
        out_specs=pl.BlockSpec((1, tq, qtiles * LANES), lambda b, g, i: (b, i, g)),
        out_shape=jax.ShapeDtypeStruct((B, S, NSA_HEADS * HEAD_DIM), MXU_DTYPE),
        scratch_shapes=[
            pltpu.VMEM((nr, LANES), MXU_DTYPE), pltpu.VMEM((HEAD_DIM, nr), MXU_DTYPE),
            pltpu.VMEM((S, LANES), MXU_DTYPE), pltpu.VMEM((HEAD_DIM, S), MXU_DTYPE),
            pltpu.VMEM((WINDOW + S, LANES), MXU_DTYPE),
            pltpu.VMEM((WINDOW // tq + S // tq, HEAD_DIM, tq), MXU_DTYPE),
            pltpu.VMEM((LANES, cols), MXU_DTYPE),
            pltpu.VMEM((LANES, tq), F32),
            pltpu.VMEM((HEAD_DIM, cols), F32), pltpu.VMEM((HEAD_DIM, cols), F32),
        ],
        compiler_params=_params("parallel", "parallel", "arbitrary"),
        name="nsa",
    )(slopes, proj, proj, proj, kc, vc, proj, proj, proj, proj)


def _outproj_kernel(om_ref, on_ref, x_ref, gm_ref, gn_ref, wm_ref, wn_ref, o_ref):
    om = om_ref[...].astype(F32)
    on = on_ref[...].astype(F32)
    y = _dot(_rms(om, gm_ref[...]), wm_ref[...]) + _dot(_rms(on, gn_ref[...]), wn_ref[...])
    o_ref[...] = x_ref[...] + y


def _outproj(om, on, x, gm, gn, w, l, tm):
    T, D = x.shape
    Wm, Wn = om.shape[1], on.shape[1]
    assert Wm == Wn and w.shape[1] == Wm + Wn
    return pl.pallas_call(
        _outproj_kernel,
        grid=(T // tm,),
        in_specs=[
            pl.BlockSpec((tm, Wm), lambda i: (i, 0)),
            pl.BlockSpec((tm, Wn), lambda i: (i, 0)),
            pl.BlockSpec((tm, D), lambda i: (i, 0)),
            pl.BlockSpec((None, 1, Wm), lambda i: (l, 0, 0)),
            pl.BlockSpec((None, 1, Wn), lambda i: (l, 0, 0)),
            pl.BlockSpec((None, Wm, D), lambda i: (l, 0, 0)),
            pl.BlockSpec((None, Wn, D), lambda i: (l, 1, 0)),
        ],
        out_specs=pl.BlockSpec((tm, D), lambda i: (i, 0)),
        out_shape=jax.ShapeDtypeStruct((T, D), F32),
        compiler_params=_params("parallel"),
        name="outproj",
    )(om, on, x, gm, gn, w, w)


def _alibi_slopes(n):
    slopes = 2.0 ** (-8.0 * np.arange(1, n + 1) / n)
    assert np.all(np.log2(slopes) == np.round(np.log2(slopes)))
    return jnp.asarray(slopes, dtype=F32)


def _token_tile(T, want):
    return want if T % want == 0 else T


def _mixer(x, B, S, l, mix_norm, w_in, pos_k, k_w1, k_w2, pos_v, v_w1, v_w2, moba_norm, nsa_norm, w_out):
    T, D = x.shape
    proj, ck, cv = _inproj(x, mix_norm, w_in, l, _token_tile(T, PROJ_TOKENS))
    proj = proj.reshape(B, S, IN_PAD)

    o_m = _moba(_alibi_slopes(MOBA_HEADS), proj)

    def windows(slab):
        return slab.reshape(NSA_KV_HEADS, B, S // CMP_STRIDE, CMP_STRIDE * HEAD_DIM)

    def placed(w2):
        return jnp.stack([jnp.pad(w2, ((0, 0), (g * HEAD_DIM, LANES - (g + 1) * HEAD_DIM)))
                          for g in range(NSA_KV_HEADS)])

    kc, vc = _compress(windows(ck), windows(cv), pos_k, pos_v, k_w1, v_w1, placed(k_w2), placed(v_w2), l)
    o_n = _nsa(_alibi_slopes(NSA_HEADS), proj, kc, vc)

    return _outproj(o_m.reshape(T, -1), o_n.reshape(T, -1), x, moba_norm, nsa_norm, w_out, l,
                    _token_tile(T, OUT_TOKENS))


def _swiglu_step(x, l, norm, w_gate, w_up, w_down, final_g, final=False):
    T = x.shape[0]
    return _ffn(x, norm, w_gate, w_up, w_down, l, _token_tile(T, FFN_TOKENS), FFN_CHUNK, final_g, final)


def _rows(p):
    return p.reshape(p.shape[0], 1, -1)


@jax.jit
def kernel(x, ffa_norm, ffa_w_gate, ffa_w_up, ffa_w_down, mix_norm, w_in, cmp_pos_k, cmp_k_w1, cmp_k_w2,
           cmp_pos_v, cmp_v_w1, cmp_v_w2, moba_out_norm, nsa_out_norm, w_out, ffb_norm, ffb_w_gate,
           ffb_w_up, ffb_w_down, final_norm):
    B, S, D = x.shape
    assert S % MOBA_BLOCK == 0 and S >= WINDOW and w_in.shape[-1] == IN_WIDTH
    h = x.reshape(B * S, D)
    ffa_norm, ffb_norm, mix_norm = _rows(ffa_norm), _rows(ffb_norm), _rows(mix_norm)
    moba_out_norm, nsa_out_norm = _rows(moba_out_norm), _rows(nsa_out_norm)
    cmp_pos_k, cmp_pos_v = _rows(cmp_pos_k), _rows(cmp_pos_v)
    depth = ffa_norm.shape[0]
    final_g = final_norm[None]
    for l in range(depth):
        h = _swiglu_step(h, l, ffa_norm, ffa_w_gate, ffa_w_up, ffa_w_down, final_g)
        h = _mixer(h, B, S, l, mix_norm, w_in, cmp_pos_k, cmp_k_w1, cmp_k_w2[l],
                   cmp_pos_v, cmp_v_w1, cmp_v_w2[l], moba_out_norm, nsa_out_norm, w_out)
        h = _swiglu_step(h, l, ffb_norm, ffb_w_gate, ffb_w_up, ffb_w_down, final_g, final=l == depth - 1)
    return h.reshape(B, S, D)
```
